```python
import jax, jax.numpy as jnp
from jax import lax
import numpy as np

D_MODEL = 2048
BATCH = 2
SEQ = 4096
DEPTH = 2

N_MIX_HEADS = 12
HEAD_DIM = 128
MIX_WIDTH = N_MIX_HEADS * HEAD_DIM
MEM_HEADS = 4
MEM_HEAD_DIM = 128
MEM_WIDTH = MEM_HEADS * MEM_HEAD_DIM
N_MEM = 256
CONV_WIDTH = 3
Q_LORA = 512
KV_LORA = 256
QK_NOPE = 128
QK_ROPE = 64
QK_HEAD = QK_NOPE + QK_ROPE
V_HEAD = 128
ROPE_THETA = 10000.0
Q_BLOCK = 128
D_FF = 7168
N_EXPERTS = 8
TOP_K = 2
EPS = 1e-6
N_A = DEPTH // 2
N_B = DEPTH - N_A
N_DENSE = (DEPTH + 1) // 2
N_MOE = DEPTH // 2

kernel_name = "yoco_shortconv_mla_memxattn_moe"


def rms_norm(x, g):
    xf = x.astype(jnp.float32)
    y = xf * lax.rsqrt(jnp.mean(xf * xf, axis=-1, keepdims=True) + EPS)
    return (y * g.astype(jnp.float32)).astype(x.dtype)


def rope_tables(positions):
    inv_freq = 1.0 / (ROPE_THETA ** (jnp.arange(0, QK_ROPE, 2, dtype=jnp.float32) / QK_ROPE))
    ang = positions.astype(jnp.float32)[..., None] * inv_freq
    return jnp.cos(ang)[:, :, None, :], jnp.sin(ang)[:, :, None, :]


def apply_rope(x, cos, sin):
    x1, x2 = jnp.split(x.astype(jnp.float32), 2, axis=-1)
    out = jnp.concatenate([x1 * cos - x2 * sin, x2 * cos + x1 * sin], axis=-1)
    return out.astype(x.dtype)


def swiglu(h, w_gate, w_up, w_down):
    return (jax.nn.silu(h @ w_gate) * (h @ w_up)) @ w_down


def short_conv_mixer(u_a, conv_w):
    xin, gate_b, gate_c = jnp.split(u_a, 3, axis=-1)
    z = gate_c * xin
    y = lax.conv_general_dilated(
        z, conv_w[:, None, :].astype(z.dtype), window_strides=(1,),
        padding=[(CONV_WIDTH - 1, 0)],
        dimension_numbers=('NWC', 'WIO', 'NWC'),
        feature_group_count=MIX_WIDTH)
    return gate_b * y


def memory_attention(u_m, mem, g_mem, w_mem_kv, g_mq, g_mk):
    b, s, _ = u_m.shape
    m = mem.shape[1]
    q = rms_norm(u_m.reshape(b, s, MEM_HEADS, MEM_HEAD_DIM), g_mq)
    kv = rms_norm(mem, g_mem) @ w_mem_kv
    k, v = jnp.split(kv, 2, axis=-1)
    k = rms_norm(k.reshape(b, m, MEM_HEADS, MEM_HEAD_DIM), g_mk)
    v = v.reshape(b, m, MEM_HEADS, MEM_HEAD_DIM)
    sc = jnp.einsum('bshd,bmhd->bhsm', q, k).astype(jnp.float32) * (MEM_HEAD_DIM ** -0.5)
    p = jax.nn.softmax(sc, axis=-1).astype(v.dtype)
    return jnp.einsum('bhsm,bmhd->bshd', p, v).reshape(b, s, MEM_WIDTH)


def shared_mla_kv(x, g_kv, w_kv_a, g_kv_a, w_kv_b, g_kn, cos, sin):
    b, s, _ = x.shape
    kv_a = rms_norm(x, g_kv) @ w_kv_a
    c_kv = rms_norm(kv_a[..., :KV_LORA], g_kv_a)
    k_pe = kv_a[..., KV_LORA:]
    kv = (c_kv @ w_kv_b).reshape(b, s, N_MIX_HEADS, QK_NOPE + V_HEAD)
    k_nope, v = kv[..., :QK_NOPE], kv[..., QK_NOPE:]
    k_pe = jnp.broadcast_to(k_pe[:, :, None, :], (b, s, N_MIX_HEADS, QK_ROPE))
    k = rms_norm(jnp.concatenate([k_nope, k_pe], axis=-1), g_kn)
    k = jnp.concatenate([k[..., :QK_NOPE], apply_rope(k[..., QK_NOPE:], cos, sin)], axis=-1)
    return k, v


def mla_query(u_q, g_q_a, w_q_b, g_qn, cos, sin):
    b, s, _ = u_q.shape
    c_q = rms_norm(u_q, g_q_a)
    q = (c_q @ w_q_b).reshape(b, s, N_MIX_HEADS, QK_HEAD)
    q = rms_norm(q, g_qn)
    return jnp.concatenate([q[..., :QK_NOPE], apply_rope(q[..., QK_NOPE:], cos, sin)], axis=-1)


def causal_block_attention(q, k, v):
    b, s, h, dq = q.shape
    nb = s // Q_BLOCK
    qb = q.reshape(b, nb, Q_BLOCK, h, dq).transpose(1, 0, 2, 3, 4)
    k_idx = jnp.arange(s)
    scale = QK_HEAD ** -0.5

    def one_block(args):
        q_blk, i = args
        sc = jnp.einsum('bqhd,bkhd->bhqk', q_blk, k).astype(jnp.float32) * scale
        q_idx = i * Q_BLOCK + jnp.arange(Q_BLOCK)
        sc = jnp.where(k_idx[None, :] <= q_idx[:, None], sc, -jnp.inf)
        p = jax.nn.softmax(sc, axis=-1).astype(v.dtype)
        return jnp.einsum('bhqk,bkhd->bqhd', p, v)

    out = lax.map(one_block, (qb, jnp.arange(nb)))
    return out.transpose(1, 0, 2, 3, 4).reshape(b, s, h * v.shape[-1])


def moe_swiglu(h, w_router, w_gate, w_up, w_down):
    b, s, d = h.shape
    t = h.reshape(b * s, d)
    logits = t.astype(jnp.float32) @ w_router.astype(jnp.float32)
    top_val, top_idx = lax.top_k(logits, TOP_K)
    top_w = jax.nn.softmax(top_val, axis=-1)
    gates = jnp.sum(jax.nn.one_hot(top_idx, N_EXPERTS, dtype=jnp.float32) * top_w[..., None], axis=1)
    gates = gates.astype(t.dtype)
    out = jnp.zeros_like(t)
    for e in range(N_EXPERTS):
        out = out + gates[:, e:e + 1] * swiglu(t, w_gate[e], w_up[e], w_down[e])
    return out.reshape(b, s, d)


def setup_inputs(seed: int = 0) -> dict:
    key = jax.random.key(seed)
    ks = iter(jax.random.split(key, 40))
    f32 = jnp.float32

    def w(shape, fan_in):
        return jax.random.normal(next(ks), shape, f32) * (fan_in ** -0.5)

    def gain(shape):
        return 1.0 + 0.02 * jax.random.normal(next(ks), shape, f32)

    x = jax.random.normal(next(ks), (BATCH, SEQ, D_MODEL), f32)
    mem = jax.random.normal(next(ks), (BATCH, N_MEM, D_MODEL), f32)
    positions = (jnp.arange(SEQ, dtype=jnp.int32)[None, :]
                 + jax.random.randint(next(ks), (BATCH, 1), 0, 1024, dtype=jnp.int32))
    return {
        "x": x,
        "mem": mem,
        "positions": positions,
        "g_mix": gain((DEPTH, D_MODEL)),
        "g_ffn": gain((DEPTH, D_MODEL)),
        "g_mem": gain((DEPTH, D_MODEL)),
        "w_mem_kv": w((DEPTH, D_MODEL, 2 * MEM_WIDTH), D_MODEL),
        "g_mq": gain((DEPTH, MEM_HEAD_DIM)),
        "g_mk": gain((DEPTH, MEM_HEAD_DIM)),
        "w_out": w((DEPTH, MIX_WIDTH + MEM_WIDTH, D_MODEL), MIX_WIDTH + MEM_WIDTH),
        "a_w_in": w((N_A, D_MODEL, 3 * MIX_WIDTH + MEM_WIDTH), D_MODEL),
        "a_conv_w": w((N_A, CONV_WIDTH, MIX_WIDTH), CONV_WIDTH),
        "b_w_in": w((N_B, D_MODEL, Q_LORA + MEM_WIDTH), D_MODEL),
        "b_g_q_a": gain((N_B, Q_LORA)),
        "b_w_q_b": w((N_B, Q_LORA, N_MIX_HEADS * QK_HEAD), Q_LORA),
        "b_g_qn": gain((N_B, QK_HEAD)),
        "g_kv": gain((D_MODEL,)),
        "w_kv_a": w((D_MODEL, KV_LORA + QK_ROPE), D_MODEL),
        "g_kv_a": gain((KV_LORA,)),
        "w_kv_b": w((KV_LORA, N_MIX_HEADS * (QK_NOPE + V_HEAD)), KV_LORA),
        "g_kn": gain((QK_HEAD,)),
        "ffn_w_gate": w((N_DENSE, D_MODEL, D_FF), D_MODEL),
        "ffn_w_up": w((N_DENSE, D_MODEL, D_FF), D_MODEL),
        "ffn_w_down": w((N_DENSE, D_FF, D_MODEL), D_FF),
        "moe_w_router": w((N_MOE, D_MODEL, N_EXPERTS), D_MODEL),
        "moe_w_gate": w((N_MOE, N_EXPERTS, D_MODEL, D_FF), D_MODEL),
        "moe_w_up": w((N_MOE, N_EXPERTS, D_MODEL, D_FF), D_MODEL),
        "moe_w_down": w((N_MOE, N_EXPERTS, D_FF, D_MODEL), D_FF),
    }


def reference(x, mem, positions, g_mix, g_ffn, g_mem, w_mem_kv, g_mq, g_mk, w_out,
              a_w_in, a_conv_w, b_w_in, b_g_q_a, b_w_q_b, b_g_qn,
              g_kv, w_kv_a, g_kv_a, w_kv_b, g_kn,
              ffn_w_gate, ffn_w_up, ffn_w_down,
              moe_w_router, moe_w_gate, moe_w_up, moe_w_down):
    cos, sin = rope_tables(positions)
    k_shared = None
    v_shared = None
    for l in range(DEPTH):
        h = rms_norm(x, g_mix[l])
        if l < N_A:
            u = h @ a_w_in[l]
            mix = short_conv_mixer(u[..., :3 * MIX_WIDTH], a_conv_w[l])
            u_m = u[..., 3 * MIX_WIDTH:]
        else:
            if l == N_A:
                k_shared, v_shared = shared_mla_kv(x, g_kv, w_kv_a, g_kv_a, w_kv_b, g_kn, cos, sin)
            j = l - N_A
            u = h @ b_w_in[j]
            q = mla_query(u[..., :Q_LORA], b_g_q_a[j], b_w_q_b[j], b_g_qn[j], cos, sin)
            mix = causal_block_attention(q, k_shared, v_shared)
            u_m = u[..., Q_LORA:]
        mem_out = memory_attention(u_m, mem, g_mem[l], w_mem_kv[l], g_mq[l], g_mk[l])
        x = x + jnp.concatenate([mix, mem_out], axis=-1) @ w_out[l]
        h = rms_norm(x, g_ffn[l])
        if l % 2 == 0:
            i = l // 2
            x = x + swiglu(h, ffn_w_gate[i], ffn_w_up[i], ffn_w_down[i])
        else:
            i = l // 2
            x = x + moe_swiglu(h, moe_w_router[i], moe_w_gate[i], moe_w_up[i], moe_w_down[i])
    return x
```

```python
import functools

import jax
import jax.numpy as jnp
from jax import lax
from jax.experimental import pallas as pl
from jax.experimental.pallas import tpu as pltpu

D_MODEL = 2048
BATCH = 2
SEQ = 4096
N_TOK = BATCH * SEQ
N_HEADS = 12
HEAD_DIM = 128
MIX_WIDTH = N_HEADS * HEAD_DIM
MEM_HEADS = 4
MEM_WIDTH = MEM_HEADS * HEAD_DIM
N_MEM = 256
CONV_WIDTH = 3
Q_LORA = 512
KV_LORA = 256
QK_NOPE = 128
QK_ROPE = 64
QK_HEAD = QK_NOPE + QK_ROPE
ROPE_THETA = 10000.0
D_FF = 7168
N_EXPERTS = 8
TOP_K = 2
EPS = 1e-6

LANES = 128
HEAD_PAD = 2 * LANES
HALF_D = D_MODEL // 2
VMEM_LIMIT = 56 * 1024 * 1024

FFN_SUB = 256
FFN_SUPER = 9
FFN_TF = 256
MOE_ROWS = N_TOK * TOP_K + N_EXPERTS * FFN_SUB
MOE_SUPERS = N_EXPERTS + -(-MOE_ROWS // (FFN_SUB * FFN_SUPER))

f32 = jnp.float32
bf16 = jnp.bfloat16


def _cparams(sem):
    return pltpu.CompilerParams(dimension_semantics=sem, vmem_limit_bytes=VMEM_LIMIT)


def _rms(x, g, n=None):
    n = x.shape[-1] if n is None else n
    ms = jnp.sum(x * x, axis=-1, keepdims=True) * (1.0 / n)
    return x * lax.rsqrt(ms + EPS) * g


def _pack_halves(h):
    hb = h.astype(bf16).astype(f32)
    lo = lax.bitcast_convert_type(hb[:, :HALF_D], jnp.uint32) >> 16
    hi = lax.bitcast_convert_type(hb[:, HALF_D:], jnp.uint32) & jnp.uint32(0xFFFF0000)
    return lo | hi


def _unpack_halves(p):
    lo = lax.bitcast_convert_type(p << 16, f32).astype(bf16)
    hi = lax.bitcast_convert_type(p & jnp.uint32(0xFFFF0000), f32).astype(bf16)
    return lo, hi


def _dot(a, b):
    return jnp.dot(a, b, preferred_element_type=f32)


def _dot_nt(a, b):
    return lax.dot_general(a, b, (((1,), (1,)), ((), ())), preferred_element_type=f32)


def _norm_matmul_kernel(x_ref, g_ref, w_ref, o_ref, h_ref):
    @pl.when(pl.program_id(1) == 0)
    def _():
        h_ref[...] = _rms(x_ref[...], g_ref[...]).astype(bf16)

    o_ref[...] = _dot(h_ref[...], w_ref[...]).astype(o_ref.dtype)


def norm_matmul(x, g, w, tm, tn, out_dtype=bf16):
    m, d = x.shape
    n = w.shape[1]
    return pl.pallas_call(
        _norm_matmul_kernel,
        grid=(m // tm, n // tn),
        in_specs=[pl.BlockSpec((tm, d), lambda i, j: (i, 0)),
                  pl.BlockSpec((1, d), lambda i, j: (0, 0)),
                  pl.BlockSpec((d, tn), lambda i, j: (0, j))],
        out_specs=pl.BlockSpec((tm, tn), lambda i, j: (i, j)),
        out_shape=jax.ShapeDtypeStruct((m, n), out_dtype),
        scratch_shapes=[pltpu.VMEM((tm, d), bf16)],
        compiler_params=_cparams(("arbitrary", "arbitrary")),
        name="norm_matmul",
    )(x, g, w)


def _mem_attention(um_ref, kvm_ref, gq_ref, gk_ref, cat_ref):
    scale = HEAD_DIM ** -0.5
    for h in range(MEM_HEADS):
        cs = slice(h * HEAD_DIM, (h + 1) * HEAD_DIM)
        q = _rms(um_ref[:, cs].astype(f32), gq_ref[...]).astype(bf16)
        k = _rms(kvm_ref[:, cs].astype(f32), gk_ref[...]).astype(bf16)
        v = kvm_ref[:, MEM_WIDTH + h * HEAD_DIM:MEM_WIDTH + (h + 1) * HEAD_DIM]
        s = _dot_nt(q, k) * scale
        p = jnp.exp(s - jnp.max(s, axis=-1, keepdims=True))
        o = _dot(p.astype(bf16), v) / jnp.sum(p, axis=-1, keepdims=True)
        cat_ref[:, MIX_WIDTH + h * HEAD_DIM:MIX_WIDTH + (h + 1) * HEAD_DIM] = o.astype(bf16)


def _tail_epilogue(cat_ref, wout_ref, x_ref, gffn_ref, xo_ref, hp_ref):
    xn = x_ref[...] + _dot(cat_ref[...], wout_ref[...])
    xo_ref[...] = xn
    hp_ref[...] = _pack_halves(_rms(xn, gffn_ref[...]))


def _conv_tail_kernel(xin_ref, gb_ref, gc_ref, um_ref, hx_ref, hg_ref, cw_ref, kvm_ref, gq_ref, gk_ref,
                      wout_ref, x_ref, gffn_ref, xo_ref, hp_ref, cat_ref, *, tm):
    i = pl.program_id(0)
    first = (i * tm) % SEQ == 0
    rows = lax.broadcasted_iota(jnp.int32, (tm, 1), 0)
    cc = 512
    for c in range(MIX_WIDTH // cc):
        cs = slice(c * cc, (c + 1) * cc)
        z = gc_ref[:, cs].astype(f32) * xin_ref[:, cs].astype(f32)
        zh = hg_ref[:, cs].astype(f32) * hx_ref[:, cs].astype(f32)
        zh = jnp.where(first, 0.0, zh)
        p1 = zh[15:16, :]
        p2 = zh[14:15, :]
        z1 = jnp.where(rows == 0, p1, pltpu.roll(z, 1, 0))
        z2 = jnp.where(rows == 0, p2, jnp.where(rows == 1, p1, pltpu.roll(z, 2, 0)))
        y = cw_ref[0:1, cs] * z2 + cw_ref[1:2, cs] * z1 + cw_ref[2:3, cs] * z
        cat_ref[:, cs] = (gb_ref[:, cs].astype(f32) * y).astype(bf16)
    _mem_attention(um_ref, kvm_ref, gq_ref, gk_ref, cat_ref)
    _tail_epilogue(cat_ref, wout_ref, x_ref, gffn_ref, xo_ref, hp_ref)


def _attn_tail_kernel(mix_ref, um_ref, kvm_ref, gq_ref, gk_ref, wout_ref, x_ref, gffn_ref,
                      xo_ref, hp_ref, cat_ref):
    cat_ref[:, :MIX_WIDTH] = mix_ref[...]
    _mem_attention(um_ref, kvm_ref, gq_ref, gk_ref, cat_ref)
    _tail_epilogue(cat_ref, wout_ref, x_ref, gffn_ref, xo_ref, hp_ref)


def _tail_common_specs(tm):
    per_seq = SEQ // tm
    return [pl.BlockSpec((N_MEM, 2 * MEM_WIDTH), lambda i: (i // per_seq, 0)),
            pl.BlockSpec((1, HEAD_DIM), lambda i: (0, 0)),
            pl.BlockSpec((1, HEAD_DIM), lambda i: (0, 0)),
            pl.BlockSpec((D_MODEL, D_MODEL), lambda i: (0, 0)),
            pl.BlockSpec((tm, D_MODEL), lambda i: (i, 0)),
            pl.BlockSpec((1, D_MODEL), lambda i: (0, 0))]


def _tail_outs(tm):
    return dict(
        out_specs=[pl.BlockSpec((tm, D_MODEL), lambda i: (i, 0)),
                   pl.BlockSpec((tm, HALF_D), lambda i: (i, 0))],
        out_shape=[jax.ShapeDtypeStruct((N_TOK, D_MODEL), f32),
                   jax.ShapeDtypeStruct((N_TOK, HALF_D), jnp.uint32)],
        scratch_shapes=[pltpu.VMEM((tm, D_MODEL), bf16)],
        compiler_params=_cparams(("arbitrary",)),
    )


def conv_tail(u, conv_w, kvm, g_mq, g_mk, w_out, x, g_ffn, tm=256):
    halo = 16
    um_blk = 3 * MIX_WIDTH // MEM_WIDTH
    prev = lambda i: jnp.maximum(i * (tm // halo) - 1, 0)
    in_specs = [pl.BlockSpec((tm, MIX_WIDTH), lambda i: (i, 0)),
                pl.BlockSpec((tm, MIX_WIDTH), lambda i: (i, 1)),
                pl.BlockSpec((tm, MIX_WIDTH), lambda i: (i, 2)),
                pl.BlockSpec((tm, MEM_WIDTH), lambda i: (i, um_blk)),
                pl.BlockSpec((halo, MIX_WIDTH), lambda i: (prev(i), 0)),
                pl.BlockSpec((halo, MIX_WIDTH), lambda i: (prev(i), 2)),
                pl.BlockSpec((CONV_WIDTH, MIX_WIDTH), lambda i: (0, 0))] + _tail_common_specs(tm)
    return pl.pallas_call(
        functools.partial(_conv_tail_kernel, tm=tm),
        grid=(N_TOK // tm,), in_specs=in_specs, name="conv_tail", **_tail_outs(tm),
    )(u, u, u, u, u, u, conv_w, kvm, g_mq, g_mk, w_out, x, g_ffn)


def attn_tail(mix, um_src, um_blk, kvm, g_mq, g_mk, w_out, x, g_ffn, tm=256):
    in_specs = [pl.BlockSpec((tm, MIX_WIDTH), lambda i: (i, 0)),
                pl.BlockSpec((tm, MEM_WIDTH), lambda i: (i, um_blk))] + _tail_common_specs(tm)
    return pl.pallas_call(
        _attn_tail_kernel,
        grid=(N_TOK // tm,), in_specs=in_specs, name="attn_tail", **_tail_outs(tm),
    )(mix, um_src, kvm, g_mq, g_mk, w_out, x, g_ffn)


def _ffn_kernel(exp_ref, row0_ref, nsub_ref, tail_ref, xs_hbm, wg_ref, wu_ref, wd_ref, y_hbm,
                x_s, acc, wg_s, wu_s, wd_s, sem):
    s = pl.program_id(0)
    f = pl.program_id(1)
    nsub = nsub_ref[s]
    row0 = row0_ref[s]
    last_step = (s == pl.num_programs(0) - 1) & (f == pl.num_programs(1) - 1)

    def x_copy(j):
        r = pl.multiple_of(j * FFN_SUB, FFN_SUB)
        g = pl.multiple_of(row0 + r, FFN_SUB)
        return pltpu.make_async_copy(xs_hbm.at[pl.ds(g, FFN_SUB)], x_s.at[pl.ds(r, FFN_SUB)], sem)

    def y_copy(j):
        r = pl.multiple_of(j * FFN_SUB, FFN_SUB)
        g = pl.multiple_of(row0 + r, FFN_SUB)
        return pltpu.make_async_copy(acc.at[pl.ds(r, FFN_SUB)], y_hbm.at[pl.ds(g, FFN_SUB)], sem)

    def for_subtiles(fn):
        def body(j, c):
            fn(j)
            return c
        lax.fori_loop(0, nsub, body, 0)

    @pl.when(nsub > 0)
    def _():
        @pl.when(f == 0)
        def _():
            for_subtiles(lambda j: x_copy(j).start())

            def zero(j):
                r = pl.multiple_of(j * FFN_SUB, FFN_SUB)
                acc[pl.ds(r, FFN_SUB), :] = jnp.zeros((FFN_SUB, D_MODEL), f32)
            for_subtiles(zero)
            for_subtiles(lambda j: x_copy(j).wait())

        wg_s[...] = wg_ref[0].astype(bf16)
        wu_s[...] = wu_ref[0].astype(bf16)
        wd_s[...] = wd_ref[0].astype(bf16)

        def sub(j):
            r = pl.multiple_of(j * FFN_SUB, FFN_SUB)
            lo, hi = _unpack_halves(x_s[pl.ds(r, FFN_SUB), :])
            g = _dot(lo, wg_s[:HALF_D, :]) + _dot(hi, wg_s[HALF_D:, :])
            u = _dot(lo, wu_s[:HALF_D, :]) + _dot(hi, wu_s[HALF_D:, :])
            a = (g * jax.nn.sigmoid(g) * u).astype(bf16)
            acc[pl.ds(r, FFN_SUB), :] += _dot(a, wd_s[...])
        for_subtiles(sub)

        @pl.when(f == pl.num_programs(1) - 1)
        def _():
            for_subtiles(lambda j: y_copy(j).start())
            for_subtiles(lambda j: y_copy(j).wait())

    @pl.when(last_step)
    def _():
        acc[pl.ds(0, FFN_SUB), :] = jnp.zeros((FFN_SUB, D_MODEL), f32)

        def tail_copy(j):
            g = pl.multiple_of(j * FFN_SUB, FFN_SUB)
            return pltpu.make_async_copy(acc.at[pl.ds(0, FFN_SUB)], y_hbm.at[pl.ds(g, FFN_SUB)], sem)

        n_tail = y_hbm.shape[0] // FFN_SUB
        lax.fori_loop(tail_ref[0], n_tail, lambda j, c: (tail_copy(j).start(), c)[1], 0)
        lax.fori_loop(tail_ref[0], n_tail, lambda j, c: (tail_copy(j).wait(), c)[1], 0)


def swiglu_ffn(xs, w_gate, w_up, w_down, st_expert, st_row0, st_nsub, st_tail):
    rows = xs.shape[0]
    n_super = st_expert.shape[0]
    nf = D_FF // FFN_TF

    def fblk(f, nsub_ref, s):
        return jnp.where(nsub_ref[s] > 0, f, nf - 1)

    grid_spec = pltpu.PrefetchScalarGridSpec(
        num_scalar_prefetch=4,
        grid=(n_super, nf),
        in_specs=[pl.BlockSpec(memory_space=pl.ANY),
                  pl.BlockSpec((1, D_MODEL, FFN_TF), lambda s, f, e, r, n, t: (e[s], 0, fblk(f, n, s))),
                  pl.BlockSpec((1, D_MODEL, FFN_TF), lambda s, f, e, r, n, t: (e[s], 0, fblk(f, n, s))),
                  pl.BlockSpec((1, FFN_TF, D_MODEL), lambda s, f, e, r, n, t: (e[s], fblk(f, n, s), 0))],
        out_specs=pl.BlockSpec(memory_space=pl.ANY),
        scratch_shapes=[pltpu.VMEM((FFN_SUPER * FFN_SUB, HALF_D), jnp.uint32),
                        pltpu.VMEM((FFN_SUPER * FFN_SUB, D_MODEL), f32),
                        pltpu.VMEM((D_MODEL, FFN_TF), bf16),
                        pltpu.VMEM((D_MODEL, FFN_TF), bf16),
                        pltpu.VMEM((FFN_TF, D_MODEL), bf16),
                        pltpu.SemaphoreType.DMA(())],
    )
    return pl.pallas_call(
        _ffn_kernel,
        grid_spec=grid_spec,
        out_shape=jax.ShapeDtypeStruct((rows, D_MODEL), f32),
        compiler_params=_cparams(("arbitrary", "arbitrary")),
        name="swiglu_ffn",
    )(st_expert, st_row0, st_nsub, st_tail, xs, w_gate, w_up, w_down)


def _super_tiles(padded, offsets, n_super):
    cap = FFN_SUB * FFN_SUPER
    per_group = (padded + cap - 1) // cap
    ends = jnp.cumsum(per_group)
    t = jnp.arange(n_super, dtype=jnp.int32)
    grp = jnp.minimum(jnp.searchsorted(ends, t, side="right").astype(jnp.int32), padded.shape[0] - 1)
    k = t - (ends[grp] - per_group[grp])
    used = t < ends[-1]
    nsub = jnp.where(used, jnp.minimum(FFN_SUPER, (padded[grp] - k * cap) // FFN_SUB), 0)
    row0 = jnp.where(used, offsets[grp] + k * cap, 0)
    last_grp = grp[jnp.maximum(ends[-1] - 1, 0)]
    grp = jnp.where(used, grp, last_grp)
    tail = (jnp.sum(padded) // FFN_SUB).reshape(1)
    return grp.astype(jnp.int32), row0.astype(jnp.int32), nsub.astype(jnp.int32), tail.astype(jnp.int32)


def _rope(x, cos, sin):
    return x * cos + pltpu.roll(x, LANES // 2, 1) * sin


def _mla_kvq_kernel(x0_ref, y_ref, cos_ref, sin_ref, gkv_ref, wkva_ref, gkva_ref, wkvb_ref, gkn_ref, gkr_ref,
                    gmix_ref, win_ref, gqa_ref, wqb_ref, gqn_ref, gqr_ref,
                    x1_ref, k_ref, v_ref, q_ref, um_ref):
    x1 = x0_ref[...] + y_ref[...]
    x1_ref[...] = x1
    cos = cos_ref[...]
    sin = sin_ref[...]

    kva = _dot(_rms(x1, gkv_ref[...]).astype(bf16), wkva_ref[...])
    c_kv = _rms(kva[:, :KV_LORA], gkva_ref[...]).astype(bf16)
    kv = _dot(c_kv, wkvb_ref[...])
    kpe = kva[:, KV_LORA:]
    ss_pe = jnp.sum(kpe * kpe, axis=-1, keepdims=True)
    k_rot = _rope(kpe * gkr_ref[...], cos, sin)
    for h in range(N_HEADS):
        kn = kv[:, h * HEAD_DIM:(h + 1) * HEAD_DIM]
        sc = lax.rsqrt((jnp.sum(kn * kn, axis=-1, keepdims=True) + ss_pe) * (1.0 / QK_HEAD) + EPS)
        k_ref[:, h * HEAD_PAD:h * HEAD_PAD + LANES] = (kn * sc * gkn_ref[...]).astype(bf16)
        k_ref[:, h * HEAD_PAD + LANES:(h + 1) * HEAD_PAD] = (k_rot * sc).astype(bf16)
    v_ref[...] = kv[:, MIX_WIDTH:].astype(bf16)

    u = _dot(_rms(x1, gmix_ref[...]).astype(bf16), win_ref[...])
    um_ref[...] = u[:, Q_LORA:].astype(bf16)
    c_q = _rms(u[:, :Q_LORA], gqa_ref[...]).astype(bf16)
    q = _dot(c_q, wqb_ref[...])
    for h in range(N_HEADS):
        qn = q[:, h * HEAD_PAD:h * HEAD_PAD + LANES]
        qr = q[:, h * HEAD_PAD + LANES:(h + 1) * HEAD_PAD]
        ss = jnp.sum(qn * qn, axis=-1, keepdims=True) + jnp.sum(qr * qr, axis=-1, keepdims=True)
        sc = lax.rsqrt(ss * (1.0 / QK_HEAD) + EPS)
        q_ref[:, h * HEAD_PAD:h * HEAD_PAD + LANES] = (qn * sc * gqn_ref[...]).astype(bf16)
        q_ref[:, h * HEAD_PAD + LANES:(h + 1) * HEAD_PAD] = (_rope(qr * gqr_ref[...], cos, sin) * sc).astype(bf16)


def mla_kvq(x0, y, cos, sin, g_kv, w_kva, g_kva, w_kvb, gk_n, gk_r, g_mix, w_in, g_qa, w_qb, gq_n, gq_r, tm=256):
    row = lambda w: pl.BlockSpec((tm, w), lambda i: (i, 0))
    full = lambda a: pl.BlockSpec(a.shape, lambda i: (0, 0))
    consts = (g_kv, w_kva, g_kva, w_kvb, gk_n, gk_r, g_mix, w_in, g_qa, w_qb, gq_n, gq_r)
    return pl.pallas_call(
        _mla_kvq_kernel,
        grid=(N_TOK // tm,),
        in_specs=[row(D_MODEL), row(D_MODEL), row(LANES), row(LANES)] + [full(a) for a in consts],
        out_specs=[row(D_MODEL), row(N_HEADS * HEAD_PAD), row(MIX_WIDTH), row(N_HEADS * HEAD_PAD), row(MEM_WIDTH)],
        out_shape=[jax.ShapeDtypeStruct((N_TOK, D_MODEL), f32),
                   jax.ShapeDtypeStruct((N_TOK, N_HEADS * HEAD_PAD), bf16),
                   jax.ShapeDtypeStruct((N_TOK, MIX_WIDTH), bf16),
                   jax.ShapeDtypeStruct((N_TOK, N_HEADS * HEAD_PAD), bf16),
                   jax.ShapeDtypeStruct((N_TOK, MEM_WIDTH), bf16)],
        compiler_params=_cparams(("arbitrary",)),
        name="mla_kvq",
    )(x0, y, cos, sin, *consts)


def _flash_kernel(q_ref, k_ref, v_ref, o_ref, *, tq, tk):
    qi = pl.program_id(2)
    q = q_ref[...]
    scale = QK_HEAD ** -0.5

    def step(kj, carry, masked):
        m, l, acc = carry
        r = pl.multiple_of(kj * tk, tk)
        s = _dot_nt(q, k_ref[pl.ds(r, tk), :]) * scale
        if masked:
            qpos = qi * tq + lax.broadcasted_iota(jnp.int32, (tq, tk), 0)
            kpos = r + lax.broadcasted_iota(jnp.int32, (tq, tk), 1)
            s = jnp.where(kpos <= qpos, s, -jnp.inf)
        m_new = jnp.maximum(m, jnp.max(s, axis=-1, keepdims=True))
        alpha = jnp.exp(m - m_new)
        p = jnp.exp(s - m_new)
        l = alpha * l + jnp.sum(p, axis=-1, keepdims=True)
        acc = alpha * acc + _dot(p.astype(bf16), v_ref[pl.ds(r, tk), :])
        return m_new, l, acc

    init = (jnp.full((tq, 1), -jnp.inf, f32), jnp.zeros((tq, 1), f32), jnp.zeros((tq, HEAD_DIM), f32))
    n_full = qi * (tq // tk)
    carry = lax.fori_loop(0, n_full, lambda kj, c: step(kj, c, False), init)
    for d in range(tq // tk):
        carry = step(n_full + d, carry, True)
    m, l, acc = carry
    o_ref[...] = (acc / l).astype(o_ref.dtype)


def flash_attention(q, k, v, tq=256, tk=256):
    per_seq = SEQ // tq
    return pl.pallas_call(
        functools.partial(_flash_kernel, tq=tq, tk=tk),
        grid=(BATCH, N_HEADS, per_seq),
        in_specs=[pl.BlockSpec((tq, HEAD_PAD), lambda b, h, i: (b * per_seq + i, h)),
                  pl.BlockSpec((SEQ, HEAD_PAD), lambda b, h, i: (b, h)),
                  pl.BlockSpec((SEQ, HEAD_DIM), lambda b, h, i: (b, h))],
        out_specs=pl.BlockSpec((tq, HEAD_DIM), lambda b, h, i: (b * per_seq + i, h)),
        out_shape=jax.ShapeDtypeStruct((N_TOK, MIX_WIDTH), bf16),
        compiler_params=_cparams(("arbitrary", "arbitrary", "arbitrary")),
        name="flash_attention",
    )(q, k, v)


def _router_kernel(x_ref, g_ref, wr_ref, mi_ref, mf_ref, cnt_ref, carry, *, tm):
    i = pl.program_id(0)

    @pl.when(i == 0)
    def _():
        carry[...] = jnp.zeros_like(carry)

    h = _rms(x_ref[...], g_ref[...])
    logits = jnp.dot(h, wr_ref[...], preferred_element_type=f32, precision=lax.Precision.HIGHEST)
    lane = lax.broadcasted_iota(jnp.int32, (tm, LANES), 1)
    logits = jnp.where(lane < N_EXPERTS, logits, -jnp.inf)
    lane_f = lane.astype(f32)
    v1 = jnp.max(logits, axis=-1, keepdims=True)
    e1 = jnp.min(jnp.where(logits == v1, lane_f, float(LANES)), axis=-1, keepdims=True).astype(jnp.int32)
    rest = jnp.where(lane == e1, -jnp.inf, logits)
    v2 = jnp.max(rest, axis=-1, keepdims=True)
    e2 = jnp.min(jnp.where(rest == v2, lane_f, float(LANES)), axis=-1, keepdims=True).astype(jnp.int32)
    t = jnp.exp(v2 - v1)
    w1 = 1.0 / (1.0 + t)
    w2 = t / (1.0 + t)

    hot = jnp.where((lane == e1) | (lane == e2), 1.0, 0.0)
    r_io = lax.broadcasted_iota(jnp.int32, (tm, tm), 0)
    c_io = lax.broadcasted_iota(jnp.int32, (tm, tm), 1)
    below = jnp.where(c_io < r_io, 1.0, 0.0).astype(bf16)
    rank = _dot(below, hot.astype(bf16)) + carry[...]
    carry[...] += jnp.sum(hot, axis=0, keepdims=True)
    rank1 = jnp.sum(jnp.where(lane == e1, rank, 0.0), axis=-1, keepdims=True).astype(jnp.int32)
    rank2 = jnp.sum(jnp.where(lane == e2, rank, 0.0), axis=-1, keepdims=True).astype(jnp.int32)

    mi_ref[...] = jnp.where(lane == 0, e1, jnp.where(lane == 1, e2, jnp.where(lane == 2, rank1,
                            jnp.where(lane == 3, rank2, 0))))
    mf_ref[...] = jnp.where(lane == 0, w1, jnp.where(lane == 1, w2, 0.0))
    cnt_ref[...] = carry[...].astype(jnp.int32)


def router(x, g, w_router_pad, tm=512):
    return pl.pallas_call(
        functools.partial(_router_kernel, tm=tm),
        grid=(N_TOK // tm,),
        in_specs=[pl.BlockSpec((tm, D_MODEL), lambda i: (i, 0)),
                  pl.BlockSpec((1, D_MODEL), lambda i: (0, 0)),
                  pl.BlockSpec((D_MODEL, LANES), lambda i: (0, 0))],
        out_specs=[pl.BlockSpec((tm, LANES), lambda i: (i, 0)),
                   pl.BlockSpec((tm, LANES), lambda i: (i, 0)),
                   pl.BlockSpec((1, LANES), lambda i: (0, 0))],
        out_shape=[jax.ShapeDtypeStruct((N_TOK, LANES), jnp.int32),
                   jax.ShapeDtypeStruct((N_TOK, LANES), f32),
                   jax.ShapeDtypeStruct((1, LANES), jnp.int32)],
        scratch_shapes=[pltpu.VMEM((1, LANES), f32)],
        compiler_params=_cparams(("arbitrary",)),
        name="router",
    )(x, g, w_router_pad)


def _dispatch_kernel(p1_ref, p2_ref, hp_ref, xs_in, xs_out, sem, *, tm):
    del xs_in
    base = pl.program_id(0) * tm

    def copy(i, pos_ref):
        return pltpu.make_async_copy(hp_ref.at[pl.ds(i, 1)], xs_out.at[pl.ds(pos_ref[base + i], 1)], sem)

    def start(i, c):
        copy(i, p1_ref).start()
        copy(i, p2_ref).start()
        return c

    def wait(i, c):
        copy(i, p1_ref).wait()
        copy(i, p2_ref).wait()
        return c

    lax.fori_loop(0, tm, start, 0)
    lax.fori_loop(0, tm, wait, 0)


def dispatch(pos1, pos2, hp, xs_zero, tm=256):
    grid_spec = pltpu.PrefetchScalarGridSpec(
        num_scalar_prefetch=2,
        grid=(N_TOK // tm,),
        in_specs=[pl.BlockSpec((tm, HALF_D), lambda i, p1, p2: (i, 0)),
                  pl.BlockSpec(memory_space=pl.ANY)],
        out_specs=pl.BlockSpec(memory_space=pl.ANY),
        scratch_shapes=[pltpu.SemaphoreType.DMA(())],
    )
    return pl.pallas_call(
        functools.partial(_dispatch_kernel, tm=tm),
        grid_spec=grid_spec,
        out_shape=jax.ShapeDtypeStruct(xs_zero.shape, xs_zero.dtype),
        input_output_aliases={3: 0},
        compiler_params=_cparams(("arbitrary",)),
        name="dispatch",
    )(pos1, pos2, hp, xs_zero)


def _combine_kernel(p1_ref, p2_ref, x_ref, mf_ref, y_hbm, o_ref, buf, sem, *, tm):
    base = pl.program_id(0) * tm

    def copy(i, k, pos_ref):
        return pltpu.make_async_copy(y_hbm.at[pl.ds(pos_ref[base + i], 1)], buf.at[k, pl.ds(i, 1)], sem)

    def start(i, c):
        copy(i, 0, p1_ref).start()
        copy(i, 1, p2_ref).start()
        return c

    def wait(i, c):
        copy(i, 0, p1_ref).wait()
        copy(i, 1, p2_ref).wait()
        return c

    lax.fori_loop(0, tm, start, 0)
    lax.fori_loop(0, tm, wait, 0)
    w = mf_ref[...]
    o_ref[...] = x_ref[...] + w[:, 0:1] * buf[0] + w[:, 1:2] * buf[1]


def combine(pos1, pos2, x, mf, y, tm=256):
    grid_spec = pltpu.PrefetchScalarGridSpec(
        num_scalar_prefetch=2,
        grid=(N_TOK // tm,),
        in_specs=[pl.BlockSpec((tm, D_MODEL), lambda i, p1, p2: (i, 0)),
                  pl.BlockSpec((tm, LANES), lambda i, p1, p2: (i, 0)),
                  pl.BlockSpec(memory_space=pl.ANY)],
        out_specs=pl.BlockSpec((tm, D_MODEL), lambda i, p1, p2: (i, 0)),
        scratch_shapes=[pltpu.VMEM((2, tm, D_MODEL), f32), pltpu.SemaphoreType.DMA(())],
    )
    return pl.pallas_call(
        functools.partial(_combine_kernel, tm=tm),
        grid_spec=grid_spec,
        out_shape=jax.ShapeDtypeStruct((N_TOK, D_MODEL), f32),
        compiler_params=_cparams(("arbitrary",)),
        name="combine",
    )(pos1, pos2, x, mf, y)


def _rope_lanes(v):
    z = jnp.zeros(v.shape[:-1] + (QK_ROPE // 2,), v.dtype)
    return jnp.concatenate([v[..., :QK_ROPE // 2], z, v[..., QK_ROPE // 2:], z], axis=-1)


def kernel(x, mem, positions, g_mix, g_ffn, g_mem, w_mem_kv, g_mq, g_mk, w_out, a_w_in, a_conv_w, b_w_in,
           b_g_q_a, b_w_q_b, b_g_qn, g_kv, w_kv_a, g_kv_a, w_kv_b, g_kn, ffn_w_gate, ffn_w_up, ffn_w_down,
           moe_w_router, moe_w_gate, moe_w_up, moe_w_down):
    row = lambda v: v.reshape(1, -1).astype(f32)
    x0 = x.reshape(N_TOK, D_MODEL)
    mem2 = mem.reshape(BATCH * N_MEM, D_MODEL)

    w_kva = jnp.concatenate([w_kv_a[:, :KV_LORA], _rope_lanes(w_kv_a[:, KV_LORA:])], axis=1).astype(bf16)
    kvb = w_kv_b.reshape(KV_LORA, N_HEADS, QK_NOPE + HEAD_DIM)
    w_kvb = jnp.concatenate([kvb[:, :, :QK_NOPE].reshape(KV_LORA, MIX_WIDTH),
                             kvb[:, :, QK_NOPE:].reshape(KV_LORA, MIX_WIDTH)], axis=1).astype(bf16)
    qb = b_w_q_b[0].reshape(Q_LORA, N_HEADS, QK_HEAD)
    w_qb = jnp.concatenate([qb[:, :, :QK_NOPE], _rope_lanes(qb[:, :, QK_NOPE:])], axis=-1)
    w_qb = w_qb.reshape(Q_LORA, N_HEADS * HEAD_PAD).astype(bf16)
    gk_n, gk_r = row(g_kn[:QK_NOPE]), row(_rope_lanes(g_kn[QK_NOPE:]))
    gq_n, gq_r = row(b_g_qn[0, :QK_NOPE]), row(_rope_lanes(b_g_qn[0, QK_NOPE:]))
    w_router_pad = jnp.pad(moe_w_router[0].astype(f32), ((0, 0), (0, LANES - N_EXPERTS)))

    inv_freq = 1.0 / (ROPE_THETA ** (jnp.arange(0, QK_ROPE, 2, dtype=f32) / QK_ROPE))
    ang = positions.reshape(N_TOK, 1).astype(f32) * inv_freq[None, :]
    zeros = jnp.zeros_like(ang)
    cos_t = jnp.concatenate([jnp.cos(ang), zeros, jnp.cos(ang), zeros], axis=-1)
    sin_t = jnp.concatenate([-jnp.sin(ang), zeros, jnp.sin(ang), zeros], axis=-1)

    u0 = norm_matmul(x0, row(g_mix[0]), a_w_in[0].astype(bf16), tm=1024, tn=1024)
    kvm0 = norm_matmul(mem2, row(g_mem[0]), w_mem_kv[0].astype(bf16), tm=BATCH * N_MEM, tn=2 * MEM_WIDTH)
    xa, hp0 = conv_tail(u0, a_conv_w[0].astype(f32), kvm0, row(g_mq[0]), row(g_mk[0]),
                        w_out[0].astype(bf16), x0, row(g_ffn[0]))
    dense_tbl = _super_tiles(jnp.array([N_TOK], jnp.int32), jnp.array([0], jnp.int32),
                             -(-N_TOK // (FFN_SUB * FFN_SUPER)))
    y0 = swiglu_ffn(hp0, ffn_w_gate, ffn_w_up, ffn_w_down, *dense_tbl)

    x1, k, v, q, um1 = mla_kvq(xa, y0, cos_t, sin_t, row(g_kv), w_kva, row(g_kv_a), w_kvb, gk_n, gk_r,
                               row(g_mix[1]), b_w_in[0].astype(bf16), row(b_g_q_a[0]), w_qb, gq_n, gq_r)
    mix1 = flash_attention(q, k, v)
    kvm1 = norm_matmul(mem2, row(g_mem[1]), w_mem_kv[1].astype(bf16), tm=BATCH * N_MEM, tn=2 * MEM_WIDTH)
    x2, hp1 = attn_tail(mix1, um1, 0, kvm1, row(g_mq[1]), row(g_mk[1]), w_out[1].astype(bf16), x1, row(g_ffn[1]))

    mi, mf, cnt = router(x2, row(g_ffn[1]), w_router_pad)
    counts = cnt[0, :N_EXPERTS]
    padded = (counts + FFN_SUB - 1) // FFN_SUB * FFN_SUB
    offsets = jnp.cumsum(padded) - padded
    pos1 = offsets[mi[:, 0]] + mi[:, 2]
    pos2 = offsets[mi[:, 1]] + mi[:, 3]
    xs = dispatch(pos1, pos2, hp1, jnp.zeros((MOE_ROWS, HALF_D), jnp.uint32))
    y1 = swiglu_ffn(xs, moe_w_gate[0], moe_w_up[0], moe_w_down[0], *_super_tiles(padded, offsets, MOE_SUPERS))
    out = combine(pos1, pos2, x2, mf, y1)
    return out.reshape(BATCH, SEQ, D_MODEL)
```

```python
import functools

import jax
import jax.numpy as jnp
from jax import lax
from jax.experimental import pallas as pl
from jax.experimental.pallas import tpu as pltpu

D_MODEL = 2048
BATCH = 2
SEQ = 4096
N_TOK = BATCH * SEQ
N_HEADS = 12
HEAD_DIM = 128
MIX_WIDTH = N_HEADS * HEAD_DIM
MEM_HEADS = 4
MEM_WIDTH = MEM_HEADS * HEAD_DIM
N_MEM = 256
CONV_WIDTH = 3
Q_LORA = 512
KV_LORA = 256
QK_NOPE = 128
QK_ROPE = 64
QK_HEAD = QK_NOPE + QK_ROPE
ROPE_THETA = 10000.0
D_FF = 7168
N_EXPERTS = 8
TOP_K = 2
EPS = 1e-6

LANES = 128
HEAD_PAD = 2 * LANES
HALF_D = D_MODEL // 2
VMEM_LIMIT = 56 * 1024 * 1024

FFN_SUB = 256
FFN_SUPER = 9
FFN_TF = 256
MOE_ROWS = N_TOK * TOP_K + N_EXPERTS * FFN_SUB
MOE_SUPERS = N_EXPERTS + -(-MOE_ROWS // (FFN_SUB * FFN_SUPER))

f32 = jnp.float32
bf16 = jnp.bfloat16


def _cparams(sem):
    return pltpu.CompilerParams(dimension_semantics=sem, vmem_limit_bytes=VMEM_LIMIT)


def _rms(x, g, n=None):
    n = x.shape[-1] if n is None else n
    ms = jnp.sum(x * x, axis=-1, keepdims=True) * (1.0 / n)
    return x * lax.rsqrt(ms + EPS) * g


def _pack_halves(h):
    hb = h.astype(bf16).astype(f32)
    return pltpu.pack_elementwise([hb[:, :HALF_D], hb[:, HALF_D:]], packed_dtype=bf16)


def _unpack_halves(p):
    lo = pltpu.unpack_elementwise(p, index=0, packed_dtype=bf16, unpacked_dtype=f32)
    hi = pltpu.unpack_elementwise(p, index=1, packed_dtype=bf16, unpacked_dtype=f32)
    return lo.astype(bf16), hi.astype(bf16)


def _dot(a, b):
    return jnp.dot(a, b, preferred_element_type=f32)


def _dot_nt(a, b):
    return lax.dot_general(a, b, (((1,), (1,)), ((), ())), preferred_element_type=f32)


def _norm_matmul_kernel(x_ref, g_ref, w_ref, o_ref, h_ref):
    @pl.when(pl.program_id(1) == 0)
    def _():
        h_ref[...] = _rms(x_ref[...], g_ref[...]).astype(bf16)

    o_ref[...] = _dot(h_ref[...], w_ref[...]).astype(o_ref.dtype)


def norm_matmul(x, g, w, tm, tn, out_dtype=bf16):
    m, d = x.shape
    n = w.shape[1]
    return pl.pallas_call(
        _norm_matmul_kernel,
        grid=(m // tm, n // tn),
        in_specs=[pl.BlockSpec((tm, d), lambda i, j: (i, 0)),
                  pl.BlockSpec((1, d), lambda i, j: (0, 0)),
                  pl.BlockSpec((d, tn), lambda i, j: (0, j))],
        out_specs=pl.BlockSpec((tm, tn), lambda i, j: (i, j)),
        out_shape=jax.ShapeDtypeStruct((m, n), out_dtype),
        scratch_shapes=[pltpu.VMEM((tm, d), bf16)],
        compiler_params=_cparams(("arbitrary", "arbitrary")),
        name="norm_matmul",
    )(x, g, w)


def _mem_attention(um_ref, kvm_ref, gq_ref, gk_ref, cat_ref):
    scale = HEAD_DIM ** -0.5
    for h in range(MEM_HEADS):
        cs = slice(h * HEAD_DIM, (h + 1) * HEAD_DIM)
        q = _rms(um_ref[:, cs].astype(f32), gq_ref[...]).astype(bf16)
        k = _rms(kvm_ref[:, cs].astype(f32), gk_ref[...]).astype(bf16)
        v = kvm_ref[:, MEM_WIDTH + h * HEAD_DIM:MEM_WIDTH + (h + 1) * HEAD_DIM]
        s = _dot_nt(q, k) * scale
        p = jnp.exp(s - jnp.max(s, axis=-1, keepdims=True))
        o = _dot(p.astype(bf16), v) / jnp.sum(p, axis=-1, keepdims=True)
        cat_ref[:, MIX_WIDTH + h * HEAD_DIM:MIX_WIDTH + (h + 1) * HEAD_DIM] = o.astype(bf16)


def _tail_epilogue(cat_ref, wout_ref, x_ref, gffn_ref, xo_ref, hp_ref):
    xn = x_ref[...] + _dot(cat_ref[...], wout_ref[...])
    xo_ref[...] = xn
    hp_ref[...] = _pack_halves(_rms(xn, gffn_ref[...]))


def _conv_tail_kernel(xin_ref, gb_ref, gc_ref, um_ref, hx_ref, hg_ref, cw_ref, kvm_ref, gq_ref, gk_ref,
                      wout_ref, x_ref, gffn_ref, xo_ref, hp_ref, cat_ref, *, tm):
    i = pl.program_id(0)
    first = (i * tm) % SEQ == 0
    rows = lax.broadcasted_iota(jnp.int32, (tm, 1), 0)
    cc = 512
    for c in range(MIX_WIDTH // cc):
        cs = slice(c * cc, (c + 1) * cc)
        z = gc_ref[:, cs].astype(f32) * xin_ref[:, cs].astype(f32)
        zh = hg_ref[:, cs].astype(f32) * hx_ref[:, cs].astype(f32)
        zh = jnp.where(first, 0.0, zh)
        p1 = zh[15:16, :]
        p2 = zh[14:15, :]
        z1 = jnp.where(rows == 0, p1, pltpu.roll(z, 1, 0))
        z2 = jnp.where(rows == 0, p2, jnp.where(rows == 1, p1, pltpu.roll(z, 2, 0)))
        y = cw_ref[0:1, cs] * z2 + cw_ref[1:2, cs] * z1 + cw_ref[2:3, cs] * z
        cat_ref[:, cs] = (gb_ref[:, cs].astype(f32) * y).astype(bf16)
    _mem_attention(um_ref, kvm_ref, gq_ref, gk_ref, cat_ref)
    _tail_epilogue(cat_ref, wout_ref, x_ref, gffn_ref, xo_ref, hp_ref)


def _attn_tail_kernel(mix_ref, um_ref, kvm_ref, gq_ref, gk_ref, wout_ref, x_ref, gffn_ref,
                      xo_ref, hp_ref, cat_ref):
    cat_ref[:, :MIX_WIDTH] = mix_ref[...]
    _mem_attention(um_ref, kvm_ref, gq_ref, gk_ref, cat_ref)
    _tail_epilogue(cat_ref, wout_ref, x_ref, gffn_ref, xo_ref, hp_ref)


def _tail_common_specs(tm):
    per_seq = SEQ // tm
    return [pl.BlockSpec((N_MEM, 2 * MEM_WIDTH), lambda i: (i // per_seq, 0)),
            pl.BlockSpec((1, HEAD_DIM), lambda i: (0, 0)),
            pl.BlockSpec((1, HEAD_DIM), lambda i: (0, 0)),
            pl.BlockSpec((D_MODEL, D_MODEL), lambda i: (0, 0)),
            pl.BlockSpec((tm, D_MODEL), lambda i: (i, 0)),
            pl.BlockSpec((1, D_MODEL), lambda i: (0, 0))]


def _tail_outs(tm):
    return dict(
        out_specs=[pl.BlockSpec((tm, D_MODEL), lambda i: (i, 0)),
                   pl.BlockSpec((tm, HALF_D), lambda i: (i, 0))],
        out_shape=[jax.ShapeDtypeStruct((N_TOK, D_MODEL), f32),
                   jax.ShapeDtypeStruct((N_TOK, HALF_D), jnp.uint32)],
        scratch_shapes=[pltpu.VMEM((tm, D_MODEL), bf16)],
        compiler_params=_cparams(("arbitrary",)),
    )


def conv_tail(u, conv_w, kvm, g_mq, g_mk, w_out, x, g_ffn, tm=256):
    halo = 16
    um_blk = 3 * MIX_WIDTH // MEM_WIDTH
    prev = lambda i: jnp.maximum(i * (tm // halo) - 1, 0)
    in_specs = [pl.BlockSpec((tm, MIX_WIDTH), lambda i: (i, 0)),
                pl.BlockSpec((tm, MIX_WIDTH), lambda i: (i, 1)),
                pl.BlockSpec((tm, MIX_WIDTH), lambda i: (i, 2)),
                pl.BlockSpec((tm, MEM_WIDTH), lambda i: (i, um_blk)),
                pl.BlockSpec((halo, MIX_WIDTH), lambda i: (prev(i), 0)),
                pl.BlockSpec((halo, MIX_WIDTH), lambda i: (prev(i), 2)),
                pl.BlockSpec((CONV_WIDTH, MIX_WIDTH), lambda i: (0, 0))] + _tail_common_specs(tm)
    return pl.pallas_call(
        functools.partial(_conv_tail_kernel, tm=tm),
        grid=(N_TOK // tm,), in_specs=in_specs, name="conv_tail", **_tail_outs(tm),
    )(u, u, u, u, u, u, conv_w, kvm, g_mq, g_mk, w_out, x, g_ffn)


def attn_tail(mix, um_src, um_blk, kvm, g_mq, g_mk, w_out, x, g_ffn, tm=256):
    in_specs = [pl.BlockSpec((tm, MIX_WIDTH), lambda i: (i, 0)),
                pl.BlockSpec((tm, MEM_WIDTH), lambda i: (i, um_blk))] + _tail_common_specs(tm)
    return pl.pallas_call(
        _attn_tail_kernel,
        grid=(N_TOK // tm,), in_specs=in_specs, name="attn_tail", **_tail_outs(tm),
    )(mix, um_src, kvm, g_mq, g_mk, w_out, x, g_ffn)


def _ffn_kernel(exp_ref, row0_ref, nsub_ref, tail_ref, xs_hbm, wg_ref, wu_ref, wd_ref, y_hbm,
                x_s, acc, wg_s, wu_s, wd_s, sem):
    s = pl.program_id(0)
    f = pl.program_id(1)
    nsub = nsub_ref[s]
    row0 = row0_ref[s]
    last_step = (s == pl.num_programs(0) - 1) & (f == pl.num_programs(1) - 1)

    def x_copy(j):
        r = pl.multiple_of(j * FFN_SUB, FFN_SUB)
        g = pl.multiple_of(row0 + r, FFN_SUB)
        return pltpu.make_async_copy(xs_hbm.at[pl.ds(g, FFN_SUB)], x_s.at[pl.ds(r, FFN_SUB)], sem)

    def y_copy(j):
        r = pl.multiple_of(j * FFN_SUB, FFN_SUB)
        g = pl.multiple_of(row0 + r, FFN_SUB)
        return pltpu.make_async_copy(acc.at[pl.ds(r, FFN_SUB)], y_hbm.at[pl.ds(g, FFN_SUB)], sem)

    def for_subtiles(fn):
        def body(j, c):
            fn(j)
            return c
        lax.fori_loop(0, nsub, body, 0)

    @pl.when(nsub > 0)
    def _():
        @pl.when(f == 0)
        def _():
            for_subtiles(lambda j: x_copy(j).start())

            def zero(j):
                r = pl.multiple_of(j * FFN_SUB, FFN_SUB)
                acc[pl.ds(r, FFN_SUB), :] = jnp.zeros((FFN_SUB, D_MODEL), f32)
            for_subtiles(zero)
            for_subtiles(lambda j: x_copy(j).wait())

        def swiglu_rows(r, rows, wg, wu, wd):
            lo, hi = _unpack_halves(x_s[pl.ds(r, rows), :])
            g = _dot(lo, wg[:HALF_D, :]) + _dot(hi, wg[HALF_D:, :])
            u = _dot(lo, wu[:HALF_D, :]) + _dot(hi, wu[HALF_D:, :])
            a = (g * jax.nn.sigmoid(g) * u).astype(bf16)
            acc[pl.ds(r, rows), :] += _dot(a, wd[...])

        wg = wg_ref[0].astype(bf16)
        wu = wu_ref[0].astype(bf16)
        wd = wd_ref[0].astype(bf16)
        wg_s[...] = wg
        wu_s[...] = wu
        wd_s[...] = wd
        swiglu_rows(0, FFN_SUB, wg, wu, wd)

        rem = nsub - 1
        n_quad = lax.shift_right_logical(rem, 2)

        def quad(j, c):
            r = pl.multiple_of(FFN_SUB + j * 4 * FFN_SUB, FFN_SUB)
            swiglu_rows(r, 4 * FFN_SUB, wg_s, wu_s, wd_s)
            return c
        lax.fori_loop(0, n_quad, quad, 0)

        @pl.when((rem & 2) != 0)
        def _():
            r = pl.multiple_of(FFN_SUB + n_quad * 4 * FFN_SUB, FFN_SUB)
            swiglu_rows(r, 2 * FFN_SUB, wg_s, wu_s, wd_s)

        @pl.when((rem & 1) != 0)
        def _():
            swiglu_rows(pl.multiple_of(rem * FFN_SUB, FFN_SUB), FFN_SUB, wg_s, wu_s, wd_s)

        @pl.when(f == pl.num_programs(1) - 1)
        def _():
            for_subtiles(lambda j: y_copy(j).start())
            for_subtiles(lambda j: y_copy(j).wait())

    @pl.when(last_step)
    def _():
        acc[pl.ds(0, FFN_SUB), :] = jnp.zeros((FFN_SUB, D_MODEL), f32)

        def tail_copy(j):
            g = pl.multiple_of(j * FFN_SUB, FFN_SUB)
            return pltpu.make_async_copy(acc.at[pl.ds(0, FFN_SUB)], y_hbm.at[pl.ds(g, FFN_SUB)], sem)

        n_tail = y_hbm.shape[0] // FFN_SUB
        lax.fori_loop(tail_ref[0], n_tail, lambda j, c: (tail_copy(j).start(), c)[1], 0)
        lax.fori_loop(tail_ref[0], n_tail, lambda j, c: (tail_copy(j).wait(), c)[1], 0)


def swiglu_ffn(xs, w_gate, w_up, w_down, st_expert, st_row0, st_nsub, st_tail):
    rows = xs.shape[0]
    n_super = st_expert.shape[0]
    nf = D_FF // FFN_TF

    def fblk(f, nsub_ref, s):
        return jnp.where(nsub_ref[s] > 0, f, nf - 1)

    grid_spec = pltpu.PrefetchScalarGridSpec(
        num_scalar_prefetch=4,
        grid=(n_super, nf),
        in_specs=[pl.BlockSpec(memory_space=pl.ANY),
                  pl.BlockSpec((1, D_MODEL, FFN_TF), lambda s, f, e, r, n, t: (e[s], 0, fblk(f, n, s))),
                  pl.BlockSpec((1, D_MODEL, FFN_TF), lambda s, f, e, r, n, t: (e[s], 0, fblk(f, n, s))),
                  pl.BlockSpec((1, FFN_TF, D_MODEL), lambda s, f, e, r, n, t: (e[s], fblk(f, n, s), 0))],
        out_specs=pl.BlockSpec(memory_space=pl.ANY),
        scratch_shapes=[pltpu.VMEM((FFN_SUPER * FFN_SUB, HALF_D), jnp.uint32),
                        pltpu.VMEM((FFN_SUPER * FFN_SUB, D_MODEL), f32),
                        pltpu.VMEM((D_MODEL, FFN_TF), bf16),
                        pltpu.VMEM((D_MODEL, FFN_TF), bf16),
                        pltpu.VMEM((FFN_TF, D_MODEL), bf16),
                        pltpu.SemaphoreType.DMA(())],
    )
    return pl.pallas_call(
        _ffn_kernel,
        grid_spec=grid_spec,
        out_shape=jax.ShapeDtypeStruct((rows, D_MODEL), f32),
        compiler_params=_cparams(("arbitrary", "arbitrary")),
        name="swiglu_ffn",
    )(st_expert, st_row0, st_nsub, st_tail, xs, w_gate, w_up, w_down)


def _super_tiles(padded, offsets, n_super):
    cap = FFN_SUB * FFN_SUPER
    per_group = (padded + cap - 1) // cap
    ends = jnp.cumsum(per_group)
    t = jnp.arange(n_super, dtype=jnp.int32)
    grp = jnp.minimum(jnp.sum((t[:, None] >= ends[None, :]).astype(jnp.int32), axis=1), padded.shape[0] - 1)
    k = t - (ends[grp] - per_group[grp])
    used = t < ends[-1]
    nsub = jnp.where(used, jnp.minimum(FFN_SUPER, (padded[grp] - k * cap) // FFN_SUB), 0)
    row0 = jnp.where(used, offsets[grp] + k * cap, 0)
    last_grp = grp[jnp.maximum(ends[-1] - 1, 0)]
    grp = jnp.where(used, grp, last_grp)
    tail = (jnp.sum(padded) // FFN_SUB).reshape(1)
    return grp.astype(jnp.int32), row0.astype(jnp.int32), nsub.astype(jnp.int32), tail.astype(jnp.int32)


def _rope(x, cos, sin):
    return x * cos + pltpu.roll(x, LANES // 2, 1) * sin


def _dot_split(a, sel):
    hi = a.astype(bf16)
    lo = (a - hi.astype(f32)).astype(bf16)
    return _dot(hi, sel) + _dot(lo, sel)


def _mla_kvq_kernel(x0_ref, y_ref, cos_ref, sin_ref, gkv_ref, wkva_ref, gkva_ref, wkvb_ref, gkn_ref, gkr_ref,
                    gmix_ref, win_ref, gqa_ref, wqb_ref, gqn_ref, gqr_ref, sel_ref,
                    x1_ref, k_ref, v_ref, q_ref, um_ref):
    x1 = x0_ref[...] + y_ref[...]
    x1_ref[...] = x1
    cos = cos_ref[...]
    sin = sin_ref[...]

    kva = _dot(_rms(x1, gkv_ref[...]).astype(bf16), wkva_ref[...])
    c_kv = _rms(kva[:, :KV_LORA], gkva_ref[...]).astype(bf16)
    kv = _dot(c_kv, wkvb_ref[...])
    tm = x1.shape[0]
    kpe = kva[:, KV_LORA:]
    ss_pe = jnp.sum(kpe * kpe, axis=-1, keepdims=True)
    sc = lax.rsqrt((_dot_split(kv[:, :MIX_WIDTH] * kv[:, :MIX_WIDTH], sel_ref[...]) + ss_pe) * (1.0 / QK_HEAD) + EPS)
    k_rot = _rope(kpe * gkr_ref[...], cos, sin)
    for h in range(N_HEADS):
        sch = jnp.broadcast_to(sc[:, h:h + 1], (tm, LANES))
        k_ref[:, h * HEAD_PAD:h * HEAD_PAD + LANES] = (
            kv[:, h * HEAD_DIM:(h + 1) * HEAD_DIM] * sch * gkn_ref[...]).astype(bf16)
        k_ref[:, h * HEAD_PAD + LANES:(h + 1) * HEAD_PAD] = (k_rot * sch).astype(bf16)
    v_ref[...] = kv[:, MIX_WIDTH:].astype(bf16)

    u = _dot(_rms(x1, gmix_ref[...]).astype(bf16), win_ref[...])
    um_ref[...] = u[:, Q_LORA:].astype(bf16)
    c_q = _rms(u[:, :Q_LORA], gqa_ref[...]).astype(bf16)
    q = _dot(c_q, wqb_ref[...])
    q2 = jnp.concatenate([q[:, h * HEAD_PAD:h * HEAD_PAD + LANES] * q[:, h * HEAD_PAD:h * HEAD_PAD + LANES]
                          + q[:, h * HEAD_PAD + LANES:(h + 1) * HEAD_PAD] * q[:, h * HEAD_PAD + LANES:(h + 1) * HEAD_PAD]
                          for h in range(N_HEADS)], axis=-1)
    sc = lax.rsqrt(_dot_split(q2, sel_ref[...]) * (1.0 / QK_HEAD) + EPS)
    for h in range(N_HEADS):
        sch = jnp.broadcast_to(sc[:, h:h + 1], (tm, LANES))
        qn = q[:, h * HEAD_PAD:h * HEAD_PAD + LANES]
        qr = q[:, h * HEAD_PAD + LANES:(h + 1) * HEAD_PAD]
        q_ref[:, h * HEAD_PAD:h * HEAD_PAD + LANES] = (qn * sch * gqn_ref[...]).astype(bf16)
        q_ref[:, h * HEAD_PAD + LANES:(h + 1) * HEAD_PAD] = (_rope(qr * gqr_ref[...], cos, sin) * sch).astype(bf16)


def mla_kvq(x0, y, cos, sin, g_kv, w_kva, g_kva, w_kvb, gk_n, gk_r, g_mix, w_in, g_qa, w_qb, gq_n, gq_r, tm=256):
    row = lambda w: pl.BlockSpec((tm, w), lambda i: (i, 0))
    full = lambda a: pl.BlockSpec(a.shape, lambda i: (0, 0))
    head = jnp.arange(MIX_WIDTH, dtype=jnp.int32) // HEAD_DIM
    sel = (head[:, None] == jnp.arange(LANES, dtype=jnp.int32)[None, :]).astype(bf16)
    consts = (g_kv, w_kva, g_kva, w_kvb, gk_n, gk_r, g_mix, w_in, g_qa, w_qb, gq_n, gq_r, sel)
    return pl.pallas_call(
        _mla_kvq_kernel,
        grid=(N_TOK // tm,),
        in_specs=[row(D_MODEL), row(D_MODEL), row(LANES), row(LANES)] + [full(a) for a in consts],
        out_specs=[row(D_MODEL), row(N_HEADS * HEAD_PAD), row(MIX_WIDTH), row(N_HEADS * HEAD_PAD), row(MEM_WIDTH)],
        out_shape=[jax.ShapeDtypeStruct((N_TOK, D_MODEL), f32),
                   jax.ShapeDtypeStruct((N_TOK, N_HEADS * HEAD_PAD), bf16),
                   jax.ShapeDtypeStruct((N_TOK, MIX_WIDTH), bf16),
                   jax.ShapeDtypeStruct((N_TOK, N_HEADS * HEAD_PAD), bf16),
                   jax.ShapeDtypeStruct((N_TOK, MEM_WIDTH), bf16)],
        compiler_params=_cparams(("arbitrary",)),
        name="mla_kvq",
    )(x0, y, cos, sin, *consts)


def _flash_kernel(q_ref, k_ref, v_ref, o_ref, m_s, l_s, acc_s, *, tq, heads):
    qi = pl.program_id(2)
    c_exp = (QK_HEAD ** -0.5) * 1.4426950408889634
    m_s[...] = jnp.full(m_s.shape, -jnp.inf, f32)
    l_s[...] = jnp.zeros(l_s.shape, f32)
    acc_s[...] = jnp.zeros(acc_s.shape, f32)
    n_blk = tq // LANES

    def chunk(kj, masked):
        r = pl.multiple_of(kj * tq, tq)

        def scores(h):
            q = q_ref[:, h * HEAD_PAD:(h + 1) * HEAD_PAD]
            s = _dot_nt(q, k_ref[pl.ds(r, tq), h * HEAD_PAD:(h + 1) * HEAD_PAD])
            if masked:
                qpos = lax.broadcasted_iota(jnp.int32, (tq, tq), 0)
                kpos = lax.broadcasted_iota(jnp.int32, (tq, tq), 1)
                s = jnp.where(kpos <= qpos, s, -jnp.inf)
            return s

        def update(h, s):
            blocks = [s[:, b * LANES:(b + 1) * LANES] for b in range(n_blk)]
            lane_max = functools.reduce(jnp.maximum, blocks)
            m_old = m_s[h]
            m_new = jnp.maximum(m_old, jnp.max(lane_max, axis=-1, keepdims=True))
            alpha = jnp.exp2((m_old - m_new) * c_exp)
            p = [jnp.exp2((blk - m_new) * c_exp) for blk in blocks]
            l_s[h] = alpha * l_s[h] + functools.reduce(jnp.add, p)
            pv = _dot(jnp.concatenate(p, axis=-1).astype(bf16),
                      v_ref[pl.ds(r, tq), h * HEAD_DIM:(h + 1) * HEAD_DIM])
            acc_s[h] = alpha * acc_s[h] + pv
            m_s[h] = m_new

        s_next = scores(0)
        for h in range(heads):
            s_cur = s_next
            if h + 1 < heads:
                s_next = scores(h + 1)
            update(h, s_cur)

    def body(kj, c):
        chunk(kj, False)
        return c

    lax.fori_loop(0, qi, body, 0)
    chunk(qi, True)
    for h in range(heads):
        l = jnp.sum(l_s[h], axis=-1, keepdims=True)
        o_ref[:, h * HEAD_DIM:(h + 1) * HEAD_DIM] = (acc_s[h] / l).astype(o_ref.dtype)


def flash_attention(q, k, v, tq=512, heads=4):
    per_seq = SEQ // tq
    return pl.pallas_call(
        functools.partial(_flash_kernel, tq=tq, heads=heads),
        grid=(BATCH, N_HEADS // heads, per_seq),
        in_specs=[pl.BlockSpec((tq, heads * HEAD_PAD), lambda b, h, i: (b * per_seq + i, h)),
                  pl.BlockSpec((SEQ, heads * HEAD_PAD), lambda b, h, i: (b, h)),
                  pl.BlockSpec((SEQ, heads * HEAD_DIM), lambda b, h, i: (b, h))],
        out_specs=pl.BlockSpec((tq, heads * HEAD_DIM), lambda b, h, i: (b * per_seq + i, h)),
        out_shape=jax.ShapeDtypeStruct((N_TOK, MIX_WIDTH), bf16),
        scratch_shapes=[pltpu.VMEM((heads, tq, LANES), f32),
                        pltpu.VMEM((heads, tq, LANES), f32),
                        pltpu.VMEM((heads, tq, HEAD_DIM), f32)],
        compiler_params=_cparams(("arbitrary", "arbitrary", "arbitrary")),
        name="flash_attention",
    )(q, k, v)


def _router_kernel(x_ref, g_ref, wr_ref, mi_ref, mf_ref, cnt_ref, carry, *, tm):
    i = pl.program_id(0)

    @pl.when(i == 0)
    def _():
        carry[...] = jnp.zeros_like(carry)

    h = _rms(x_ref[...], g_ref[...])
    w = wr_ref[...]
    h_hi, w_hi = h.astype(bf16), w.astype(bf16)
    h_lo, w_lo = (h - h_hi.astype(f32)).astype(bf16), (w - w_hi.astype(f32)).astype(bf16)
    logits = _dot(h_hi, w_hi) + (_dot(h_hi, w_lo) + _dot(h_lo, w_hi))
    lane = lax.broadcasted_iota(jnp.int32, (tm, LANES), 1)
    logits = jnp.where(lane < N_EXPERTS, logits, -jnp.inf)
    lane_f = lane.astype(f32)
    v1 = jnp.max(logits, axis=-1, keepdims=True)
    e1 = jnp.min(jnp.where(logits == v1, lane_f, float(LANES)), axis=-1, keepdims=True).astype(jnp.int32)
    rest = jnp.where(lane == e1, -jnp.inf, logits)
    v2 = jnp.max(rest, axis=-1, keepdims=True)
    e2 = jnp.min(jnp.where(rest == v2, lane_f, float(LANES)), axis=-1, keepdims=True).astype(jnp.int32)
    t = jnp.exp(v2 - v1)
    w1 = 1.0 / (1.0 + t)
    w2 = t / (1.0 + t)

    hot = jnp.where((lane == e1) | (lane == e2), 1.0, 0.0)
    r_io = lax.broadcasted_iota(jnp.int32, (tm, tm), 0)
    c_io = lax.broadcasted_iota(jnp.int32, (tm, tm), 1)
    below = jnp.where(c_io < r_io, 1.0, 0.0).astype(bf16)
    rank = _dot(below, hot.astype(bf16)) + carry[...]
    carry[...] += jnp.sum(hot, axis=0, keepdims=True)
    rank1 = jnp.sum(jnp.where(lane == e1, rank, 0.0), axis=-1, keepdims=True).astype(jnp.int32)
    rank2 = jnp.sum(jnp.where(lane == e2, rank, 0.0), axis=-1, keepdims=True).astype(jnp.int32)

    mi_ref[...] = jnp.where(lane == 0, e1, jnp.where(lane == 1, e2, jnp.where(lane == 2, rank1,
                            jnp.where(lane == 3, rank2, 0))))
    mf_ref[...] = jnp.where(lane == 0, w1, jnp.where(lane == 1, w2, 0.0))
    cnt_ref[...] = carry[...].astype(jnp.int32)


def router(x, g, w_router_pad, tm=512):
    return pl.pallas_call(
        functools.partial(_router_kernel, tm=tm),
        grid=(N_TOK // tm,),
        in_specs=[pl.BlockSpec((tm, D_MODEL), lambda i: (i, 0)),
                  pl.BlockSpec((1, D_MODEL), lambda i: (0, 0)),
                  pl.BlockSpec((D_MODEL, LANES), lambda i: (0, 0))],
        out_specs=[pl.BlockSpec((tm, LANES), lambda i: (i, 0)),
                   pl.BlockSpec((tm, LANES), lambda i: (i, 0)),
                   pl.BlockSpec((1, LANES), lambda i: (0, 0))],
        out_shape=[jax.ShapeDtypeStruct((N_TOK, LANES), jnp.int32),
                   jax.ShapeDtypeStruct((N_TOK, LANES), f32),
                   jax.ShapeDtypeStruct((1, LANES), jnp.int32)],
        scratch_shapes=[pltpu.VMEM((1, LANES), f32)],
        compiler_params=_cparams(("arbitrary",)),
        name="router",
    )(x, g, w_router_pad)


def _dispatch_kernel(p1_ref, p2_ref, hp_ref, xs_in, xs_out, sem, *, tm):
    del xs_in
    base = pl.program_id(0) * tm

    def copy(i, pos_ref):
        return pltpu.make_async_copy(hp_ref.at[pl.ds(i, 1)], xs_out.at[pl.ds(pos_ref[base + i], 1)], sem)

    def start(i, c):
        copy(i, p1_ref).start()
        copy(i, p2_ref).start()
        return c

    def wait(i, c):
        copy(i, p1_ref).wait()
        copy(i, p2_ref).wait()
        return c

    lax.fori_loop(0, tm, start, 0, unroll=8)
    lax.fori_loop(0, tm, wait, 0, unroll=8)


def dispatch(pos1, pos2, hp, xs_zero, tm=512):
    grid_spec = pltpu.PrefetchScalarGridSpec(
        num_scalar_prefetch=2,
        grid=(N_TOK // tm,),
        in_specs=[pl.BlockSpec((tm, HALF_D), lambda i, p1, p2: (i, 0)),
                  pl.BlockSpec(memory_space=pl.ANY)],
        out_specs=pl.BlockSpec(memory_space=pl.ANY),
        scratch_shapes=[pltpu.SemaphoreType.DMA(())],
    )
    return pl.pallas_call(
        functools.partial(_dispatch_kernel, tm=tm),
        grid_spec=grid_spec,
        out_shape=jax.ShapeDtypeStruct(xs_zero.shape, xs_zero.dtype),
        input_output_aliases={3: 0},
        compiler_params=_cparams(("arbitrary",)),
        name="dispatch",
    )(pos1, pos2, hp, xs_zero)


def _combine_kernel(p1_ref, p2_ref, x_ref, mf_ref, y_hbm, o_ref, buf, sem, *, tm):
    i = pl.program_id(0)

    def copy(t, r, k, pos_ref):
        slot = t % 2
        return pltpu.make_async_copy(y_hbm.at[pl.ds(pos_ref[t * tm + r], 1)],
                                     buf.at[slot, k, pl.ds(r, 1)], sem.at[slot])

    def for_rows(fn):
        def body(r, c):
            fn(r)
            return c
        lax.fori_loop(0, tm, body, 0, unroll=8)

    def start_tile(t):
        for_rows(lambda r: (copy(t, r, 0, p1_ref).start(), copy(t, r, 1, p2_ref).start()))

    @pl.when(i == 0)
    def _():
        start_tile(0)

    @pl.when(i + 1 < pl.num_programs(0))
    def _():
        start_tile(i + 1)

    for_rows(lambda r: (copy(i, r, 0, p1_ref).wait(), copy(i, r, 1, p2_ref).wait()))
    w = mf_ref[...]
    slot = i % 2
    o_ref[...] = x_ref[...] + w[:, 0:1] * buf[slot, 0] + w[:, 1:2] * buf[slot, 1]


def combine(pos1, pos2, x, mf, y, tm=256):
    grid_spec = pltpu.PrefetchScalarGridSpec(
        num_scalar_prefetch=2,
        grid=(N_TOK // tm,),
        in_specs=[pl.BlockSpec((tm, D_MODEL), lambda i, p1, p2: (i, 0)),
                  pl.BlockSpec((tm, LANES), lambda i, p1, p2: (i, 0)),
                  pl.BlockSpec(memory_space=pl.ANY)],
        out_specs=pl.BlockSpec((tm, D_MODEL), lambda i, p1, p2: (i, 0)),
        scratch_shapes=[pltpu.VMEM((2, TOP_K, tm, D_MODEL), f32), pltpu.SemaphoreType.DMA((2,))],
    )
    return pl.pallas_call(
        functools.partial(_combine_kernel, tm=tm),
        grid_spec=grid_spec,
        out_shape=jax.ShapeDtypeStruct((N_TOK, D_MODEL), f32),
        compiler_params=_cparams(("arbitrary",)),
        name="combine",
    )(pos1, pos2, x, mf, y)


def _rope_lanes(v):
    z = jnp.zeros(v.shape[:-1] + (QK_ROPE // 2,), v.dtype)
    return jnp.concatenate([v[..., :QK_ROPE // 2], z, v[..., QK_ROPE // 2:], z], axis=-1)


def kernel(x, mem, positions, g_mix, g_ffn, g_mem, w_mem_kv, g_mq, g_mk, w_out, a_w_in, a_conv_w, b_w_in,
           b_g_q_a, b_w_q_b, b_g_qn, g_kv, w_kv_a, g_kv_a, w_kv_b, g_kn, ffn_w_gate, ffn_w_up, ffn_w_down,
           moe_w_router, moe_w_gate, moe_w_up, moe_w_down):
    row = lambda v: v.reshape(1, -1).astype(f32)
    x0 = x.reshape(N_TOK, D_MODEL)
    mem2 = mem.reshape(BATCH * N_MEM, D_MODEL)

    w_kva = jnp.concatenate([w_kv_a[:, :KV_LORA], _rope_lanes(w_kv_a[:, KV_LORA:])], axis=1).astype(bf16)
    kvb = w_kv_b.reshape(KV_LORA, N_HEADS, QK_NOPE + HEAD_DIM)
    w_kvb = jnp.concatenate([kvb[:, :, :QK_NOPE].reshape(KV_LORA, MIX_WIDTH),
                             kvb[:, :, QK_NOPE:].reshape(KV_LORA, MIX_WIDTH)], axis=1).astype(bf16)
    qb = b_w_q_b[0].reshape(Q_LORA, N_HEADS, QK_HEAD)
    w_qb = jnp.concatenate([qb[:, :, :QK_NOPE], _rope_lanes(qb[:, :, QK_NOPE:])], axis=-1)
    w_qb = w_qb.reshape(Q_LORA, N_HEADS * HEAD_PAD).astype(bf16)
    gk_n, gk_r = row(g_kn[:QK_NOPE]), row(_rope_lanes(g_kn[QK_NOPE:]))
    gq_n, gq_r = row(b_g_qn[0, :QK_NOPE]), row(_rope_lanes(b_g_qn[0, QK_NOPE:]))
    w_router_pad = jnp.pad(moe_w_router[0].astype(f32), ((0, 0), (0, LANES - N_EXPERTS)))

    inv_freq = 1.0 / (ROPE_THETA ** (jnp.arange(0, QK_ROPE, 2, dtype=f32) / QK_ROPE))
    ang = positions.reshape(N_TOK, 1).astype(f32) * inv_freq[None, :]
    zeros = jnp.zeros_like(ang)
    cos_t = jnp.concatenate([jnp.cos(ang), zeros, jnp.cos(ang), zeros], axis=-1)
    sin_t = jnp.concatenate([-jnp.sin(ang), zeros, jnp.sin(ang), zeros], axis=-1)

    u0 = norm_matmul(x0, row(g_mix[0]), a_w_in[0].astype(bf16), tm=1024, tn=1024)
    kvm0 = norm_matmul(mem2, row(g_mem[0]), w_mem_kv[0].astype(bf16), tm=BATCH * N_MEM, tn=2 * MEM_WIDTH)
    xa, hp0 = conv_tail(u0, a_conv_w[0].astype(f32), kvm0, row(g_mq[0]), row(g_mk[0]),
                        w_out[0].astype(bf16), x0, row(g_ffn[0]))
    dense_tbl = _super_tiles(jnp.array([N_TOK], jnp.int32), jnp.array([0], jnp.int32),
                             -(-N_TOK // (FFN_SUB * FFN_SUPER)))
    y0 = swiglu_ffn(hp0, ffn_w_gate, ffn_w_up, ffn_w_down, *dense_tbl)

    x1, k, v, q, um1 = mla_kvq(xa, y0, cos_t, sin_t, row(g_kv), w_kva, row(g_kv_a), w_kvb, gk_n, gk_r,
                               row(g_mix[1]), b_w_in[0].astype(bf16), row(b_g_q_a[0]), w_qb, gq_n, gq_r)
    mix1 = flash_attention(q, k, v)
    kvm1 = norm_matmul(mem2, row(g_mem[1]), w_mem_kv[1].astype(bf16), tm=BATCH * N_MEM, tn=2 * MEM_WIDTH)
    x2, hp1 = attn_tail(mix1, um1, 0, kvm1, row(g_mq[1]), row(g_mk[1]), w_out[1].astype(bf16), x1, row(g_ffn[1]))

    mi, mf, cnt = router(x2, row(g_ffn[1]), w_router_pad)
    counts = cnt[0, :N_EXPERTS]
    padded = (counts + FFN_SUB - 1) // FFN_SUB * FFN_SUB
    offsets = jnp.cumsum(padded) - padded
    pos1 = offsets[mi[:, 0]] + mi[:, 2]
    pos2 = offsets[mi[:, 1]] + mi[:, 3]
    xs = dispatch(pos1, pos2, hp1, jnp.zeros((MOE_ROWS, HALF_D), jnp.uint32))
    y1 = swiglu_ffn(xs, moe_w_gate[0], moe_w_up[0], moe_w_down[0], *_super_tiles(padded, offsets, MOE_SUPERS))
    out = combine(pos1, pos2, x2, mf, y1)
    return out.reshape(BATCH, SEQ, D_MODEL)
```

```python
import functools

import jax
import jax.numpy as jnp
from jax import lax
from jax.experimental import pallas as pl
from jax.experimental.pallas import tpu as pltpu

D_MODEL = 2048
BATCH = 2
SEQ = 4096
N_TOK = BATCH * SEQ
N_HEADS = 12
HEAD_DIM = 128
MIX_WIDTH = N_HEADS * HEAD_DIM
MEM_HEADS = 4
MEM_WIDTH = MEM_HEADS * HEAD_DIM
N_MEM = 256
CONV_WIDTH = 3
Q_LORA = 512
KV_LORA = 256
QK_NOPE = 128
QK_ROPE = 64
QK_HEAD = QK_NOPE + QK_ROPE
ROPE_THETA = 10000.0
D_FF = 7168
N_EXPERTS = 8
TOP_K = 2
EPS = 1e-6

LANES = 128
HEAD_PAD = 2 * LANES
HALF_D = D_MODEL // 2
VMEM_LIMIT = 56 * 1024 * 1024

FFN_SUB = 256
FFN_SUPER = 9
FFN_TF = 256
MOE_ROWS = N_TOK * TOP_K + N_EXPERTS * FFN_SUB
MOE_SUPERS = N_EXPERTS + -(-MOE_ROWS // (FFN_SUB * FFN_SUPER))

f32 = jnp.float32
bf16 = jnp.bfloat16


def _cparams(sem):
    return pltpu.CompilerParams(dimension_semantics=sem, vmem_limit_bytes=VMEM_LIMIT)


def _rms(x, g, n=None):
    n = x.shape[-1] if n is None else n
    ms = jnp.sum(x * x, axis=-1, keepdims=True) * (1.0 / n)
    return x * lax.rsqrt(ms + EPS) * g


def _pack_halves(h):
    hb = h.astype(bf16).astype(f32)
    return pltpu.pack_elementwise([hb[:, :HALF_D], hb[:, HALF_D:]], packed_dtype=bf16)


def _unpack_halves(p):
    lo = pltpu.unpack_elementwise(p, index=0, packed_dtype=bf16, unpacked_dtype=f32)
    hi = pltpu.unpack_elementwise(p, index=1, packed_dtype=bf16, unpacked_dtype=f32)
    return lo.astype(bf16), hi.astype(bf16)


def _dot(a, b):
    return jnp.dot(a, b, preferred_element_type=f32)


def _dot_nt(a, b):
    return lax.dot_general(a, b, (((1,), (1,)), ((), ())), preferred_element_type=f32)


def _norm_matmul_kernel(x_ref, g_ref, w_ref, o_ref, h_ref):
    @pl.when(pl.program_id(1) == 0)
    def _():
        h_ref[...] = _rms(x_ref[...], g_ref[...]).astype(bf16)

    o_ref[...] = _dot(h_ref[...], w_ref[...]).astype(o_ref.dtype)


def norm_matmul(x, g, w, tm, tn, out_dtype=bf16):
    m, d = x.shape
    n = w.shape[1]
    return pl.pallas_call(
        _norm_matmul_kernel,
        grid=(m // tm, n // tn),
        in_specs=[pl.BlockSpec((tm, d), lambda i, j: (i, 0)),
                  pl.BlockSpec((1, d), lambda i, j: (0, 0)),
                  pl.BlockSpec((d, tn), lambda i, j: (0, j))],
        out_specs=pl.BlockSpec((tm, tn), lambda i, j: (i, j)),
        out_shape=jax.ShapeDtypeStruct((m, n), out_dtype),
        scratch_shapes=[pltpu.VMEM((tm, d), bf16)],
        compiler_params=_cparams(("arbitrary", "arbitrary")),
        name="norm_matmul",
    )(x, g, w)


def _mem_attention(um_ref, kvm_ref, gq_ref, gk_ref, cat_ref):
    scale = HEAD_DIM ** -0.5
    for h in range(MEM_HEADS):
        cs = slice(h * HEAD_DIM, (h + 1) * HEAD_DIM)
        q = _rms(um_ref[:, cs].astype(f32), gq_ref[...]).astype(bf16)
        k = _rms(kvm_ref[:, cs].astype(f32), gk_ref[...]).astype(bf16)
        v = kvm_ref[:, MEM_WIDTH + h * HEAD_DIM:MEM_WIDTH + (h + 1) * HEAD_DIM]
        s = _dot_nt(q, k) * scale
        p = jnp.exp(s - jnp.max(s, axis=-1, keepdims=True))
        o = _dot(p.astype(bf16), v) / jnp.sum(p, axis=-1, keepdims=True)
        cat_ref[:, MIX_WIDTH + h * HEAD_DIM:MIX_WIDTH + (h + 1) * HEAD_DIM] = o.astype(bf16)


def _tail_epilogue(cat_ref, wout_ref, x_ref, gffn_ref, xo_ref, hp_ref):
    xn = x_ref[...] + _dot(cat_ref[...], wout_ref[...])
    xo_ref[...] = xn
    hp_ref[...] = _pack_halves(_rms(xn, gffn_ref[...]))


def _conv_tail_kernel(xin_ref, gb_ref, gc_ref, um_ref, hx_ref, hg_ref, cw_ref, kvm_ref, gq_ref, gk_ref,
                      wout_ref, x_ref, gffn_ref, xo_ref, hp_ref, cat_ref, *, tm):
    i = pl.program_id(0)
    first = (i * tm) % SEQ == 0
    rows = lax.broadcasted_iota(jnp.int32, (tm, 1), 0)
    cc = 512
    for c in range(MIX_WIDTH // cc):
        cs = slice(c * cc, (c + 1) * cc)
        z = gc_ref[:, cs].astype(f32) * xin_ref[:, cs].astype(f32)
        zh = hg_ref[:, cs].astype(f32) * hx_ref[:, cs].astype(f32)
        zh = jnp.where(first, 0.0, zh)
        p1 = zh[15:16, :]
        p2 = zh[14:15, :]
        z1 = jnp.where(rows == 0, p1, pltpu.roll(z, 1, 0))
        z2 = jnp.where(rows == 0, p2, jnp.where(rows == 1, p1, pltpu.roll(z, 2, 0)))
        y = cw_ref[0:1, cs] * z2 + cw_ref[1:2, cs] * z1 + cw_ref[2:3, cs] * z
        cat_ref[:, cs] = (gb_ref[:, cs].astype(f32) * y).astype(bf16)
    _mem_attention(um_ref, kvm_ref, gq_ref, gk_ref, cat_ref)
    _tail_epilogue(cat_ref, wout_ref, x_ref, gffn_ref, xo_ref, hp_ref)


def _attn_tail_kernel(mix_ref, um_ref, kvm_ref, gq_ref, gk_ref, wout_ref, x_ref, gffn_ref,
                      xo_ref, hp_ref, cat_ref):
    cat_ref[:, :MIX_WIDTH] = mix_ref[...]
    _mem_attention(um_ref, kvm_ref, gq_ref, gk_ref, cat_ref)
    _tail_epilogue(cat_ref, wout_ref, x_ref, gffn_ref, xo_ref, hp_ref)


def _tail_common_specs(tm):
    per_seq = SEQ // tm
    return [pl.BlockSpec((N_MEM, 2 * MEM_WIDTH), lambda i: (i // per_seq, 0)),
            pl.BlockSpec((1, HEAD_DIM), lambda i: (0, 0)),
            pl.BlockSpec((1, HEAD_DIM), lambda i: (0, 0)),
            pl.BlockSpec((D_MODEL, D_MODEL), lambda i: (0, 0)),
            pl.BlockSpec((tm, D_MODEL), lambda i: (i, 0)),
            pl.BlockSpec((1, D_MODEL), lambda i: (0, 0))]


def _tail_outs(tm):
    return dict(
        out_specs=[pl.BlockSpec((tm, D_MODEL), lambda i: (i, 0)),
                   pl.BlockSpec((tm, HALF_D), lambda i: (i, 0))],
        out_shape=[jax.ShapeDtypeStruct((N_TOK, D_MODEL), f32),
                   jax.ShapeDtypeStruct((N_TOK, HALF_D), jnp.uint32)],
        scratch_shapes=[pltpu.VMEM((tm, D_MODEL), bf16)],
        compiler_params=_cparams(("arbitrary",)),
    )


def conv_tail(u, conv_w, kvm, g_mq, g_mk, w_out, x, g_ffn, tm=256):
    halo = 16
    um_blk = 3 * MIX_WIDTH // MEM_WIDTH
    prev = lambda i: jnp.maximum(i * (tm // halo) - 1, 0)
    in_specs = [pl.BlockSpec((tm, MIX_WIDTH), lambda i: (i, 0)),
                pl.BlockSpec((tm, MIX_WIDTH), lambda i: (i, 1)),
                pl.BlockSpec((tm, MIX_WIDTH), lambda i: (i, 2)),
                pl.BlockSpec((tm, MEM_WIDTH), lambda i: (i, um_blk)),
                pl.BlockSpec((halo, MIX_WIDTH), lambda i: (prev(i), 0)),
                pl.BlockSpec((halo, MIX_WIDTH), lambda i: (prev(i), 2)),
                pl.BlockSpec((CONV_WIDTH, MIX_WIDTH), lambda i: (0, 0))] + _tail_common_specs(tm)
    return pl.pallas_call(
        functools.partial(_conv_tail_kernel, tm=tm),
        grid=(N_TOK // tm,), in_specs=in_specs, name="conv_tail", **_tail_outs(tm),
    )(u, u, u, u, u, u, conv_w, kvm, g_mq, g_mk, w_out, x, g_ffn)


def attn_tail(mix, um_src, um_blk, kvm, g_mq, g_mk, w_out, x, g_ffn, tm=256):
    in_specs = [pl.BlockSpec((tm, MIX_WIDTH), lambda i: (i, 0)),
                pl.BlockSpec((tm, MEM_WIDTH), lambda i: (i, um_blk))] + _tail_common_specs(tm)
    return pl.pallas_call(
        _attn_tail_kernel,
        grid=(N_TOK // tm,), in_specs=in_specs, name="attn_tail", **_tail_outs(tm),
    )(mix, um_src, kvm, g_mq, g_mk, w_out, x, g_ffn)


def _ffn_kernel(exp_ref, row0_ref, nsub_ref, tail_ref, xs_hbm, wg_ref, wu_ref, wd_ref, y_hbm,
                x_s, acc, wg_s, wu_s, wd_s, sem):
    s = pl.program_id(0)
    f = pl.program_id(1)
    nsub = nsub_ref[s]
    row0 = row0_ref[s]
    last_step = (s == pl.num_programs(0) - 1) & (f == pl.num_programs(1) - 1)

    def x_copy(j):
        r = pl.multiple_of(j * FFN_SUB, FFN_SUB)
        g = pl.multiple_of(row0 + r, FFN_SUB)
        return pltpu.make_async_copy(xs_hbm.at[pl.ds(g, FFN_SUB)], x_s.at[pl.ds(r, FFN_SUB)], sem)

    def y_copy(j):
        r = pl.multiple_of(j * FFN_SUB, FFN_SUB)
        g = pl.multiple_of(row0 + r, FFN_SUB)
        return pltpu.make_async_copy(acc.at[pl.ds(r, FFN_SUB)], y_hbm.at[pl.ds(g, FFN_SUB)], sem)

    def for_subtiles(fn):
        def body(j, c):
            fn(j)
            return c
        lax.fori_loop(0, nsub, body, 0)

    @pl.when(nsub > 0)
    def _():
        @pl.when(f == 0)
        def _():
            for_subtiles(lambda j: x_copy(j).start())

            def zero(j):
                r = pl.multiple_of(j * FFN_SUB, FFN_SUB)
                acc[pl.ds(r, FFN_SUB), :] = jnp.zeros((FFN_SUB, D_MODEL), f32)
            for_subtiles(zero)
            for_subtiles(lambda j: x_copy(j).wait())

        def swiglu_rows(r, rows, wg, wu, wd):
            lo, hi = _unpack_halves(x_s[pl.ds(r, rows), :])
            g = _dot(lo, wg[:HALF_D, :]) + _dot(hi, wg[HALF_D:, :])
            u = _dot(lo, wu[:HALF_D, :]) + _dot(hi, wu[HALF_D:, :])
            a = (g * jax.nn.sigmoid(g) * u).astype(bf16)
            acc[pl.ds(r, rows), :] += _dot(a, wd[...])

        wg = wg_ref[0].astype(bf16)
        wu = wu_ref[0].astype(bf16)
        wd = wd_ref[0].astype(bf16)
        wg_s[...] = wg
        wu_s[...] = wu
        wd_s[...] = wd
        swiglu_rows(0, FFN_SUB, wg, wu, wd)

        rem = nsub - 1
        n_quad = lax.shift_right_logical(rem, 2)

        def quad(j, c):
            r = pl.multiple_of(FFN_SUB + j * 4 * FFN_SUB, FFN_SUB)
            swiglu_rows(r, 4 * FFN_SUB, wg_s, wu_s, wd_s)
            return c
        lax.fori_loop(0, n_quad, quad, 0)

        @pl.when((rem & 2) != 0)
        def _():
            r = pl.multiple_of(FFN_SUB + n_quad * 4 * FFN_SUB, FFN_SUB)
            swiglu_rows(r, 2 * FFN_SUB, wg_s, wu_s, wd_s)

        @pl.when((rem & 1) != 0)
        def _():
            swiglu_rows(pl.multiple_of(rem * FFN_SUB, FFN_SUB), FFN_SUB, wg_s, wu_s, wd_s)

        @pl.when(f == pl.num_programs(1) - 1)
        def _():
            for_subtiles(lambda j: y_copy(j).start())
            for_subtiles(lambda j: y_copy(j).wait())

    @pl.when(last_step)
    def _():
        acc[pl.ds(0, FFN_SUB), :] = jnp.zeros((FFN_SUB, D_MODEL), f32)

        def tail_copy(j):
            g = pl.multiple_of(j * FFN_SUB, FFN_SUB)
            return pltpu.make_async_copy(acc.at[pl.ds(0, FFN_SUB)], y_hbm.at[pl.ds(g, FFN_SUB)], sem)

        n_tail = y_hbm.shape[0] // FFN_SUB
        lax.fori_loop(tail_ref[0], n_tail, lambda j, c: (tail_copy(j).start(), c)[1], 0)
        lax.fori_loop(tail_ref[0], n_tail, lambda j, c: (tail_copy(j).wait(), c)[1], 0)


def swiglu_ffn(xs, w_gate, w_up, w_down, st_expert, st_row0, st_nsub, st_tail):
    rows = xs.shape[0]
    n_super = st_expert.shape[0]
    nf = D_FF // FFN_TF

    def fblk(f, nsub_ref, s):
        return jnp.where(nsub_ref[s] > 0, f, nf - 1)

    grid_spec = pltpu.PrefetchScalarGridSpec(
        num_scalar_prefetch=4,
        grid=(n_super, nf),
        in_specs=[pl.BlockSpec(memory_space=pl.ANY),
                  pl.BlockSpec((1, D_MODEL, FFN_TF), lambda s, f, e, r, n, t: (e[s], 0, fblk(f, n, s))),
                  pl.BlockSpec((1, D_MODEL, FFN_TF), lambda s, f, e, r, n, t: (e[s], 0, fblk(f, n, s))),
                  pl.BlockSpec((1, FFN_TF, D_MODEL), lambda s, f, e, r, n, t: (e[s], fblk(f, n, s), 0))],
        out_specs=pl.BlockSpec(memory_space=pl.ANY),
        scratch_shapes=[pltpu.VMEM((FFN_SUPER * FFN_SUB, HALF_D), jnp.uint32),
                        pltpu.VMEM((FFN_SUPER * FFN_SUB, D_MODEL), f32),
                        pltpu.VMEM((D_MODEL, FFN_TF), bf16),
                        pltpu.VMEM((D_MODEL, FFN_TF), bf16),
                        pltpu.VMEM((FFN_TF, D_MODEL), bf16),
                        pltpu.SemaphoreType.DMA(())],
    )
    return pl.pallas_call(
        _ffn_kernel,
        grid_spec=grid_spec,
        out_shape=jax.ShapeDtypeStruct((rows, D_MODEL), f32),
        compiler_params=_cparams(("arbitrary", "arbitrary")),
        name="swiglu_ffn",
    )(st_expert, st_row0, st_nsub, st_tail, xs, w_gate, w_up, w_down)


def _super_tiles(padded, offsets, n_super):
    cap = FFN_SUB * FFN_SUPER
    per_group = (padded + cap - 1) // cap
    ends = jnp.cumsum(per_group)
    t = jnp.arange(n_super, dtype=jnp.int32)
    grp = jnp.minimum(jnp.sum((t[:, None] >= ends[None, :]).astype(jnp.int32), axis=1), padded.shape[0] - 1)
    k = t - (ends[grp] - per_group[grp])
    used = t < ends[-1]
    nsub = jnp.where(used, jnp.minimum(FFN_SUPER, (padded[grp] - k * cap) // FFN_SUB), 0)
    row0 = jnp.where(used, offsets[grp] + k * cap, 0)
    last_grp = grp[jnp.maximum(ends[-1] - 1, 0)]
    grp = jnp.where(used, grp, last_grp)
    tail = (jnp.sum(padded) // FFN_SUB).reshape(1)
    return grp.astype(jnp.int32), row0.astype(jnp.int32), nsub.astype(jnp.int32), tail.astype(jnp.int32)


def _rope(x, cos, sin):
    return x * cos + pltpu.roll(x, LANES // 2, 1) * sin


def _dot_split(a, sel):
    hi = a.astype(bf16)
    lo = (a - hi.astype(f32)).astype(bf16)
    return _dot(hi, sel) + _dot(lo, sel)


def _mla_kvq_kernel(x0_ref, y_ref, cos_ref, sin_ref, gkv_ref, wkva_ref, gkva_ref, wkvb_ref, gkn_ref, gkr_ref,
                    gmix_ref, win_ref, gqa_ref, wqb_ref, gqn_ref, gqr_ref, sel_ref,
                    x1_ref, k_ref, v_ref, q_ref, um_ref):
    x1 = x0_ref[...] + y_ref[...]
    x1_ref[...] = x1
    cos = cos_ref[...]
    sin = sin_ref[...]

    kva = _dot(_rms(x1, gkv_ref[...]).astype(bf16), wkva_ref[...])
    c_kv = _rms(kva[:, :KV_LORA], gkva_ref[...]).astype(bf16)
    kv = _dot(c_kv, wkvb_ref[...])
    tm = x1.shape[0]
    kpe = kva[:, KV_LORA:]
    ss_pe = jnp.sum(kpe * kpe, axis=-1, keepdims=True)
    sc = lax.rsqrt((_dot_split(kv[:, :MIX_WIDTH] * kv[:, :MIX_WIDTH], sel_ref[...]) + ss_pe) * (1.0 / QK_HEAD) + EPS)
    k_rot = _rope(kpe * gkr_ref[...], cos, sin)
    for h in range(N_HEADS):
        sch = jnp.broadcast_to(sc[:, h:h + 1], (tm, LANES))
        k_ref[:, h * HEAD_PAD:h * HEAD_PAD + LANES] = (
            kv[:, h * HEAD_DIM:(h + 1) * HEAD_DIM] * sch * gkn_ref[...]).astype(bf16)
        k_ref[:, h * HEAD_PAD + LANES:(h + 1) * HEAD_PAD] = (k_rot * sch).astype(bf16)
    v_ref[...] = kv[:, MIX_WIDTH:].astype(bf16)

    u = _dot(_rms(x1, gmix_ref[...]).astype(bf16), win_ref[...])
    um_ref[...] = u[:, Q_LORA:].astype(bf16)
    c_q = _rms(u[:, :Q_LORA], gqa_ref[...]).astype(bf16)
    q = _dot(c_q, wqb_ref[...])
    q2 = jnp.concatenate([q[:, h * HEAD_PAD:h * HEAD_PAD + LANES] * q[:, h * HEAD_PAD:h * HEAD_PAD + LANES]
                          + q[:, h * HEAD_PAD + LANES:(h + 1) * HEAD_PAD] * q[:, h * HEAD_PAD + LANES:(h + 1) * HEAD_PAD]
                          for h in range(N_HEADS)], axis=-1)
    sc = lax.rsqrt(_dot_split(q2, sel_ref[...]) * (1.0 / QK_HEAD) + EPS)
    for h in range(N_HEADS):
        sch = jnp.broadcast_to(sc[:, h:h + 1], (tm, LANES))
        qn = q[:, h * HEAD_PAD:h * HEAD_PAD + LANES]
        qr = q[:, h * HEAD_PAD + LANES:(h + 1) * HEAD_PAD]
        q_ref[:, h * HEAD_PAD:h * HEAD_PAD + LANES] = (qn * sch * gqn_ref[...]).astype(bf16)
        q_ref[:, h * HEAD_PAD + LANES:(h + 1) * HEAD_PAD] = (_rope(qr * gqr_ref[...], cos, sin) * sch).astype(bf16)


def mla_kvq(x0, y, cos, sin, g_kv, w_kva, g_kva, w_kvb, gk_n, gk_r, g_mix, w_in, g_qa, w_qb, gq_n, gq_r, tm=256):
    row = lambda w: pl.BlockSpec((tm, w), lambda i: (i, 0))
    full = lambda a: pl.BlockSpec(a.shape, lambda i: (0, 0))
    head = jnp.arange(MIX_WIDTH, dtype=jnp.int32) // HEAD_DIM
    sel = (head[:, None] == jnp.arange(LANES, dtype=jnp.int32)[None, :]).astype(bf16)
    consts = (g_kv, w_kva, g_kva, w_kvb, gk_n, gk_r, g_mix, w_in, g_qa, w_qb, gq_n, gq_r, sel)
    return pl.pallas_call(
        _mla_kvq_kernel,
        grid=(N_TOK // tm,),
        in_specs=[row(D_MODEL), row(D_MODEL), row(LANES), row(LANES)] + [full(a) for a in consts],
        out_specs=[row(D_MODEL), row(N_HEADS * HEAD_PAD), row(MIX_WIDTH), row(N_HEADS * HEAD_PAD), row(MEM_WIDTH)],
        out_shape=[jax.ShapeDtypeStruct((N_TOK, D_MODEL), f32),
                   jax.ShapeDtypeStruct((N_TOK, N_HEADS * HEAD_PAD), bf16),
                   jax.ShapeDtypeStruct((N_TOK, MIX_WIDTH), bf16),
                   jax.ShapeDtypeStruct((N_TOK, N_HEADS * HEAD_PAD), bf16),
                   jax.ShapeDtypeStruct((N_TOK, MEM_WIDTH), bf16)],
        compiler_params=_cparams(("arbitrary",)),
        name="mla_kvq",
    )(x0, y, cos, sin, *consts)


def _flash_kernel(q_ref, k_ref, v_ref, o_ref, m_s, l_s, acc_s, *, tq, heads):
    qi = pl.program_id(2)
    c_exp = (QK_HEAD ** -0.5) * 1.4426950408889634
    m_s[...] = jnp.full(m_s.shape, -jnp.inf, f32)
    l_s[...] = jnp.zeros(l_s.shape, f32)
    acc_s[...] = jnp.zeros(acc_s.shape, f32)
    n_blk = tq // LANES

    def chunk(kj, masked):
        r = pl.multiple_of(kj * tq, tq)

        def scores(h):
            q = q_ref[:, h * HEAD_PAD:(h + 1) * HEAD_PAD]
            s = _dot_nt(q, k_ref[pl.ds(r, tq), h * HEAD_PAD:(h + 1) * HEAD_PAD])
            if masked:
                qpos = lax.broadcasted_iota(jnp.int32, (tq, tq), 0)
                kpos = lax.broadcasted_iota(jnp.int32, (tq, tq), 1)
                s = jnp.where(kpos <= qpos, s, -jnp.inf)
            return s

        def update(h, s):
            blocks = [s[:, b * LANES:(b + 1) * LANES] for b in range(n_blk)]
            lane_max = functools.reduce(jnp.maximum, blocks)
            m_old = m_s[h]
            m_new = jnp.maximum(m_old, jnp.max(lane_max, axis=-1, keepdims=True))
            alpha = jnp.exp2((m_old - m_new) * c_exp)
            p = [jnp.exp2((blk - m_new) * c_exp) for blk in blocks]
            l_s[h] = alpha * l_s[h] + functools.reduce(jnp.add, p)
            pv = _dot(jnp.concatenate(p, axis=-1).astype(bf16),
                      v_ref[pl.ds(r, tq), h * HEAD_DIM:(h + 1) * HEAD_DIM])
            acc_s[h] = alpha * acc_s[h] + pv
            m_s[h] = m_new

        s_next = scores(0)
        for h in range(heads):
            s_cur = s_next
            if h + 1 < heads:
                s_next = scores(h + 1)
            update(h, s_cur)

    def body(kj, c):
        chunk(kj, False)
        return c

    lax.fori_loop(0, qi, body, 0)
    chunk(qi, True)
    for h in range(heads):
        l = jnp.sum(l_s[h], axis=-1, keepdims=True)
        o_ref[:, h * HEAD_DIM:(h + 1) * HEAD_DIM] = (acc_s[h] / l).astype(o_ref.dtype)


def flash_attention(q, k, v, tq=512, heads=4):
    per_seq = SEQ // tq
    return pl.pallas_call(
        functools.partial(_flash_kernel, tq=tq, heads=heads),
        grid=(BATCH, N_HEADS // heads, per_seq),
        in_specs=[pl.BlockSpec((tq, heads * HEAD_PAD), lambda b, h, i: (b * per_seq + i, h)),
                  pl.BlockSpec((SEQ, heads * HEAD_PAD), lambda b, h, i: (b, h)),
                  pl.BlockSpec((SEQ, heads * HEAD_DIM), lambda b, h, i: (b, h))],
        out_specs=pl.BlockSpec((tq, heads * HEAD_DIM), lambda b, h, i: (b * per_seq + i, h)),
        out_shape=jax.ShapeDtypeStruct((N_TOK, MIX_WIDTH), bf16),
        scratch_shapes=[pltpu.VMEM((heads, tq, LANES), f32),
                        pltpu.VMEM((heads, tq, LANES), f32),
                        pltpu.VMEM((heads, tq, HEAD_DIM), f32)],
        compiler_params=_cparams(("arbitrary", "arbitrary", "arbitrary")),
        name="flash_attention",
    )(q, k, v)


def _router_kernel(x_ref, g_ref, wr_ref, mi_ref, mf_ref, cnt_ref, carry, *, tm):
    i = pl.program_id(0)

    @pl.when(i == 0)
    def _():
        carry[...] = jnp.zeros_like(carry)

    h = _rms(x_ref[...], g_ref[...])
    w = wr_ref[...]
    h_hi, w_hi = h.astype(bf16), w.astype(bf16)
    h_lo, w_lo = (h - h_hi.astype(f32)).astype(bf16), (w - w_hi.astype(f32)).astype(bf16)
    logits = _dot(h_hi, w_hi) + (_dot(h_hi, w_lo) + _dot(h_lo, w_hi))
    lane = lax.broadcasted_iota(jnp.int32, (tm, LANES), 1)
    logits = jnp.where(lane < N_EXPERTS, logits, -jnp.inf)
    lane_f = lane.astype(f32)
    v1 = jnp.max(logits, axis=-1, keepdims=True)
    e1 = jnp.min(jnp.where(logits == v1, lane_f, float(LANES)), axis=-1, keepdims=True).astype(jnp.int32)
    rest = jnp.where(lane == e1, -jnp.inf, logits)
    v2 = jnp.max(rest, axis=-1, keepdims=True)
    e2 = jnp.min(jnp.where(rest == v2, lane_f, float(LANES)), axis=-1, keepdims=True).astype(jnp.int32)
    t = jnp.exp(v2 - v1)
    w1 = 1.0 / (1.0 + t)
    w2 = t / (1.0 + t)

    hot = jnp.where((lane == e1) | (lane == e2), 1.0, 0.0)
    r_io = lax.broadcasted_iota(jnp.int32, (tm, tm), 0)
    c_io = lax.broadcasted_iota(jnp.int32, (tm, tm), 1)
    below = jnp.where(c_io < r_io, 1.0, 0.0).astype(bf16)
    rank = _dot(below, hot.astype(bf16)) + carry[...]
    carry[...] += jnp.sum(hot, axis=0, keepdims=True)
    rank1 = jnp.sum(jnp.where(lane == e1, rank, 0.0), axis=-1, keepdims=True).astype(jnp.int32)
    rank2 = jnp.sum(jnp.where(lane == e2, rank, 0.0), axis=-1, keepdims=True).astype(jnp.int32)

    mi_ref[...] = jnp.where(lane == 0, e1, jnp.where(lane == 1, e2, jnp.where(lane == 2, rank1,
                            jnp.where(lane == 3, rank2, 0))))
    mf_ref[...] = jnp.where(lane == 0, w1, jnp.where(lane == 1, w2, 0.0))
    cnt_ref[...] = carry[...].astype(jnp.int32)


def router(x, g, w_router_pad, tm=512):
    return pl.pallas_call(
        functools.partial(_router_kernel, tm=tm),
        grid=(N_TOK // tm,),
        in_specs=[pl.BlockSpec((tm, D_MODEL), lambda i: (i, 0)),
                  pl.BlockSpec((1, D_MODEL), lambda i: (0, 0)),
                  pl.BlockSpec((D_MODEL, LANES), lambda i: (0, 0))],
        out_specs=[pl.BlockSpec((tm, LANES), lambda i: (i, 0)),
                   pl.BlockSpec((tm, LANES), lambda i: (i, 0)),
                   pl.BlockSpec((1, LANES), lambda i: (0, 0))],
        out_shape=[jax.ShapeDtypeStruct((N_TOK, LANES), jnp.int32),
                   jax.ShapeDtypeStruct((N_TOK, LANES), f32),
                   jax.ShapeDtypeStruct((1, LANES), jnp.int32)],
        scratch_shapes=[pltpu.VMEM((1, LANES), f32)],
        compiler_params=_cparams(("arbitrary",)),
        name="router",
    )(x, g, w_router_pad)


def _dispatch_kernel(p1_ref, p2_ref, hp_ref, xs_in, xs_out, sem, *, tm):
    del xs_in
    base = pl.program_id(0) * tm

    def copy(i, pos_ref):
        return pltpu.make_async_copy(hp_ref.at[pl.ds(i, 1)], xs_out.at[pl.ds(pos_ref[base + i], 1)], sem)

    def start(i, c):
        copy(i, p1_ref).start()
        copy(i, p2_ref).start()
        return c

    def wait(i, c):
        copy(i, p1_ref).wait()
        copy(i, p2_ref).wait()
        return c

    lax.fori_loop(0, tm, start, 0, unroll=8)
    lax.fori_loop(0, tm, wait, 0, unroll=8)


def dispatch(pos1, pos2, hp, xs_zero, tm=512):
    grid_spec = pltpu.PrefetchScalarGridSpec(
        num_scalar_prefetch=2,
        grid=(N_TOK // tm,),
        in_specs=[pl.BlockSpec((tm, HALF_D), lambda i, p1, p2: (i, 0)),
                  pl.BlockSpec(memory_space=pl.ANY)],
        out_specs=pl.BlockSpec(memory_space=pl.ANY),
        scratch_shapes=[pltpu.SemaphoreType.DMA(())],
    )
    return pl.pallas_call(
        functools.partial(_dispatch_kernel, tm=tm),
        grid_spec=grid_spec,
        out_shape=jax.ShapeDtypeStruct(xs_zero.shape, xs_zero.dtype),
        input_output_aliases={3: 0},
        compiler_params=_cparams(("arbitrary",)),
        name="dispatch",
    )(pos1, pos2, hp, xs_zero)


def _combine_kernel(p1_ref, p2_ref, x_ref, mf_ref, y_hbm, o_ref, buf, sem, *, tm):
    i = pl.program_id(0)

    def copy(t, r, k, pos_ref):
        slot = t % 2
        return pltpu.make_async_copy(y_hbm.at[pl.ds(pos_ref[t * tm + r], 1)],
                                     buf.at[slot, k, pl.ds(r, 1)], sem.at[slot])

    def for_rows(fn):
        def body(r, c):
            fn(r)
            return c
        lax.fori_loop(0, tm, body, 0, unroll=8)

    def start_tile(t):
        for_rows(lambda r: (copy(t, r, 0, p1_ref).start(), copy(t, r, 1, p2_ref).start()))

    @pl.when(i == 0)
    def _():
        start_tile(0)

    @pl.when(i + 1 < pl.num_programs(0))
    def _():
        start_tile(i + 1)

    for_rows(lambda r: (copy(i, r, 0, p1_ref).wait(), copy(i, r, 1, p2_ref).wait()))
    w = mf_ref[...]
    slot = i % 2
    o_ref[...] = x_ref[...] + w[:, 0:1] * buf[slot, 0] + w[:, 1:2] * buf[slot, 1]


def combine(pos1, pos2, x, mf, y, tm=256):
    grid_spec = pltpu.PrefetchScalarGridSpec(
        num_scalar_prefetch=2,
        grid=(N_TOK // tm,),
        in_specs=[pl.BlockSpec((tm, D_MODEL), lambda i, p1, p2: (i, 0)),
                  pl.BlockSpec((tm, LANES), lambda i, p1, p2: (i, 0)),
                  pl.BlockSpec(memory_space=pl.ANY)],
        out_specs=pl.BlockSpec((tm, D_MODEL), lambda i, p1, p2: (i, 0)),
        scratch_shapes=[pltpu.VMEM((2, TOP_K, tm, D_MODEL), f32), pltpu.SemaphoreType.DMA((2,))],
    )
    return pl.pallas_call(
        functools.partial(_combine_kernel, tm=tm),
        grid_spec=grid_spec,
        out_shape=jax.ShapeDtypeStruct((N_TOK, D_MODEL), f32),
        compiler_params=_cparams(("arbitrary",)),
        name="combine",
    )(pos1, pos2, x, mf, y)


def _rope_lanes(v):
    z = jnp.zeros(v.shape[:-1] + (QK_ROPE // 2,), v.dtype)
    return jnp.concatenate([v[..., :QK_ROPE // 2], z, v[..., QK_ROPE // 2:], z], axis=-1)


def kernel(x, mem, positions, g_mix, g_ffn, g_mem, w_mem_kv, g_mq, g_mk, w_out, a_w_in, a_conv_w, b_w_in,
           b_g_q_a, b_w_q_b, b_g_qn, g_kv, w_kv_a, g_kv_a, w_kv_b, g_kn, ffn_w_gate, ffn_w_up, ffn_w_down,
           moe_w_router, moe_w_gate, moe_w_up, moe_w_down):
    row = lambda v: v.reshape(1, -1).astype(f32)
    x0 = x.reshape(N_TOK, D_MODEL)
    mem2 = mem.reshape(BATCH * N_MEM, D_MODEL)

    w_kva = jnp.concatenate([w_kv_a[:, :KV_LORA], _rope_lanes(w_kv_a[:, KV_LORA:])], axis=1).astype(bf16)
    kvb = w_kv_b.reshape(KV_LORA, N_HEADS, QK_NOPE + HEAD_DIM)
    w_kvb = jnp.concatenate([kvb[:, :, :QK_NOPE].reshape(KV_LORA, MIX_WIDTH),
                             kvb[:, :, QK_NOPE:].reshape(KV_LORA, MIX_WIDTH)], axis=1).astype(bf16)
    qb = b_w_q_b[0].reshape(Q_LORA, N_HEADS, QK_HEAD)
    w_qb = jnp.concatenate([qb[:, :, :QK_NOPE], _rope_lanes(qb[:, :, QK_NOPE:])], axis=-1)
    w_qb = w_qb.reshape(Q_LORA, N_HEADS * HEAD_PAD).astype(bf16)
    gk_n, gk_r = row(g_kn[:QK_NOPE]), row(_rope_lanes(g_kn[QK_NOPE:]))
    gq_n, gq_r = row(b_g_qn[0, :QK_NOPE]), row(_rope_lanes(b_g_qn[0, QK_NOPE:]))
    w_router_pad = jnp.pad(moe_w_router[0].astype(f32), ((0, 0), (0, LANES - N_EXPERTS)))

    inv_freq = 1.0 / (ROPE_THETA ** (jnp.arange(0, QK_ROPE, 2, dtype=f32) / QK_ROPE))
    ang = positions.reshape(N_TOK, 1).astype(f32) * inv_freq[None, :]
    zeros = jnp.zeros_like(ang)
    cos_t = jnp.concatenate([jnp.cos(ang), zeros, jnp.cos(ang), zeros], axis=-1)
    sin_t = jnp.concatenate([-jnp.sin(ang), zeros, jnp.sin(ang), zeros], axis=-1)

    u0 = norm_matmul(x0, row(g_mix[0]), a_w_in[0].astype(bf16), tm=1024, tn=1024)
    kvm0 = norm_matmul(mem2, row(g_mem[0]), w_mem_kv[0].astype(bf16), tm=BATCH * N_MEM, tn=2 * MEM_WIDTH)
    xa, hp0 = conv_tail(u0, a_conv_w[0].astype(f32), kvm0, row(g_mq[0]), row(g_mk[0]),
                        w_out[0].astype(bf16), x0, row(g_ffn[0]))
    dense_tbl = _super_tiles(jnp.array([N_TOK], jnp.int32), jnp.array([0], jnp.int32),
                             -(-N_TOK // (FFN_SUB * FFN_SUPER)))
    y0 = swiglu_ffn(hp0, ffn_w_gate.astype(bf16), ffn_w_up.astype(bf16), ffn_w_down.astype(bf16), *dense_tbl)

    x1, k, v, q, um1 = mla_kvq(xa, y0, cos_t, sin_t, row(g_kv), w_kva, row(g_kv_a), w_kvb, gk_n, gk_r,
                               row(g_mix[1]), b_w_in[0].astype(bf16), row(b_g_q_a[0]), w_qb, gq_n, gq_r)
    mix1 = flash_attention(q, k, v)
    kvm1 = norm_matmul(mem2, row(g_mem[1]), w_mem_kv[1].astype(bf16), tm=BATCH * N_MEM, tn=2 * MEM_WIDTH)
    x2, hp1 = attn_tail(mix1, um1, 0, kvm1, row(g_mq[1]), row(g_mk[1]), w_out[1].astype(bf16), x1, row(g_ffn[1]))

    mi, mf, cnt = router(x2, row(g_ffn[1]), w_router_pad)
    counts = cnt[0, :N_EXPERTS]
    padded = (counts + FFN_SUB - 1) // FFN_SUB * FFN_SUB
    offsets = jnp.cumsum(padded) - padded
    pos1 = offsets[mi[:, 0]] + mi[:, 2]
    pos2 = offsets[mi[:, 1]] + mi[:, 3]
    xs = dispatch(pos1, pos2, hp1, jnp.zeros((MOE_ROWS, HALF_D), jnp.uint32))
    y1 = swiglu_ffn(xs, moe_w_gate[0], moe_w_up[0], moe_w_down[0], *_super_tiles(padded, offsets, MOE_SUPERS))
    out = combine(pos1, pos2, x2, mf, y1)
    return out.reshape(BATCH, SEQ, D_MODEL)
```

```python
import functools

import jax
import jax.numpy as jnp
from jax import lax
from jax.experimental import pallas as pl
from jax.experimental.pallas import tpu as pltpu

D_MODEL = 2048
BATCH = 2
SEQ = 4096
N_TOK = BATCH * SEQ
N_HEADS = 12
HEAD_DIM = 128
MIX_WIDTH = N_HEADS * HEAD_DIM
MEM_HEADS = 4
MEM_WIDTH = MEM_HEADS * HEAD_DIM
N_MEM = 256
CONV_WIDTH = 3
Q_LORA = 512
KV_LORA = 256
QK_NOPE = 128
QK_ROPE = 64
QK_HEAD = QK_NOPE + QK_ROPE
ROPE_THETA = 10000.0
D_FF = 7168
N_EXPERTS = 8
TOP_K = 2
EPS = 1e-6

LANES = 128
HEAD_PAD = 2 * LANES
HALF_D = D_MODEL // 2
VMEM_LIMIT = 56 * 1024 * 1024

FFN_SUB = 256
FFN_SUPER = 9
FFN_TF = 256
MOE_ROWS = N_TOK * TOP_K + N_EXPERTS * FFN_SUB
MOE_SUPERS = N_EXPERTS + -(-MOE_ROWS // (FFN_SUB * FFN_SUPER))

f32 = jnp.float32
bf16 = jnp.bfloat16


def _cparams(sem):
    return pltpu.CompilerParams(dimension_semantics=sem, vmem_limit_bytes=VMEM_LIMIT)


def _rms(x, g, n=None):
    n = x.shape[-1] if n is None else n
    ms = jnp.sum(x * x, axis=-1, keepdims=True) * (1.0 / n)
    return x * lax.rsqrt(ms + EPS) * g


def _pack_halves(h):
    hb = h.astype(bf16).astype(f32)
    return pltpu.pack_elementwise([hb[:, :HALF_D], hb[:, HALF_D:]], packed_dtype=bf16)


def _unpack_halves(p):
    lo = pltpu.unpack_elementwise(p, index=0, packed_dtype=bf16, unpacked_dtype=f32)
    hi = pltpu.unpack_elementwise(p, index=1, packed_dtype=bf16, unpacked_dtype=f32)
    return lo.astype(bf16), hi.astype(bf16)


def _dot(a, b):
    return jnp.dot(a, b, preferred_element_type=f32)


def _dot_nt(a, b):
    return lax.dot_general(a, b, (((1,), (1,)), ((), ())), preferred_element_type=f32)


def _norm_matmul_kernel(x_ref, g_ref, w_ref, o_ref, h_ref):
    @pl.when(pl.program_id(1) == 0)
    def _():
        h_ref[...] = _rms(x_ref[...], g_ref[...]).astype(bf16)

    o_ref[...] = _dot(h_ref[...], w_ref[...]).astype(o_ref.dtype)


def norm_matmul(x, g, w, tm, tn, out_dtype=bf16):
    m, d = x.shape
    n = w.shape[1]
    return pl.pallas_call(
        _norm_matmul_kernel,
        grid=(m // tm, n // tn),
        in_specs=[pl.BlockSpec((tm, d), lambda i, j: (i, 0)),
                  pl.BlockSpec((1, d), lambda i, j: (0, 0)),
                  pl.BlockSpec((d, tn), lambda i, j: (0, j))],
        out_specs=pl.BlockSpec((tm, tn), lambda i, j: (i, j)),
        out_shape=jax.ShapeDtypeStruct((m, n), out_dtype),
        scratch_shapes=[pltpu.VMEM((tm, d), bf16)],
        compiler_params=_cparams(("arbitrary", "arbitrary")),
        name="norm_matmul",
    )(x, g, w)


def _mem_attention(um_ref, kvm_ref, gq_ref, gk_ref, cat_ref):
    scale = HEAD_DIM ** -0.5
    for h in range(MEM_HEADS):
        cs = slice(h * HEAD_DIM, (h + 1) * HEAD_DIM)
        q = _rms(um_ref[:, cs].astype(f32), gq_ref[...]).astype(bf16)
        k = _rms(kvm_ref[:, cs].astype(f32), gk_ref[...]).astype(bf16)
        v = kvm_ref[:, MEM_WIDTH + h * HEAD_DIM:MEM_WIDTH + (h + 1) * HEAD_DIM]
        s = _dot_nt(q, k) * scale
        p = jnp.exp(s - jnp.max(s, axis=-1, keepdims=True))
        o = _dot(p.astype(bf16), v) / jnp.sum(p, axis=-1, keepdims=True)
        cat_ref[:, MIX_WIDTH + h * HEAD_DIM:MIX_WIDTH + (h + 1) * HEAD_DIM] = o.astype(bf16)


def _tail_epilogue(cat_ref, wout_ref, x_ref, gffn_ref, xo_ref, hp_ref):
    xn = x_ref[...] + _dot(cat_ref[...], wout_ref[...])
    xo_ref[...] = xn
    hp_ref[...] = _pack_halves(_rms(xn, gffn_ref[...]))


def _conv_tail_kernel(xin_ref, gb_ref, gc_ref, um_ref, hx_ref, hg_ref, cw_ref, kvm_ref, gq_ref, gk_ref,
                      wout_ref, x_ref, gffn_ref, xo_ref, hp_ref, cat_ref, *, tm):
    i = pl.program_id(0)
    first = (i * tm) % SEQ == 0
    rows = lax.broadcasted_iota(jnp.int32, (tm, 1), 0)
    cc = 512
    for c in range(MIX_WIDTH // cc):
        cs = slice(c * cc, (c + 1) * cc)
        z = gc_ref[:, cs].astype(f32) * xin_ref[:, cs].astype(f32)
        zh = hg_ref[:, cs].astype(f32) * hx_ref[:, cs].astype(f32)
        zh = jnp.where(first, 0.0, zh)
        p1 = zh[15:16, :]
        p2 = zh[14:15, :]
        z1 = jnp.where(rows == 0, p1, pltpu.roll(z, 1, 0))
        z2 = jnp.where(rows == 0, p2, jnp.where(rows == 1, p1, pltpu.roll(z, 2, 0)))
        y = cw_ref[0:1, cs] * z2 + cw_ref[1:2, cs] * z1 + cw_ref[2:3, cs] * z
        cat_ref[:, cs] = (gb_ref[:, cs].astype(f32) * y).astype(bf16)
    _mem_attention(um_ref, kvm_ref, gq_ref, gk_ref, cat_ref)
    _tail_epilogue(cat_ref, wout_ref, x_ref, gffn_ref, xo_ref, hp_ref)


def _attn_tail_kernel(mix_ref, um_ref, kvm_ref, gq_ref, gk_ref, wout_ref, x_ref, gffn_ref,
                      xo_ref, hp_ref, cat_ref):
    cat_ref[:, :MIX_WIDTH] = mix_ref[...]
    _mem_attention(um_ref, kvm_ref, gq_ref, gk_ref, cat_ref)
    _tail_epilogue(cat_ref, wout_ref, x_ref, gffn_ref, xo_ref, hp_ref)


def _tail_common_specs(tm):
    per_seq = SEQ // tm
    return [pl.BlockSpec((N_MEM, 2 * MEM_WIDTH), lambda i: (i // per_seq, 0)),
            pl.BlockSpec((1, HEAD_DIM), lambda i: (0, 0)),
            pl.BlockSpec((1, HEAD_DIM), lambda i: (0, 0)),
            pl.BlockSpec((D_MODEL, D_MODEL), lambda i: (0, 0)),
            pl.BlockSpec((tm, D_MODEL), lambda i: (i, 0)),
            pl.BlockSpec((1, D_MODEL), lambda i: (0, 0))]


def _tail_outs(tm):
    return dict(
        out_specs=[pl.BlockSpec((tm, D_MODEL), lambda i: (i, 0)),
                   pl.BlockSpec((tm, HALF_D), lambda i: (i, 0))],
        out_shape=[jax.ShapeDtypeStruct((N_TOK, D_MODEL), f32),
                   jax.ShapeDtypeStruct((N_TOK, HALF_D), jnp.uint32)],
        scratch_shapes=[pltpu.VMEM((tm, D_MODEL), bf16)],
        compiler_params=_cparams(("arbitrary",)),
    )


def conv_tail(u, conv_w, kvm, g_mq, g_mk, w_out, x, g_ffn, tm=256):
    halo = 16
    um_blk = 3 * MIX_WIDTH // MEM_WIDTH
    prev = lambda i: jnp.maximum(i * (tm // halo) - 1, 0)
    in_specs = [pl.BlockSpec((tm, MIX_WIDTH), lambda i: (i, 0)),
                pl.BlockSpec((tm, MIX_WIDTH), lambda i: (i, 1)),
                pl.BlockSpec((tm, MIX_WIDTH), lambda i: (i, 2)),
                pl.BlockSpec((tm, MEM_WIDTH), lambda i: (i, um_blk)),
                pl.BlockSpec((halo, MIX_WIDTH), lambda i: (prev(i), 0)),
                pl.BlockSpec((halo, MIX_WIDTH), lambda i: (prev(i), 2)),
                pl.BlockSpec((CONV_WIDTH, MIX_WIDTH), lambda i: (0, 0))] + _tail_common_specs(tm)
    return pl.pallas_call(
        functools.partial(_conv_tail_kernel, tm=tm),
        grid=(N_TOK // tm,), in_specs=in_specs, name="conv_tail", **_tail_outs(tm),
    )(u, u, u, u, u, u, conv_w, kvm, g_mq, g_mk, w_out, x, g_ffn)


def attn_tail(mix, um_src, um_blk, kvm, g_mq, g_mk, w_out, x, g_ffn, tm=256):
    in_specs = [pl.BlockSpec((tm, MIX_WIDTH), lambda i: (i, 0)),
                pl.BlockSpec((tm, MEM_WIDTH), lambda i: (i, um_blk))] + _tail_common_specs(tm)
    return pl.pallas_call(
        _attn_tail_kernel,
        grid=(N_TOK // tm,), in_specs=in_specs, name="attn_tail", **_tail_outs(tm),
    )(mix, um_src, kvm, g_mq, g_mk, w_out, x, g_ffn)


def _ffn_kernel(exp_ref, row0_ref, nsub_ref, tail_ref, xs_hbm, wg_ref, wu_ref, wd_ref, y_hbm,
                x_s, acc, wg_s, wu_s, wd_s, sem):
    s = pl.program_id(0)
    f = pl.program_id(1)
    nsub = nsub_ref[s]
    row0 = row0_ref[s]
    last_f = f == pl.num_programs(1) - 1
    last_step = (s == pl.num_programs(0) - 1) & last_f

    def x_copy(j):
        r = pl.multiple_of(j * FFN_SUB, FFN_SUB)
        g = pl.multiple_of(row0 + r, FFN_SUB)
        return pltpu.make_async_copy(xs_hbm.at[pl.ds(g, FFN_SUB)], x_s.at[pl.ds(r, FFN_SUB)], sem)

    def y_copy(r, rows):
        g = pl.multiple_of(row0 + r, FFN_SUB)
        return pltpu.make_async_copy(acc.at[pl.ds(r, rows)], y_hbm.at[pl.ds(g, rows)], sem)

    def for_subtiles(fn):
        def body(j, c):
            fn(j)
            return c
        lax.fori_loop(0, nsub, body, 0)

    @pl.when(f == 0)
    def _():
        for_subtiles(lambda j: x_copy(j).start())

        def zero(j):
            r = pl.multiple_of(j * FFN_SUB, FFN_SUB)
            acc[pl.ds(r, FFN_SUB), :] = jnp.zeros((FFN_SUB, D_MODEL), f32)
        for_subtiles(zero)
        for_subtiles(lambda j: x_copy(j).wait())

    def swiglu_rows(r, rows, wg, wu, wd):
        lo, hi = _unpack_halves(x_s[pl.ds(r, rows), :])
        g = _dot(lo, wg[:HALF_D, :]) + _dot(hi, wg[HALF_D:, :])
        u = _dot(lo, wu[:HALF_D, :]) + _dot(hi, wu[HALF_D:, :])
        a = (g * jax.nn.sigmoid(g) * u).astype(bf16)
        acc[pl.ds(r, rows), :] += _dot(a, wd[...])

        @pl.when(last_f)
        def _():
            y_copy(r, rows).start()

    rem = nsub - 1
    n_quad = lax.shift_right_logical(rem, 2)
    quad_row = lambda j: pl.multiple_of(FFN_SUB + j * 4 * FFN_SUB, FFN_SUB)
    pair_row = pl.multiple_of(FFN_SUB + n_quad * 4 * FFN_SUB, FFN_SUB)
    last_row = pl.multiple_of(rem * FFN_SUB, FFN_SUB)

    def for_groups(first, fn):
        first()

        def body(j, c):
            fn(quad_row(j), 4 * FFN_SUB)
            return c
        lax.fori_loop(0, n_quad, body, 0)

        @pl.when((rem & 2) != 0)
        def _():
            fn(pair_row, 2 * FFN_SUB)

        @pl.when((rem & 1) != 0)
        def _():
            fn(last_row, FFN_SUB)

    def first_group():
        wg = wg_ref[0].astype(bf16)
        wu = wu_ref[0].astype(bf16)
        wd = wd_ref[0].astype(bf16)
        wg_s[...] = wg
        wu_s[...] = wu
        wd_s[...] = wd
        swiglu_rows(0, FFN_SUB, wg, wu, wd)

    for_groups(first_group, lambda r, rows: swiglu_rows(r, rows, wg_s, wu_s, wd_s))

    @pl.when(last_f)
    def _():
        for_groups(lambda: y_copy(0, FFN_SUB).wait(), lambda r, rows: y_copy(r, rows).wait())

    @pl.when(last_step)
    def _():
        acc[pl.ds(0, FFN_SUB), :] = jnp.zeros((FFN_SUB, D_MODEL), f32)

        def tail_copy(j):
            g = pl.multiple_of(j * FFN_SUB, FFN_SUB)
            return pltpu.make_async_copy(acc.at[pl.ds(0, FFN_SUB)], y_hbm.at[pl.ds(g, FFN_SUB)], sem)

        n_tail = y_hbm.shape[0] // FFN_SUB
        lax.fori_loop(tail_ref[0], n_tail, lambda j, c: (tail_copy(j).start(), c)[1], 0)
        lax.fori_loop(tail_ref[0], n_tail, lambda j, c: (tail_copy(j).wait(), c)[1], 0)


def swiglu_ffn(xs, w_gate, w_up, w_down, st_expert, st_row0, st_nsub, st_tail):
    rows = xs.shape[0]
    nf = D_FF // FFN_TF

    grid_spec = pltpu.PrefetchScalarGridSpec(
        num_scalar_prefetch=4,
        grid=(st_tail[1], nf),
        in_specs=[pl.BlockSpec(memory_space=pl.ANY),
                  pl.BlockSpec((1, D_MODEL, FFN_TF), lambda s, f, e, r, n, t: (e[s], 0, f)),
                  pl.BlockSpec((1, D_MODEL, FFN_TF), lambda s, f, e, r, n, t: (e[s], 0, f)),
                  pl.BlockSpec((1, FFN_TF, D_MODEL), lambda s, f, e, r, n, t: (e[s], f, 0))],
        out_specs=pl.BlockSpec(memory_space=pl.ANY),
        scratch_shapes=[pltpu.VMEM((FFN_SUPER * FFN_SUB, HALF_D), jnp.uint32),
                        pltpu.VMEM((FFN_SUPER * FFN_SUB, D_MODEL), f32),
                        pltpu.VMEM((D_MODEL, FFN_TF), bf16),
                        pltpu.VMEM((D_MODEL, FFN_TF), bf16),
                        pltpu.VMEM((FFN_TF, D_MODEL), bf16),
                        pltpu.SemaphoreType.DMA(())],
    )
    return pl.pallas_call(
        _ffn_kernel,
        grid_spec=grid_spec,
        out_shape=jax.ShapeDtypeStruct((rows, D_MODEL), f32),
        compiler_params=_cparams(("arbitrary", "arbitrary")),
        name="swiglu_ffn",
    )(st_expert, st_row0, st_nsub, st_tail, xs, w_gate, w_up, w_down)


def _super_tiles(padded, offsets, n_super):
    cap = FFN_SUB * FFN_SUPER
    per_group = (padded + cap - 1) // cap
    ends = jnp.cumsum(per_group)
    t = jnp.arange(n_super, dtype=jnp.int32)
    grp = jnp.minimum(jnp.sum((t[:, None] >= ends[None, :]).astype(jnp.int32), axis=1), padded.shape[0] - 1)
    k = t - (ends[grp] - per_group[grp])
    used = t < ends[-1]
    nsub = jnp.where(used, jnp.minimum(FFN_SUPER, (padded[grp] - k * cap) // FFN_SUB), 0)
    row0 = jnp.where(used, offsets[grp] + k * cap, 0)
    last_grp = grp[jnp.maximum(ends[-1] - 1, 0)]
    grp = jnp.where(used, grp, last_grp)
    tail = jnp.stack([jnp.sum(padded) // FFN_SUB, ends[-1]])
    return grp.astype(jnp.int32), row0.astype(jnp.int32), nsub.astype(jnp.int32), tail.astype(jnp.int32)


def _rope(x, cos, sin):
    return x * cos + pltpu.roll(x, LANES // 2, 1) * sin


def _dot_split(a, sel):
    hi = a.astype(bf16)
    lo = (a - hi.astype(f32)).astype(bf16)
    return _dot(hi, sel) + _dot(lo, sel)


def _mla_kvq_kernel(x0_ref, y_ref, cos_ref, sin_ref, gkv_ref, wkva_ref, gkva_ref, wkvb_ref, gkn_ref, gkr_ref,
                    gmix_ref, win_ref, gqa_ref, wqb_ref, gqn_ref, gqr_ref, sel_ref,
                    x1_ref, k_ref, v_ref, q_ref, um_ref):
    x1 = x0_ref[...] + y_ref[...]
    x1_ref[...] = x1
    cos = cos_ref[...]
    sin = sin_ref[...]

    kva = _dot(_rms(x1, gkv_ref[...]).astype(bf16), wkva_ref[...])
    c_kv = _rms(kva[:, :KV_LORA], gkva_ref[...]).astype(bf16)
    kv = _dot(c_kv, wkvb_ref[...])
    tm = x1.shape[0]
    kpe = kva[:, KV_LORA:]
    ss_pe = jnp.sum(kpe * kpe, axis=-1, keepdims=True)
    sc = lax.rsqrt((_dot_split(kv[:, :MIX_WIDTH] * kv[:, :MIX_WIDTH], sel_ref[...]) + ss_pe) * (1.0 / QK_HEAD) + EPS)
    k_rot = _rope(kpe * gkr_ref[...], cos, sin)
    for h in range(N_HEADS):
        sch = jnp.broadcast_to(sc[:, h:h + 1], (tm, LANES))
        k_ref[:, h * HEAD_PAD:h * HEAD_PAD + LANES] = (
            kv[:, h * HEAD_DIM:(h + 1) * HEAD_DIM] * sch * gkn_ref[...]).astype(bf16)
        k_ref[:, h * HEAD_PAD + LANES:(h + 1) * HEAD_PAD] = (k_rot * sch).astype(bf16)
    v_ref[...] = kv[:, MIX_WIDTH:].astype(bf16)

    u = _dot(_rms(x1, gmix_ref[...]).astype(bf16), win_ref[...])
    um_ref[...] = u[:, Q_LORA:].astype(bf16)
    c_q = _rms(u[:, :Q_LORA], gqa_ref[...]).astype(bf16)
    q = _dot(c_q, wqb_ref[...])
    q2 = jnp.concatenate([q[:, h * HEAD_PAD:h * HEAD_PAD + LANES] * q[:, h * HEAD_PAD:h * HEAD_PAD + LANES]
                          + q[:, h * HEAD_PAD + LANES:(h + 1) * HEAD_PAD] * q[:, h * HEAD_PAD + LANES:(h + 1) * HEAD_PAD]
                          for h in range(N_HEADS)], axis=-1)
    sc = lax.rsqrt(_dot_split(q2, sel_ref[...]) * (1.0 / QK_HEAD) + EPS)
    for h in range(N_HEADS):
        sch = jnp.broadcast_to(sc[:, h:h + 1], (tm, LANES))
        qn = q[:, h * HEAD_PAD:h * HEAD_PAD + LANES]
        qr = q[:, h * HEAD_PAD + LANES:(h + 1) * HEAD_PAD]
        q_ref[:, h * HEAD_PAD:h * HEAD_PAD + LANES] = (qn * sch * gqn_ref[...]).astype(bf16)
        q_ref[:, h * HEAD_PAD + LANES:(h + 1) * HEAD_PAD] = (_rope(qr * gqr_ref[...], cos, sin) * sch).astype(bf16)


def mla_kvq(x0, y, cos, sin, g_kv, w_kva, g_kva, w_kvb, gk_n, gk_r, g_mix, w_in, g_qa, w_qb, gq_n, gq_r, tm=256):
    row = lambda w: pl.BlockSpec((tm, w), lambda i: (i, 0))
    full = lambda a: pl.BlockSpec(a.shape, lambda i: (0, 0))
    head = jnp.arange(MIX_WIDTH, dtype=jnp.int32) // HEAD_DIM
    sel = (head[:, None] == jnp.arange(LANES, dtype=jnp.int32)[None, :]).astype(bf16)
    consts = (g_kv, w_kva, g_kva, w_kvb, gk_n, gk_r, g_mix, w_in, g_qa, w_qb, gq_n, gq_r, sel)
    return pl.pallas_call(
        _mla_kvq_kernel,
        grid=(N_TOK // tm,),
        in_specs=[row(D_MODEL), row(D_MODEL), row(LANES), row(LANES)] + [full(a) for a in consts],
        out_specs=[row(D_MODEL), row(N_HEADS * HEAD_PAD), row(MIX_WIDTH), row(N_HEADS * HEAD_PAD), row(MEM_WIDTH)],
        out_shape=[jax.ShapeDtypeStruct((N_TOK, D_MODEL), f32),
                   jax.ShapeDtypeStruct((N_TOK, N_HEADS * HEAD_PAD), bf16),
                   jax.ShapeDtypeStruct((N_TOK, MIX_WIDTH), bf16),
                   jax.ShapeDtypeStruct((N_TOK, N_HEADS * HEAD_PAD), bf16),
                   jax.ShapeDtypeStruct((N_TOK, MEM_WIDTH), bf16)],
        compiler_params=_cparams(("arbitrary",)),
        name="mla_kvq",
    )(x0, y, cos, sin, *consts)


def _flash_kernel(q_ref, k_ref, v_ref, o_ref, m_s, l_s, acc_s, *, tq, heads):
    qi = pl.program_id(2)
    c_exp = (QK_HEAD ** -0.5) * 1.4426950408889634
    m_s[...] = jnp.full(m_s.shape, -jnp.inf, f32)
    l_s[...] = jnp.zeros(l_s.shape, f32)
    acc_s[...] = jnp.zeros(acc_s.shape, f32)
    n_blk = tq // LANES

    def chunk(kj, masked):
        r = pl.multiple_of(kj * tq, tq)

        def scores(h):
            q = q_ref[:, h * HEAD_PAD:(h + 1) * HEAD_PAD]
            s = _dot_nt(q, k_ref[pl.ds(r, tq), h * HEAD_PAD:(h + 1) * HEAD_PAD])
            if masked:
                qpos = lax.broadcasted_iota(jnp.int32, (tq, tq), 0)
                kpos = lax.broadcasted_iota(jnp.int32, (tq, tq), 1)
                s = jnp.where(kpos <= qpos, s, -jnp.inf)
            return s

        def update(h, s):
            blocks = [s[:, b * LANES:(b + 1) * LANES] for b in range(n_blk)]
            lane_max = functools.reduce(jnp.maximum, blocks)
            m_old = m_s[h]
            m_new = jnp.maximum(m_old, jnp.max(lane_max, axis=-1, keepdims=True))
            alpha = jnp.exp2((m_old - m_new) * c_exp)
            p = [jnp.exp2((blk - m_new) * c_exp) for blk in blocks]
            l_s[h] = alpha * l_s[h] + functools.reduce(jnp.add, p)
            pv = _dot(jnp.concatenate(p, axis=-1).astype(bf16),
                      v_ref[pl.ds(r, tq), h * HEAD_DIM:(h + 1) * HEAD_DIM])
            acc_s[h] = alpha * acc_s[h] + pv
            m_s[h] = m_new

        s_next = scores(0)
        for h in range(heads):
            s_cur = s_next
            if h + 1 < heads:
                s_next = scores(h + 1)
            update(h, s_cur)

    def body(kj, c):
        chunk(kj, False)
        return c

    lax.fori_loop(0, qi, body, 0)
    chunk(qi, True)
    for h in range(heads):
        l = jnp.sum(l_s[h], axis=-1, keepdims=True)
        o_ref[:, h * HEAD_DIM:(h + 1) * HEAD_DIM] = (acc_s[h] / l).astype(o_ref.dtype)


def flash_attention(q, k, v, tq=512, heads=6):
    per_seq = SEQ // tq
    return pl.pallas_call(
        functools.partial(_flash_kernel, tq=tq, heads=heads),
        grid=(BATCH, N_HEADS // heads, per_seq),
        in_specs=[pl.BlockSpec((tq, heads * HEAD_PAD), lambda b, h, i: (b * per_seq + i, h)),
                  pl.BlockSpec((SEQ, heads * HEAD_PAD), lambda b, h, i: (b, h)),
                  pl.BlockSpec((SEQ, heads * HEAD_DIM), lambda b, h, i: (b, h))],
        out_specs=pl.BlockSpec((tq, heads * HEAD_DIM), lambda b, h, i: (b * per_seq + i, h)),
        out_shape=jax.ShapeDtypeStruct((N_TOK, MIX_WIDTH), bf16),
        scratch_shapes=[pltpu.VMEM((heads, tq, LANES), f32),
                        pltpu.VMEM((heads, tq, LANES), f32),
                        pltpu.VMEM((heads, tq, HEAD_DIM), f32)],
        compiler_params=_cparams(("arbitrary", "arbitrary", "arbitrary")),
        name="flash_attention",
    )(q, k, v)


def _router_kernel(x_ref, g_ref, wr_ref, mi_ref, mf_ref, cnt_ref, carry, *, tm):
    i = pl.program_id(0)

    @pl.when(i == 0)
    def _():
        carry[...] = jnp.zeros_like(carry)

    h = _rms(x_ref[...], g_ref[...])
    w = wr_ref[...]
    h_hi, w_hi = h.astype(bf16), w.astype(bf16)
    h_lo, w_lo = (h - h_hi.astype(f32)).astype(bf16), (w - w_hi.astype(f32)).astype(bf16)
    logits = _dot(h_hi, w_hi) + (_dot(h_hi, w_lo) + _dot(h_lo, w_hi))
    lane = lax.broadcasted_iota(jnp.int32, (tm, LANES), 1)
    logits = jnp.where(lane < N_EXPERTS, logits, -jnp.inf)
    lane_f = lane.astype(f32)
    v1 = jnp.max(logits, axis=-1, keepdims=True)
    e1 = jnp.min(jnp.where(logits == v1, lane_f, float(LANES)), axis=-1, keepdims=True).astype(jnp.int32)
    rest = jnp.where(lane == e1, -jnp.inf, logits)
    v2 = jnp.max(rest, axis=-1, keepdims=True)
    e2 = jnp.min(jnp.where(rest == v2, lane_f, float(LANES)), axis=-1, keepdims=True).astype(jnp.int32)
    t = jnp.exp(v2 - v1)
    w1 = 1.0 / (1.0 + t)
    w2 = t / (1.0 + t)

    hot = jnp.where((lane == e1) | (lane == e2), 1.0, 0.0)
    r_io = lax.broadcasted_iota(jnp.int32, (tm, tm), 0)
    c_io = lax.broadcasted_iota(jnp.int32, (tm, tm), 1)
    below = jnp.where(c_io < r_io, 1.0, 0.0).astype(bf16)
    rank = _dot(below, hot.astype(bf16)) + carry[...]
    carry[...] += jnp.sum(hot, axis=0, keepdims=True)
    rank1 = jnp.sum(jnp.where(lane == e1, rank, 0.0), axis=-1, keepdims=True).astype(jnp.int32)
    rank2 = jnp.sum(jnp.where(lane == e2, rank, 0.0), axis=-1, keepdims=True).astype(jnp.int32)

    mi_ref[...] = jnp.where(lane == 0, e1, jnp.where(lane == 1, e2, jnp.where(lane == 2, rank1,
                            jnp.where(lane == 3, rank2, 0))))
    mf_ref[...] = jnp.where(lane == 0, w1, jnp.where(lane == 1, w2, 0.0))
    cnt_ref[...] = carry[...].astype(jnp.int32)


def router(x, g, w_router_pad, tm=512):
    return pl.pallas_call(
        functools.partial(_router_kernel, tm=tm),
        grid=(N_TOK // tm,),
        in_specs=[pl.BlockSpec((tm, D_MODEL), lambda i: (i, 0)),
                  pl.BlockSpec((1, D_MODEL), lambda i: (0, 0)),
                  pl.BlockSpec((D_MODEL, LANES), lambda i: (0, 0))],
        out_specs=[pl.BlockSpec((tm, LANES), lambda i: (i, 0)),
                   pl.BlockSpec((tm, LANES), lambda i: (i, 0)),
                   pl.BlockSpec((1, LANES), lambda i: (0, 0))],
        out_shape=[jax.ShapeDtypeStruct((N_TOK, LANES), jnp.int32),
                   jax.ShapeDtypeStruct((N_TOK, LANES), f32),
                   jax.ShapeDtypeStruct((1, LANES), jnp.int32)],
        scratch_shapes=[pltpu.VMEM((1, LANES), f32)],
        compiler_params=_cparams(("arbitrary",)),
        name="router",
    )(x, g, w_router_pad)


def _dispatch_kernel(p1_ref, p2_ref, pad0_ref, padn_ref, hp_ref, xs_out, zrow, sem, *, tm):
    base = pl.program_id(0) * tm

    @pl.when(pl.program_id(0) == 0)
    def _():
        zrow[...] = jnp.zeros(zrow.shape, zrow.dtype)

        def pad_copy(e, i):
            return pltpu.make_async_copy(zrow.at[pl.ds(0, 1)], xs_out.at[pl.ds(pad0_ref[e] + i, 1)], sem)

        def tail_copy(j):
            r = pl.multiple_of(j * FFN_SUB, FFN_SUB)
            return pltpu.make_async_copy(zrow, xs_out.at[pl.ds(r, FFN_SUB)], sem)

        tail0 = lax.shift_right_logical(pad0_ref[N_EXPERTS], FFN_SUB.bit_length() - 1)
        n_tail = xs_out.shape[0] // FFN_SUB
        for e in range(N_EXPERTS):
            lax.fori_loop(0, padn_ref[e], lambda i, c: (pad_copy(e, i).start(), c)[1], 0)
        lax.fori_loop(tail0, n_tail, lambda j, c: (tail_copy(j).start(), c)[1], 0)
        for e in range(N_EXPERTS):
            lax.fori_loop(0, padn_ref[e], lambda i, c: (pad_copy(e, i).wait(), c)[1], 0)
        lax.fori_loop(tail0, n_tail, lambda j, c: (tail_copy(j).wait(), c)[1], 0)

    def copy(i, pos_ref):
        return pltpu.make_async_copy(hp_ref.at[pl.ds(i, 1)], xs_out.at[pl.ds(pos_ref[base + i], 1)], sem)

    def start(i, c):
        copy(i, p1_ref).start()
        copy(i, p2_ref).start()
        return c

    def wait(i, c):
        copy(i, p1_ref).wait()
        copy(i, p2_ref).wait()
        return c

    lax.fori_loop(0, tm, start, 0, unroll=8)
    lax.fori_loop(0, tm, wait, 0, unroll=8)


def dispatch(pos1, pos2, pad_start, pad_len, hp, tm=512):
    grid_spec = pltpu.PrefetchScalarGridSpec(
        num_scalar_prefetch=4,
        grid=(N_TOK // tm,),
        in_specs=[pl.BlockSpec((tm, HALF_D), lambda i, *_: (i, 0))],
        out_specs=pl.BlockSpec(memory_space=pl.ANY),
        scratch_shapes=[pltpu.VMEM((FFN_SUB, HALF_D), jnp.uint32), pltpu.SemaphoreType.DMA(())],
    )
    return pl.pallas_call(
        functools.partial(_dispatch_kernel, tm=tm),
        grid_spec=grid_spec,
        out_shape=jax.ShapeDtypeStruct((MOE_ROWS, HALF_D), jnp.uint32),
        compiler_params=_cparams(("arbitrary",)),
        name="dispatch",
    )(pos1, pos2, pad_start, pad_len, hp)


def _combine_kernel(p1_ref, p2_ref, x_ref, mf_ref, y_hbm, o_ref, buf, sem, *, tm):
    i = pl.program_id(0)

    def copy(t, r, k, pos_ref):
        slot = t % 2
        return pltpu.make_async_copy(y_hbm.at[pl.ds(pos_ref[t * tm + r], 1)],
                                     buf.at[slot, k, pl.ds(r, 1)], sem.at[slot])

    def for_rows(fn):
        def body(r, c):
            fn(r)
            return c
        lax.fori_loop(0, tm, body, 0, unroll=8)

    def start_tile(t):
        for_rows(lambda r: (copy(t, r, 0, p1_ref).start(), copy(t, r, 1, p2_ref).start()))

    @pl.when(i == 0)
    def _():
        start_tile(0)

    @pl.when(i + 1 < pl.num_programs(0))
    def _():
        start_tile(i + 1)

    for_rows(lambda r: (copy(i, r, 0, p1_ref).wait(), copy(i, r, 1, p2_ref).wait()))
    w = mf_ref[...]
    slot = i % 2
    o_ref[...] = x_ref[...] + w[:, 0:1] * buf[slot, 0] + w[:, 1:2] * buf[slot, 1]


def combine(pos1, pos2, x, mf, y, tm=256):
    grid_spec = pltpu.PrefetchScalarGridSpec(
        num_scalar_prefetch=2,
        grid=(N_TOK // tm,),
        in_specs=[pl.BlockSpec((tm, D_MODEL), lambda i, p1, p2: (i, 0)),
                  pl.BlockSpec((tm, LANES), lambda i, p1, p2: (i, 0)),
                  pl.BlockSpec(memory_space=pl.ANY)],
        out_specs=pl.BlockSpec((tm, D_MODEL), lambda i, p1, p2: (i, 0)),
        scratch_shapes=[pltpu.VMEM((2, TOP_K, tm, D_MODEL), f32), pltpu.SemaphoreType.DMA((2,))],
    )
    return pl.pallas_call(
        functools.partial(_combine_kernel, tm=tm),
        grid_spec=grid_spec,
        out_shape=jax.ShapeDtypeStruct((N_TOK, D_MODEL), f32),
        compiler_params=_cparams(("arbitrary",)),
        name="combine",
    )(pos1, pos2, x, mf, y)


def _rope_lanes(v):
    z = jnp.zeros(v.shape[:-1] + (QK_ROPE // 2,), v.dtype)
    return jnp.concatenate([v[..., :QK_ROPE // 2], z, v[..., QK_ROPE // 2:], z], axis=-1)


def kernel(x, mem, positions, g_mix, g_ffn, g_mem, w_mem_kv, g_mq, g_mk, w_out, a_w_in, a_conv_w, b_w_in,
           b_g_q_a, b_w_q_b, b_g_qn, g_kv, w_kv_a, g_kv_a, w_kv_b, g_kn, ffn_w_gate, ffn_w_up, ffn_w_down,
           moe_w_router, moe_w_gate, moe_w_up, moe_w_down):
    row = lambda v: v.reshape(1, -1).astype(f32)
    x0 = x.reshape(N_TOK, D_MODEL)
    mem2 = mem.reshape(BATCH * N_MEM, D_MODEL)

    w_kva = jnp.concatenate([w_kv_a[:, :KV_LORA], _rope_lanes(w_kv_a[:, KV_LORA:])], axis=1).astype(bf16)
    kvb = w_kv_b.reshape(KV_LORA, N_HEADS, QK_NOPE + HEAD_DIM)
    w_kvb = jnp.concatenate([kvb[:, :, :QK_NOPE].reshape(KV_LORA, MIX_WIDTH),
                             kvb[:, :, QK_NOPE:].reshape(KV_LORA, MIX_WIDTH)], axis=1).astype(bf16)
    qb = b_w_q_b[0].reshape(Q_LORA, N_HEADS, QK_HEAD)
    w_qb = jnp.concatenate([qb[:, :, :QK_NOPE], _rope_lanes(qb[:, :, QK_NOPE:])], axis=-1)
    w_qb = w_qb.reshape(Q_LORA, N_HEADS * HEAD_PAD).astype(bf16)
    gk_n, gk_r = row(g_kn[:QK_NOPE]), row(_rope_lanes(g_kn[QK_NOPE:]))
    gq_n, gq_r = row(b_g_qn[0, :QK_NOPE]), row(_rope_lanes(b_g_qn[0, QK_NOPE:]))
    w_router_pad = jnp.pad(moe_w_router[0].astype(f32), ((0, 0), (0, LANES - N_EXPERTS)))

    inv_freq = 1.0 / (ROPE_THETA ** (jnp.arange(0, QK_ROPE, 2, dtype=f32) / QK_ROPE))
    ang = positions.reshape(N_TOK, 1).astype(f32) * inv_freq[None, :]
    zeros = jnp.zeros_like(ang)
    cos_t = jnp.concatenate([jnp.cos(ang), zeros, jnp.cos(ang), zeros], axis=-1)
    sin_t = jnp.concatenate([-jnp.sin(ang), zeros, jnp.sin(ang), zeros], axis=-1)

    u0 = norm_matmul(x0, row(g_mix[0]), a_w_in[0].astype(bf16), tm=1024, tn=1024)
    kvm0 = norm_matmul(mem2, row(g_mem[0]), w_mem_kv[0].astype(bf16), tm=BATCH * N_MEM, tn=2 * MEM_WIDTH)
    xa, hp0 = conv_tail(u0, a_conv_w[0].astype(f32), kvm0, row(g_mq[0]), row(g_mk[0]),
                        w_out[0].astype(bf16), x0, row(g_ffn[0]))
    dense_tbl = _super_tiles(jnp.array([N_TOK], jnp.int32), jnp.array([0], jnp.int32),
                             -(-N_TOK // (FFN_SUB * FFN_SUPER)))
    y0 = swiglu_ffn(hp0, ffn_w_gate, ffn_w_up, ffn_w_down, *dense_tbl)

    x1, k, v, q, um1 = mla_kvq(xa, y0, cos_t, sin_t, row(g_kv), w_kva, row(g_kv_a), w_kvb, gk_n, gk_r,
                               row(g_mix[1]), b_w_in[0].astype(bf16), row(b_g_q_a[0]), w_qb, gq_n, gq_r)
    mix1 = flash_attention(q, k, v)
    kvm1 = norm_matmul(mem2, row(g_mem[1]), w_mem_kv[1].astype(bf16), tm=BATCH * N_MEM, tn=2 * MEM_WIDTH)
    x2, hp1 = attn_tail(mix1, um1, 0, kvm1, row(g_mq[1]), row(g_mk[1]), w_out[1].astype(bf16), x1, row(g_ffn[1]))

    mi, mf, cnt = router(x2, row(g_ffn[1]), w_router_pad)
    counts = cnt[0, :N_EXPERTS]
    padded = (counts + FFN_SUB - 1) // FFN_SUB * FFN_SUB
    offsets = jnp.cumsum(padded) - padded
    pos1 = offsets[mi[:, 0]] + mi[:, 2]
    pos2 = offsets[mi[:, 1]] + mi[:, 3]
    pad_start = jnp.concatenate([offsets + counts, jnp.sum(padded, keepdims=True)])
    xs = dispatch(pos1, pos2, pad_start, padded - counts, hp1)
    y1 = swiglu_ffn(xs, moe_w_gate[0], moe_w_up[0], moe_w_down[0], *_super_tiles(padded, offsets, MOE_SUPERS))
    out = combine(pos1, pos2, x2, mf, y1)
    return out.reshape(BATCH, SEQ, D_MODEL)
```

```python
import functools

import jax
import jax.numpy as jnp
from jax import lax
from jax.experimental import pallas as pl
from jax.experimental.pallas import tpu as pltpu

D_MODEL = 2048
BATCH = 2
SEQ = 4096
N_TOK = BATCH * SEQ
N_HEADS = 12
HEAD_DIM = 128
MIX_WIDTH = N_HEADS * HEAD_DIM
MEM_HEADS = 4
MEM_WIDTH = MEM_HEADS * HEAD_DIM
N_MEM = 256
CONV_WIDTH = 3
Q_LORA = 512
KV_LORA = 256
QK_NOPE = 128
QK_ROPE = 64
QK_HEAD = QK_NOPE + QK_ROPE
ROPE_THETA = 10000.0
D_FF = 7168
N_EXPERTS = 8
TOP_K = 2
EPS = 1e-6

LANES = 128
HEAD_PAD = 2 * LANES
HALF_D = D_MODEL // 2
VMEM_LIMIT = 56 * 1024 * 1024

FFN_SUB = 256
FFN_SUPER = 9
FFN_TF = 256
MOE_ROWS = N_TOK * TOP_K + N_EXPERTS * FFN_SUB
MOE_SUPERS = N_EXPERTS + -(-MOE_ROWS // (FFN_SUB * FFN_SUPER))

f32 = jnp.float32
bf16 = jnp.bfloat16


def _cparams(sem):
    return pltpu.CompilerParams(dimension_semantics=sem, vmem_limit_bytes=VMEM_LIMIT)


def _rms(x, g, n=None):
    n = x.shape[-1] if n is None else n
    ms = jnp.sum(x * x, axis=-1, keepdims=True) * (1.0 / n)
    return x * lax.rsqrt(ms + EPS) * g


def _pack_halves(h):
    hb = h.astype(bf16).astype(f32)
    return pltpu.pack_elementwise([hb[:, :HALF_D], hb[:, HALF_D:]], packed_dtype=bf16)


def _unpack_halves(p):
    lo = pltpu.unpack_elementwise(p, index=0, packed_dtype=bf16, unpacked_dtype=f32)
    hi = pltpu.unpack_elementwise(p, index=1, packed_dtype=bf16, unpacked_dtype=f32)
    return lo.astype(bf16), hi.astype(bf16)


def _dot(a, b):
    return jnp.dot(a, b, preferred_element_type=f32)


def _dot_nt(a, b):
    return lax.dot_general(a, b, (((1,), (1,)), ((), ())), preferred_element_type=f32)


def _norm_matmul_kernel(x_ref, g_ref, w_ref, o_ref, h_ref):
    @pl.when(pl.program_id(1) == 0)
    def _():
        h_ref[...] = _rms(x_ref[...], g_ref[...]).astype(bf16)

    o_ref[...] = _dot(h_ref[...], w_ref[...]).astype(o_ref.dtype)


def norm_matmul(x, g, w, tm, tn, out_dtype=bf16):
    m, d = x.shape
    n = w.shape[1]
    return pl.pallas_call(
        _norm_matmul_kernel,
        grid=(m // tm, n // tn),
        in_specs=[pl.BlockSpec((tm, d), lambda i, j: (i, 0)),
                  pl.BlockSpec((1, d), lambda i, j: (0, 0)),
                  pl.BlockSpec((d, tn), lambda i, j: (0, j))],
        out_specs=pl.BlockSpec((tm, tn), lambda i, j: (i, j)),
        out_shape=jax.ShapeDtypeStruct((m, n), out_dtype),
        scratch_shapes=[pltpu.VMEM((tm, d), bf16)],
        compiler_params=_cparams(("arbitrary", "arbitrary")),
        name="norm_matmul",
    )(x, g, w)


def _mem_attention(um_ref, kvm_ref, gq_ref, gk_ref, cat_ref):
    scale = HEAD_DIM ** -0.5
    for h in range(MEM_HEADS):
        cs = slice(h * HEAD_DIM, (h + 1) * HEAD_DIM)
        q = _rms(um_ref[:, cs].astype(f32), gq_ref[...]).astype(bf16)
        k = _rms(kvm_ref[:, cs].astype(f32), gk_ref[...]).astype(bf16)
        v = kvm_ref[:, MEM_WIDTH + h * HEAD_DIM:MEM_WIDTH + (h + 1) * HEAD_DIM]
        s = _dot_nt(q, k) * scale
        p = jnp.exp(s - jnp.max(s, axis=-1, keepdims=True))
        o = _dot(p.astype(bf16), v) / jnp.sum(p, axis=-1, keepdims=True)
        cat_ref[:, MIX_WIDTH + h * HEAD_DIM:MIX_WIDTH + (h + 1) * HEAD_DIM] = o.astype(bf16)


def _tail_epilogue(cat_ref, wout_ref, x_ref, gffn_ref, xo_ref, hp_ref):
    xn = x_ref[...] + _dot(cat_ref[...], wout_ref[...])
    xo_ref[...] = xn
    hp_ref[...] = _pack_halves(_rms(xn, gffn_ref[...]))


def _conv_tail_kernel(xin_ref, gb_ref, gc_ref, um_ref, hx_ref, hg_ref, cw_ref, kvm_ref, gq_ref, gk_ref,
                      wout_ref, x_ref, gffn_ref, xo_ref, hp_ref, cat_ref, *, tm):
    i = pl.program_id(0)
    first = (i * tm) % SEQ == 0
    rows = lax.broadcasted_iota(jnp.int32, (tm, 1), 0)
    cc = 512
    for c in range(MIX_WIDTH // cc):
        cs = slice(c * cc, (c + 1) * cc)
        z = gc_ref[:, cs].astype(f32) * xin_ref[:, cs].astype(f32)
        zh = hg_ref[:, cs].astype(f32) * hx_ref[:, cs].astype(f32)
        zh = jnp.where(first, 0.0, zh)
        p1 = zh[15:16, :]
        p2 = zh[14:15, :]
        z1 = jnp.where(rows == 0, p1, pltpu.roll(z, 1, 0))
        z2 = jnp.where(rows == 0, p2, jnp.where(rows == 1, p1, pltpu.roll(z, 2, 0)))
        y = cw_ref[0:1, cs] * z2 + cw_ref[1:2, cs] * z1 + cw_ref[2:3, cs] * z
        cat_ref[:, cs] = (gb_ref[:, cs].astype(f32) * y).astype(bf16)
    _mem_attention(um_ref, kvm_ref, gq_ref, gk_ref, cat_ref)
    _tail_epilogue(cat_ref, wout_ref, x_ref, gffn_ref, xo_ref, hp_ref)


def _attn_tail_kernel(mix_ref, um_ref, kvm_ref, gq_ref, gk_ref, wout_ref, x_ref, gffn_ref,
                      xo_ref, hp_ref, cat_ref):
    cat_ref[:, :MIX_WIDTH] = mix_ref[...]
    _mem_attention(um_ref, kvm_ref, gq_ref, gk_ref, cat_ref)
    _tail_epilogue(cat_ref, wout_ref, x_ref, gffn_ref, xo_ref, hp_ref)


def _tail_common_specs(tm):
    per_seq = SEQ // tm
    return [pl.BlockSpec((N_MEM, 2 * MEM_WIDTH), lambda i: (i // per_seq, 0)),
            pl.BlockSpec((1, HEAD_DIM), lambda i: (0, 0)),
            pl.BlockSpec((1, HEAD_DIM), lambda i: (0, 0)),
            pl.BlockSpec((D_MODEL, D_MODEL), lambda i: (0, 0), pipeline_mode=pl.Buffered(1)),
            pl.BlockSpec((tm, D_MODEL), lambda i: (i, 0)),
            pl.BlockSpec((1, D_MODEL), lambda i: (0, 0))]


def _tail_outs(tm):
    return dict(
        out_specs=[pl.BlockSpec((tm, D_MODEL), lambda i: (i, 0)),
                   pl.BlockSpec((tm, HALF_D), lambda i: (i, 0))],
        out_shape=[jax.ShapeDtypeStruct((N_TOK, D_MODEL), f32),
                   jax.ShapeDtypeStruct((N_TOK, HALF_D), jnp.uint32)],
        scratch_shapes=[pltpu.VMEM((tm, D_MODEL), bf16)],
        compiler_params=_cparams(("arbitrary",)),
    )


def conv_tail(u, conv_w, kvm, g_mq, g_mk, w_out, x, g_ffn, tm=512):
    halo = 16
    um_blk = 3 * MIX_WIDTH // MEM_WIDTH
    prev = lambda i: jnp.maximum(i * (tm // halo) - 1, 0)
    in_specs = [pl.BlockSpec((tm, MIX_WIDTH), lambda i: (i, 0)),
                pl.BlockSpec((tm, MIX_WIDTH), lambda i: (i, 1)),
                pl.BlockSpec((tm, MIX_WIDTH), lambda i: (i, 2)),
                pl.BlockSpec((tm, MEM_WIDTH), lambda i: (i, um_blk)),
                pl.BlockSpec((halo, MIX_WIDTH), lambda i: (prev(i), 0)),
                pl.BlockSpec((halo, MIX_WIDTH), lambda i: (prev(i), 2)),
                pl.BlockSpec((CONV_WIDTH, MIX_WIDTH), lambda i: (0, 0))] + _tail_common_specs(tm)
    return pl.pallas_call(
        functools.partial(_conv_tail_kernel, tm=tm),
        grid=(N_TOK // tm,), in_specs=in_specs, name="conv_tail", **_tail_outs(tm),
    )(u, u, u, u, u, u, conv_w, kvm, g_mq, g_mk, w_out, x, g_ffn)


def attn_tail(mix, um_src, um_blk, kvm, g_mq, g_mk, w_out, x, g_ffn, tm=512):
    in_specs = [pl.BlockSpec((tm, MIX_WIDTH), lambda i: (i, 0)),
                pl.BlockSpec((tm, MEM_WIDTH), lambda i: (i, um_blk))] + _tail_common_specs(tm)
    return pl.pallas_call(
        _attn_tail_kernel,
        grid=(N_TOK // tm,), in_specs=in_specs, name="attn_tail", **_tail_outs(tm),
    )(mix, um_src, kvm, g_mq, g_mk, w_out, x, g_ffn)


def _ffn_kernel(exp_ref, row0_ref, nsub_ref, tail_ref, xs_hbm, wg_ref, wu_ref, wd_ref, y_hbm,
                x_s, acc, wg_s, wu_s, wd_s, sem):
    s = pl.program_id(0)
    f = pl.program_id(1)
    nsub = nsub_ref[s]
    row0 = row0_ref[s]
    last_f = f == pl.num_programs(1) - 1
    last_step = (s == pl.num_programs(0) - 1) & last_f

    def x_copy(j):
        r = pl.multiple_of(j * FFN_SUB, FFN_SUB)
        g = pl.multiple_of(row0 + r, FFN_SUB)
        return pltpu.make_async_copy(xs_hbm.at[pl.ds(g, FFN_SUB)], x_s.at[pl.ds(r, FFN_SUB)], sem)

    def y_copy(r, rows):
        g = pl.multiple_of(row0 + r, FFN_SUB)
        return pltpu.make_async_copy(acc.at[pl.ds(r, rows)], y_hbm.at[pl.ds(g, rows)], sem)

    def for_subtiles(fn):
        def body(j, c):
            fn(j)
            return c
        lax.fori_loop(0, nsub, body, 0)

    @pl.when(f == 0)
    def _():
        for_subtiles(lambda j: x_copy(j).start())

        def zero(j):
            r = pl.multiple_of(j * FFN_SUB, FFN_SUB)
            acc[pl.ds(r, FFN_SUB), :] = jnp.zeros((FFN_SUB, D_MODEL), f32)
        for_subtiles(zero)
        for_subtiles(lambda j: x_copy(j).wait())

    def swiglu_rows(r, rows, wg, wu, wd):
        lo, hi = _unpack_halves(x_s[pl.ds(r, rows), :])
        g = _dot(lo, wg[:HALF_D, :]) + _dot(hi, wg[HALF_D:, :])
        u = _dot(lo, wu[:HALF_D, :]) + _dot(hi, wu[HALF_D:, :])
        a = (g * jax.nn.sigmoid(g) * u).astype(bf16)
        acc[pl.ds(r, rows), :] += _dot(a, wd[...])

        @pl.when(last_f)
        def _():
            y_copy(r, rows).start()

    rem = nsub - 1
    n_quad = lax.shift_right_logical(rem, 2)
    quad_row = lambda j: pl.multiple_of(FFN_SUB + j * 4 * FFN_SUB, FFN_SUB)
    pair_row = pl.multiple_of(FFN_SUB + n_quad * 4 * FFN_SUB, FFN_SUB)
    last_row = pl.multiple_of(rem * FFN_SUB, FFN_SUB)

    def for_groups(first, fn):
        first()

        def body(j, c):
            fn(quad_row(j), 4 * FFN_SUB)
            return c
        lax.fori_loop(0, n_quad, body, 0)

        @pl.when((rem & 2) != 0)
        def _():
            fn(pair_row, 2 * FFN_SUB)

        @pl.when((rem & 1) != 0)
        def _():
            fn(last_row, FFN_SUB)

    def first_group():
        wg = wg_ref[0].astype(bf16)
        wu = wu_ref[0].astype(bf16)
        wd = wd_ref[0].astype(bf16)
        wg_s[...] = wg
        wu_s[...] = wu
        wd_s[...] = wd
        swiglu_rows(0, FFN_SUB, wg, wu, wd)

    for_groups(first_group, lambda r, rows: swiglu_rows(r, rows, wg_s, wu_s, wd_s))

    @pl.when(last_f)
    def _():
        for_groups(lambda: y_copy(0, FFN_SUB).wait(), lambda r, rows: y_copy(r, rows).wait())

    @pl.when(last_step)
    def _():
        acc[pl.ds(0, FFN_SUB), :] = jnp.zeros((FFN_SUB, D_MODEL), f32)

        def tail_copy(j):
            g = pl.multiple_of(j * FFN_SUB, FFN_SUB)
            return pltpu.make_async_copy(acc.at[pl.ds(0, FFN_SUB)], y_hbm.at[pl.ds(g, FFN_SUB)], sem)

        n_tail = y_hbm.shape[0] // FFN_SUB
        lax.fori_loop(tail_ref[0], n_tail, lambda j, c: (tail_copy(j).start(), c)[1], 0)
        lax.fori_loop(tail_ref[0], n_tail, lambda j, c: (tail_copy(j).wait(), c)[1], 0)


def swiglu_ffn(xs, w_gate, w_up, w_down, st_expert, st_row0, st_nsub, st_tail):
    rows = xs.shape[0]
    nf = D_FF // FFN_TF

    grid_spec = pltpu.PrefetchScalarGridSpec(
        num_scalar_prefetch=4,
        grid=(st_tail[1], nf),
        in_specs=[pl.BlockSpec(memory_space=pl.ANY),
                  pl.BlockSpec((1, D_MODEL, FFN_TF), lambda s, f, e, r, n, t: (e[s], 0, f)),
                  pl.BlockSpec((1, D_MODEL, FFN_TF), lambda s, f, e, r, n, t: (e[s], 0, f)),
                  pl.BlockSpec((1, FFN_TF, D_MODEL), lambda s, f, e, r, n, t: (e[s], f, 0))],
        out_specs=pl.BlockSpec(memory_space=pl.ANY),
        scratch_shapes=[pltpu.VMEM((FFN_SUPER * FFN_SUB, HALF_D), jnp.uint32),
                        pltpu.VMEM((FFN_SUPER * FFN_SUB, D_MODEL), f32),
                        pltpu.VMEM((D_MODEL, FFN_TF), bf16),
                        pltpu.VMEM((D_MODEL, FFN_TF), bf16),
                        pltpu.VMEM((FFN_TF, D_MODEL), bf16),
                        pltpu.SemaphoreType.DMA(())],
    )
    return pl.pallas_call(
        _ffn_kernel,
        grid_spec=grid_spec,
        out_shape=jax.ShapeDtypeStruct((rows, D_MODEL), f32),
        compiler_params=_cparams(("arbitrary", "arbitrary")),
        name="swiglu_ffn",
    )(st_expert, st_row0, st_nsub, st_tail, xs, w_gate, w_up, w_down)


def _super_tiles(padded, offsets, n_super):
    cap = FFN_SUB * FFN_SUPER
    per_group = (padded + cap - 1) // cap
    ends = jnp.cumsum(per_group)
    t = jnp.arange(n_super, dtype=jnp.int32)
    grp = jnp.minimum(jnp.sum((t[:, None] >= ends[None, :]).astype(jnp.int32), axis=1), padded.shape[0] - 1)
    k = t - (ends[grp] - per_group[grp])
    used = t < ends[-1]
    nsub = jnp.where(used, jnp.minimum(FFN_SUPER, (padded[grp] - k * cap) // FFN_SUB), 0)
    row0 = jnp.where(used, offsets[grp] + k * cap, 0)
    last_grp = grp[jnp.maximum(ends[-1] - 1, 0)]
    grp = jnp.where(used, grp, last_grp)
    tail = jnp.stack([jnp.sum(padded) // FFN_SUB, ends[-1]])
    return grp.astype(jnp.int32), row0.astype(jnp.int32), nsub.astype(jnp.int32), tail.astype(jnp.int32)


def _rope(x, cos, sin):
    return x * cos + pltpu.roll(x, LANES // 2, 1) * sin


def _dot_split(a, sel):
    hi = a.astype(bf16)
    lo = (a - hi.astype(f32)).astype(bf16)
    return _dot(hi, sel) + _dot(lo, sel)


def _mla_kvq_kernel(x0_ref, y_ref, cos_ref, sin_ref, gkv_ref, wkva_ref, gkva_ref, wkvb_ref, gkn_ref, gkr_ref,
                    gmix_ref, win_ref, gqa_ref, wqb_ref, gqn_ref, gqr_ref, sel_ref,
                    x1_ref, k_ref, v_ref, q_ref, um_ref):
    x1 = x0_ref[...] + y_ref[...]
    x1_ref[...] = x1
    cos = cos_ref[...]
    sin = sin_ref[...]

    kva = _dot(_rms(x1, gkv_ref[...]).astype(bf16), wkva_ref[...])
    c_kv = _rms(kva[:, :KV_LORA], gkva_ref[...]).astype(bf16)
    kv = _dot(c_kv, wkvb_ref[...])
    tm = x1.shape[0]
    kpe = kva[:, KV_LORA:]
    ss_pe = jnp.sum(kpe * kpe, axis=-1, keepdims=True)
    sc = lax.rsqrt((_dot_split(kv[:, :MIX_WIDTH] * kv[:, :MIX_WIDTH], sel_ref[...]) + ss_pe) * (1.0 / QK_HEAD) + EPS)
    k_rot = _rope(kpe * gkr_ref[...], cos, sin)
    for h in range(N_HEADS):
        sch = jnp.broadcast_to(sc[:, h:h + 1], (tm, LANES))
        k_ref[:, h * HEAD_PAD:h * HEAD_PAD + LANES] = (
            kv[:, h * HEAD_DIM:(h + 1) * HEAD_DIM] * sch * gkn_ref[...]).astype(bf16)
        k_ref[:, h * HEAD_PAD + LANES:(h + 1) * HEAD_PAD] = (k_rot * sch).astype(bf16)
    v_ref[...] = kv[:, MIX_WIDTH:].astype(bf16)

    u = _dot(_rms(x1, gmix_ref[...]).astype(bf16), win_ref[...])
    um_ref[...] = u[:, Q_LORA:].astype(bf16)
    c_q = _rms(u[:, :Q_LORA], gqa_ref[...]).astype(bf16)
    q = _dot(c_q, wqb_ref[...])
    q2 = jnp.concatenate([q[:, h * HEAD_PAD:h * HEAD_PAD + LANES] * q[:, h * HEAD_PAD:h * HEAD_PAD + LANES]
                          + q[:, h * HEAD_PAD + LANES:(h + 1) * HEAD_PAD] * q[:, h * HEAD_PAD + LANES:(h + 1) * HEAD_PAD]
                          for h in range(N_HEADS)], axis=-1)
    sc = lax.rsqrt(_dot_split(q2, sel_ref[...]) * (1.0 / QK_HEAD) + EPS)
    for h in range(N_HEADS):
        sch = jnp.broadcast_to(sc[:, h:h + 1], (tm, LANES))
        qn = q[:, h * HEAD_PAD:h * HEAD_PAD + LANES]
        qr = q[:, h * HEAD_PAD + LANES:(h + 1) * HEAD_PAD]
        q_ref[:, h * HEAD_PAD:h * HEAD_PAD + LANES] = (qn * sch * gqn_ref[...]).astype(bf16)
        q_ref[:, h * HEAD_PAD + LANES:(h + 1) * HEAD_PAD] = (_rope(qr * gqr_ref[...], cos, sin) * sch).astype(bf16)


def mla_kvq(x0, y, cos, sin, g_kv, w_kva, g_kva, w_kvb, gk_n, gk_r, g_mix, w_in, g_qa, w_qb, gq_n, gq_r, tm=256):
    row = lambda w: pl.BlockSpec((tm, w), lambda i: (i, 0))
    full = lambda a: pl.BlockSpec(a.shape, lambda i: (0, 0))
    head = jnp.arange(MIX_WIDTH, dtype=jnp.int32) // HEAD_DIM
    sel = (head[:, None] == jnp.arange(LANES, dtype=jnp.int32)[None, :]).astype(bf16)
    consts = (g_kv, w_kva, g_kva, w_kvb, gk_n, gk_r, g_mix, w_in, g_qa, w_qb, gq_n, gq_r, sel)
    return pl.pallas_call(
        _mla_kvq_kernel,
        grid=(N_TOK // tm,),
        in_specs=[row(D_MODEL), row(D_MODEL), row(LANES), row(LANES)] + [full(a) for a in consts],
        out_specs=[row(D_MODEL), row(N_HEADS * HEAD_PAD), row(MIX_WIDTH), row(N_HEADS * HEAD_PAD), row(MEM_WIDTH)],
        out_shape=[jax.ShapeDtypeStruct((N_TOK, D_MODEL), f32),
                   jax.ShapeDtypeStruct((N_TOK, N_HEADS * HEAD_PAD), bf16),
                   jax.ShapeDtypeStruct((N_TOK, MIX_WIDTH), bf16),
                   jax.ShapeDtypeStruct((N_TOK, N_HEADS * HEAD_PAD), bf16),
                   jax.ShapeDtypeStruct((N_TOK, MEM_WIDTH), bf16)],
        compiler_params=_cparams(("arbitrary",)),
        name="mla_kvq",
    )(x0, y, cos, sin, *consts)


def _flash_kernel(q_ref, k_ref, v_ref, o_ref, m_s, l_s, acc_s, *, tq, heads):
    qi = pl.program_id(2)
    c_exp = (QK_HEAD ** -0.5) * 1.4426950408889634
    m_s[...] = jnp.full(m_s.shape, -jnp.inf, f32)
    l_s[...] = jnp.zeros(l_s.shape, f32)
    acc_s[...] = jnp.zeros(acc_s.shape, f32)
    n_blk = tq // LANES

    def chunk(kj, masked):
        r = pl.multiple_of(kj * tq, tq)

        def scores(h):
            q = q_ref[:, h * HEAD_PAD:(h + 1) * HEAD_PAD]
            s = _dot_nt(q, k_ref[pl.ds(r, tq), h * HEAD_PAD:(h + 1) * HEAD_PAD])
            if masked:
                qpos = lax.broadcasted_iota(jnp.int32, (tq, tq), 0)
                kpos = lax.broadcasted_iota(jnp.int32, (tq, tq), 1)
                s = jnp.where(kpos <= qpos, s, -jnp.inf)
            return s

        def update(h, s):
            blocks = [s[:, b * LANES:(b + 1) * LANES] for b in range(n_blk)]
            lane_max = functools.reduce(jnp.maximum, blocks)
            m_old = m_s[h]
            m_new = jnp.maximum(m_old, jnp.max(lane_max, axis=-1, keepdims=True))
            alpha = jnp.exp2((m_old - m_new) * c_exp)
            p = [jnp.exp2((blk - m_new) * c_exp) for blk in blocks]
            l_s[h] = alpha * l_s[h] + functools.reduce(jnp.add, p)
            pv = _dot(jnp.concatenate(p, axis=-1).astype(bf16),
                      v_ref[pl.ds(r, tq), h * HEAD_DIM:(h + 1) * HEAD_DIM])
            acc_s[h] = alpha * acc_s[h] + pv
            m_s[h] = m_new

        s_next = scores(0)
        for h in range(heads):
            s_cur = s_next
            if h + 1 < heads:
                s_next = scores(h + 1)
            update(h, s_cur)

    def body(kj, c):
        chunk(kj, False)
        return c

    lax.fori_loop(0, qi, body, 0)
    chunk(qi, True)
    for h in range(heads):
        l = jnp.sum(l_s[h], axis=-1, keepdims=True)
        o_ref[:, h * HEAD_DIM:(h + 1) * HEAD_DIM] = (acc_s[h] / l).astype(o_ref.dtype)


def flash_attention(q, k, v, tq=512, heads=6):
    per_seq = SEQ // tq
    return pl.pallas_call(
        functools.partial(_flash_kernel, tq=tq, heads=heads),
        grid=(BATCH, N_HEADS // heads, per_seq),
        in_specs=[pl.BlockSpec((tq, heads * HEAD_PAD), lambda b, h, i: (b * per_seq + i, h)),
                  pl.BlockSpec((SEQ, heads * HEAD_PAD), lambda b, h, i: (b, h)),
                  pl.BlockSpec((SEQ, heads * HEAD_DIM), lambda b, h, i: (b, h))],
        out_specs=pl.BlockSpec((tq, heads * HEAD_DIM), lambda b, h, i: (b * per_seq + i, h)),
        out_shape=jax.ShapeDtypeStruct((N_TOK, MIX_WIDTH), bf16),
        scratch_shapes=[pltpu.VMEM((heads, tq, LANES), f32),
                        pltpu.VMEM((heads, tq, LANES), f32),
                        pltpu.VMEM((heads, tq, HEAD_DIM), f32)],
        compiler_params=_cparams(("arbitrary", "arbitrary", "arbitrary")),
        name="flash_attention",
    )(q, k, v)


def _router_kernel(x_ref, g_ref, wr_ref, mi_ref, mf_ref, cnt_ref, carry, *, tm):
    i = pl.program_id(0)

    @pl.when(i == 0)
    def _():
        carry[...] = jnp.zeros_like(carry)

    h = _rms(x_ref[...], g_ref[...])
    w = wr_ref[...]
    h_hi, w_hi = h.astype(bf16), w.astype(bf16)
    h_lo, w_lo = (h - h_hi.astype(f32)).astype(bf16), (w - w_hi.astype(f32)).astype(bf16)
    logits = _dot(h_hi, w_hi) + (_dot(h_hi, w_lo) + _dot(h_lo, w_hi))
    lane = lax.broadcasted_iota(jnp.int32, (tm, LANES), 1)
    logits = jnp.where(lane < N_EXPERTS, logits, -jnp.inf)
    lane_f = lane.astype(f32)
    v1 = jnp.max(logits, axis=-1, keepdims=True)
    e1 = jnp.min(jnp.where(logits == v1, lane_f, float(LANES)), axis=-1, keepdims=True).astype(jnp.int32)
    rest = jnp.where(lane == e1, -jnp.inf, logits)
    v2 = jnp.max(rest, axis=-1, keepdims=True)
    e2 = jnp.min(jnp.where(rest == v2, lane_f, float(LANES)), axis=-1, keepdims=True).astype(jnp.int32)
    t = jnp.exp(v2 - v1)
    w1 = 1.0 / (1.0 + t)
    w2 = t / (1.0 + t)

    hot = jnp.where((lane == e1) | (lane == e2), 1.0, 0.0)
    r_io = lax.broadcasted_iota(jnp.int32, (tm, tm), 0)
    c_io = lax.broadcasted_iota(jnp.int32, (tm, tm), 1)
    below = jnp.where(c_io < r_io, 1.0, 0.0).astype(bf16)
    rank = _dot(below, hot.astype(bf16)) + carry[...]
    carry[...] += jnp.sum(hot, axis=0, keepdims=True)
    rank1 = jnp.sum(jnp.where(lane == e1, rank, 0.0), axis=-1, keepdims=True).astype(jnp.int32)
    rank2 = jnp.sum(jnp.where(lane == e2, rank, 0.0), axis=-1, keepdims=True).astype(jnp.int32)

    meta = jnp.where(lane == 0, e1, jnp.where(lane == 1, e2, jnp.where(lane == 2, rank1,
                     jnp.where(lane == 3, rank2, 0))))
    mi_ref[...] = meta.T[:8, :]
    mf_ref[...] = jnp.where(lane == 0, w1, jnp.where(lane == 1, w2, 0.0))
    cnt_ref[...] = carry[...].astype(jnp.int32)


def router(x, g, w_router_pad, tm=512):
    return pl.pallas_call(
        functools.partial(_router_kernel, tm=tm),
        grid=(N_TOK // tm,),
        in_specs=[pl.BlockSpec((tm, D_MODEL), lambda i: (i, 0)),
                  pl.BlockSpec((1, D_MODEL), lambda i: (0, 0)),
                  pl.BlockSpec((D_MODEL, LANES), lambda i: (0, 0))],
        out_specs=[pl.BlockSpec((8, tm), lambda i: (0, i)),
                   pl.BlockSpec((tm, LANES), lambda i: (i, 0)),
                   pl.BlockSpec((1, LANES), lambda i: (0, 0))],
        out_shape=[jax.ShapeDtypeStruct((8, N_TOK), jnp.int32),
                   jax.ShapeDtypeStruct((N_TOK, LANES), f32),
                   jax.ShapeDtypeStruct((1, LANES), jnp.int32)],
        scratch_shapes=[pltpu.VMEM((1, LANES), f32)],
        compiler_params=_cparams(("arbitrary",)),
        name="router",
    )(x, g, w_router_pad)


def _dispatch_kernel(p1_ref, p2_ref, pad0_ref, padn_ref, hp_ref, xs_out, zrow, sem, *, tm):
    base = pl.program_id(0) * tm

    @pl.when(pl.program_id(0) == 0)
    def _():
        zrow[...] = jnp.zeros(zrow.shape, zrow.dtype)

        def pad_copy(e, i):
            return pltpu.make_async_copy(zrow.at[pl.ds(0, 1)], xs_out.at[pl.ds(pad0_ref[e] + i, 1)], sem)

        def tail_copy(j):
            r = pl.multiple_of(j * FFN_SUB, FFN_SUB)
            return pltpu.make_async_copy(zrow, xs_out.at[pl.ds(r, FFN_SUB)], sem)

        tail0 = lax.shift_right_logical(pad0_ref[N_EXPERTS], FFN_SUB.bit_length() - 1)
        n_tail = xs_out.shape[0] // FFN_SUB
        for e in range(N_EXPERTS):
            lax.fori_loop(0, padn_ref[e], lambda i, c: (pad_copy(e, i).start(), c)[1], 0)
        lax.fori_loop(tail0, n_tail, lambda j, c: (tail_copy(j).start(), c)[1], 0)
        for e in range(N_EXPERTS):
            lax.fori_loop(0, padn_ref[e], lambda i, c: (pad_copy(e, i).wait(), c)[1], 0)
        lax.fori_loop(tail0, n_tail, lambda j, c: (tail_copy(j).wait(), c)[1], 0)

    def copy(i, pos_ref):
        return pltpu.make_async_copy(hp_ref.at[pl.ds(i, 1)], xs_out.at[pl.ds(pos_ref[base + i], 1)], sem)

    def start(i, c):
        copy(i, p1_ref).start()
        copy(i, p2_ref).start()
        return c

    def wait(i, c):
        copy(i, p1_ref).wait()
        copy(i, p2_ref).wait()
        return c

    lax.fori_loop(0, tm, start, 0, unroll=8)
    lax.fori_loop(0, tm, wait, 0, unroll=8)


def dispatch(pos1, pos2, pad_start, pad_len, hp, tm=512):
    grid_spec = pltpu.PrefetchScalarGridSpec(
        num_scalar_prefetch=4,
        grid=(N_TOK // tm,),
        in_specs=[pl.BlockSpec((tm, HALF_D), lambda i, *_: (i, 0))],
        out_specs=pl.BlockSpec(memory_space=pl.ANY),
        scratch_shapes=[pltpu.VMEM((FFN_SUB, HALF_D), jnp.uint32), pltpu.SemaphoreType.DMA(())],
    )
    return pl.pallas_call(
        functools.partial(_dispatch_kernel, tm=tm),
        grid_spec=grid_spec,
        out_shape=jax.ShapeDtypeStruct((MOE_ROWS, HALF_D), jnp.uint32),
        compiler_params=_cparams(("arbitrary",)),
        name="dispatch",
    )(pos1, pos2, pad_start, pad_len, hp)


def _combine_kernel(p1_ref, p2_ref, x_ref, mf_ref, y_hbm, o_ref, buf, sem, *, tm):
    i = pl.program_id(0)

    def copy(t, r, k, pos_ref):
        slot = t % 2
        return pltpu.make_async_copy(y_hbm.at[pl.ds(pos_ref[t * tm + r], 1)],
                                     buf.at[slot, k, pl.ds(r, 1)], sem.at[slot])

    def for_rows(fn):
        def body(r, c):
            fn(r)
            return c
        lax.fori_loop(0, tm, body, 0, unroll=8)

    def start_tile(t):
        for_rows(lambda r: (copy(t, r, 0, p1_ref).start(), copy(t, r, 1, p2_ref).start()))

    @pl.when(i == 0)
    def _():
        start_tile(0)

    @pl.when(i + 1 < pl.num_programs(0))
    def _():
        start_tile(i + 1)

    for_rows(lambda r: (copy(i, r, 0, p1_ref).wait(), copy(i, r, 1, p2_ref).wait()))
    w = mf_ref[...]
    slot = i % 2
    o_ref[...] = x_ref[...] + w[:, 0:1] * buf[slot, 0] + w[:, 1:2] * buf[slot, 1]


def combine(pos1, pos2, x, mf, y, tm=256):
    grid_spec = pltpu.PrefetchScalarGridSpec(
        num_scalar_prefetch=2,
        grid=(N_TOK // tm,),
        in_specs=[pl.BlockSpec((tm, D_MODEL), lambda i, p1, p2: (i, 0)),
                  pl.BlockSpec((tm, LANES), lambda i, p1, p2: (i, 0)),
                  pl.BlockSpec(memory_space=pl.ANY)],
        out_specs=pl.BlockSpec((tm, D_MODEL), lambda i, p1, p2: (i, 0)),
        scratch_shapes=[pltpu.VMEM((2, TOP_K, tm, D_MODEL), f32), pltpu.SemaphoreType.DMA((2,))],
    )
    return pl.pallas_call(
        functools.partial(_combine_kernel, tm=tm),
        grid_spec=grid_spec,
        out_shape=jax.ShapeDtypeStruct((N_TOK, D_MODEL), f32),
        compiler_params=_cparams(("arbitrary",)),
        name="combine",
    )(pos1, pos2, x, mf, y)


def _rope_lanes(v):
    z = jnp.zeros(v.shape[:-1] + (QK_ROPE // 2,), v.dtype)
    return jnp.concatenate([v[..., :QK_ROPE // 2], z, v[..., QK_ROPE // 2:], z], axis=-1)


def kernel(x, mem, positions, g_mix, g_ffn, g_mem, w_mem_kv, g_mq, g_mk, w_out, a_w_in, a_conv_w, b_w_in,
           b_g_q_a, b_w_q_b, b_g_qn, g_kv, w_kv_a, g_kv_a, w_kv_b, g_kn, ffn_w_gate, ffn_w_up, ffn_w_down,
           moe_w_router, moe_w_gate, moe_w_up, moe_w_down):
    row = lambda v: v.reshape(1, -1).astype(f32)
    x0 = x.reshape(N_TOK, D_MODEL)
    mem2 = mem.reshape(BATCH * N_MEM, D_MODEL)

    w_kva = jnp.concatenate([w_kv_a[:, :KV_LORA], _rope_lanes(w_kv_a[:, KV_LORA:])], axis=1).astype(bf16)
    kvb = w_kv_b.reshape(KV_LORA, N_HEADS, QK_NOPE + HEAD_DIM)
    w_kvb = jnp.concatenate([kvb[:, :, :QK_NOPE].reshape(KV_LORA, MIX_WIDTH),
                             kvb[:, :, QK_NOPE:].reshape(KV_LORA, MIX_WIDTH)], axis=1).astype(bf16)
    qb = b_w_q_b[0].reshape(Q_LORA, N_HEADS, QK_HEAD)
    w_qb = jnp.concatenate([qb[:, :, :QK_NOPE], _rope_lanes(qb[:, :, QK_NOPE:])], axis=-1)
    w_qb = w_qb.reshape(Q_LORA, N_HEADS * HEAD_PAD).astype(bf16)
    gk_n, gk_r = row(g_kn[:QK_NOPE]), row(_rope_lanes(g_kn[QK_NOPE:]))
    gq_n, gq_r = row(b_g_qn[0, :QK_NOPE]), row(_rope_lanes(b_g_qn[0, QK_NOPE:]))
    w_router_pad = jnp.pad(moe_w_router[0].astype(f32), ((0, 0), (0, LANES - N_EXPERTS)))

    inv_freq = 1.0 / (ROPE_THETA ** (jnp.arange(0, QK_ROPE, 2, dtype=f32) / QK_ROPE))
    ang = positions.reshape(N_TOK, 1).astype(f32) * inv_freq[None, :]
    zeros = jnp.zeros_like(ang)
    cos_t = jnp.concatenate([jnp.cos(ang), zeros, jnp.cos(ang), zeros], axis=-1)
    sin_t = jnp.concatenate([-jnp.sin(ang), zeros, jnp.sin(ang), zeros], axis=-1)

    u0 = norm_matmul(x0, row(g_mix[0]), a_w_in[0].astype(bf16), tm=1024, tn=1280)
    kvm0 = norm_matmul(mem2, row(g_mem[0]), w_mem_kv[0].astype(bf16), tm=BATCH * N_MEM, tn=2 * MEM_WIDTH)
    xa, hp0 = conv_tail(u0, a_conv_w[0].astype(f32), kvm0, row(g_mq[0]), row(g_mk[0]),
                        w_out[0].astype(bf16), x0, row(g_ffn[0]))
    dense_tbl = _super_tiles(jnp.array([N_TOK], jnp.int32), jnp.array([0], jnp.int32),
                             -(-N_TOK // (FFN_SUB * FFN_SUPER)))
    y0 = swiglu_ffn(hp0, ffn_w_gate, ffn_w_up, ffn_w_down, *dense_tbl)

    x1, k, v, q, um1 = mla_kvq(xa, y0, cos_t, sin_t, row(g_kv), w_kva, row(g_kv_a), w_kvb, gk_n, gk_r,
                               row(g_mix[1]), b_w_in[0].astype(bf16), row(b_g_q_a[0]), w_qb, gq_n, gq_r)
    mix1 = flash_attention(q, k, v)
    kvm1 = norm_matmul(mem2, row(g_mem[1]), w_mem_kv[1].astype(bf16), tm=BATCH * N_MEM, tn=2 * MEM_WIDTH)
    x2, hp1 = attn_tail(mix1, um1, 0, kvm1, row(g_mq[1]), row(g_mk[1]), w_out[1].astype(bf16), x1, row(g_ffn[1]))

    mi, mf, cnt = router(x2, row(g_ffn[1]), w_router_pad)
    counts = cnt[0, :N_EXPERTS]
    padded = (counts + FFN_SUB - 1) // FFN_SUB * FFN_SUB
    offsets = jnp.cumsum(padded) - padded
    pos1 = offsets[mi[0]] + mi[2]
    pos2 = offsets[mi[1]] + mi[3]
    pad_start = jnp.concatenate([offsets + counts, jnp.sum(padded, keepdims=True)])
    xs = dispatch(pos1, pos2, pad_start, padded - counts, hp1)
    y1 = swiglu_ffn(xs, moe_w_gate[0], moe_w_up[0], moe_w_down[0], *_super_tiles(padded, offsets, MOE_SUPERS))
    out = combine(pos1, pos2, x2, mf, y1)
    return out.reshape(BATCH, SEQ, D_MODEL)
```

```python
import functools

import jax
import jax.numpy as jnp
from jax import lax
from jax.experimental import pallas as pl
from jax.experimental.pallas import tpu as pltpu

D_MODEL = 2048
BATCH = 2
SEQ = 4096
N_TOK = BATCH * SEQ
N_HEADS = 12
HEAD_DIM = 128
MIX_WIDTH = N_HEADS * HEAD_DIM
MEM_HEADS = 4
MEM_WIDTH = MEM_HEADS * HEAD_DIM
N_MEM = 256
CONV_WIDTH = 3
Q_LORA = 512
KV_LORA = 256
QK_NOPE = 128
QK_ROPE = 64
QK_HEAD = QK_NOPE + QK_ROPE
ROPE_THETA = 10000.0
D_FF = 7168
N_EXPERTS = 8
TOP_K = 2
EPS = 1e-6

LANES = 128
SUBLANES = 8
HEAD_PAD = 2 * LANES
HALF_D = D_MODEL // 2
VMEM_LIMIT = 56 * 1024 * 1024

FFN_SUB = 256
FFN_SUPER = 9
FFN_TF = 256
MOE_ROWS = N_TOK * TOP_K + N_EXPERTS * FFN_SUB
MOE_SUPERS = N_EXPERTS + -(-MOE_ROWS // (FFN_SUB * FFN_SUPER))

f32 = jnp.float32
bf16 = jnp.bfloat16


def _cparams(sem):
    return pltpu.CompilerParams(dimension_semantics=sem, vmem_limit_bytes=VMEM_LIMIT)


def _rms(x, g, n=None):
    n = x.shape[-1] if n is None else n
    ms = jnp.sum(x * x, axis=-1, keepdims=True) * (1.0 / n)
    return x * lax.rsqrt(ms + EPS) * g


def _pack_halves(h):
    hb = h.astype(bf16).astype(f32)
    return pltpu.pack_elementwise([hb[:, :HALF_D], hb[:, HALF_D:]], packed_dtype=bf16)


def _unpack_halves(p):
    lo = pltpu.unpack_elementwise(p, index=0, packed_dtype=bf16, unpacked_dtype=f32)
    hi = pltpu.unpack_elementwise(p, index=1, packed_dtype=bf16, unpacked_dtype=f32)
    return lo.astype(bf16), hi.astype(bf16)


def _dot(a, b):
    return jnp.dot(a, b, preferred_element_type=f32)


def _dot_nt(a, b):
    return lax.dot_general(a, b, (((1,), (1,)), ((), ())), preferred_element_type=f32)


def _norm_matmul_kernel(x_ref, g_ref, w_ref, o_ref, h_ref):
    @pl.when(pl.program_id(1) == 0)
    def _():
        h_ref[...] = _rms(x_ref[...], g_ref[...]).astype(bf16)

    o_ref[...] = _dot(h_ref[...], w_ref[...].astype(bf16)).astype(o_ref.dtype)


def norm_matmul(x, g, w, tm, tn, out_dtype=bf16):
    m, d = x.shape
    n = w.shape[1]
    return pl.pallas_call(
        _norm_matmul_kernel,
        grid=(m // tm, n // tn),
        in_specs=[pl.BlockSpec((tm, d), lambda i, j: (i, 0)),
                  pl.BlockSpec((1, d), lambda i, j: (0, 0)),
                  pl.BlockSpec((d, tn), lambda i, j: (0, j))],
        out_specs=pl.BlockSpec((tm, tn), lambda i, j: (i, j)),
        out_shape=jax.ShapeDtypeStruct((m, n), out_dtype),
        scratch_shapes=[pltpu.VMEM((tm, d), bf16)],
        compiler_params=_cparams(("arbitrary", "arbitrary")),
        name="norm_matmul",
    )(x, g, w)


def _mem_attention(um_ref, kvm_ref, gq_ref, gk_ref, cat_ref):
    scale = HEAD_DIM ** -0.5
    for h in range(MEM_HEADS):
        cs = slice(h * HEAD_DIM, (h + 1) * HEAD_DIM)
        q = _rms(um_ref[:, cs].astype(f32), gq_ref[...]).astype(bf16)
        k = _rms(kvm_ref[:, cs].astype(f32), gk_ref[...]).astype(bf16)
        v = kvm_ref[:, MEM_WIDTH + h * HEAD_DIM:MEM_WIDTH + (h + 1) * HEAD_DIM]
        s = _dot_nt(q, k) * scale
        p = jnp.exp(s - jnp.max(s, axis=-1, keepdims=True))
        o = _dot(p.astype(bf16), v) / jnp.sum(p, axis=-1, keepdims=True)
        cat_ref[:, MIX_WIDTH + h * HEAD_DIM:MIX_WIDTH + (h + 1) * HEAD_DIM] = o.astype(bf16)


def _tail_epilogue(cat_ref, wout_ref, x_ref, gffn_ref, xo_ref, hp_ref):
    xn = x_ref[...] + _dot(cat_ref[...], wout_ref[...])
    xo_ref[...] = xn
    hp_ref[...] = _pack_halves(_rms(xn, gffn_ref[...]))


def _conv_tail_kernel(xin_ref, gb_ref, gc_ref, um_ref, hx_ref, hg_ref, cw_ref, kvm_ref, gq_ref, gk_ref,
                      wout_ref, x_ref, gffn_ref, xo_ref, hp_ref, cat_ref, *, tm):
    i = pl.program_id(0)
    first = (i * tm) % SEQ == 0
    rows = lax.broadcasted_iota(jnp.int32, (tm, 1), 0)
    cc = 512
    for c in range(MIX_WIDTH // cc):
        cs = slice(c * cc, (c + 1) * cc)
        z = gc_ref[:, cs].astype(f32) * xin_ref[:, cs].astype(f32)
        zh = hg_ref[:, cs].astype(f32) * hx_ref[:, cs].astype(f32)
        zh = jnp.where(first, 0.0, zh)
        p1 = zh[15:16, :]
        p2 = zh[14:15, :]
        z1 = jnp.where(rows == 0, p1, pltpu.roll(z, 1, 0))
        z2 = jnp.where(rows == 0, p2, jnp.where(rows == 1, p1, pltpu.roll(z, 2, 0)))
        y = cw_ref[0:1, cs] * z2 + cw_ref[1:2, cs] * z1 + cw_ref[2:3, cs] * z
        cat_ref[:, cs] = (gb_ref[:, cs].astype(f32) * y).astype(bf16)
    _mem_attention(um_ref, kvm_ref, gq_ref, gk_ref, cat_ref)
    _tail_epilogue(cat_ref, wout_ref, x_ref, gffn_ref, xo_ref, hp_ref)


def _attn_tail_kernel(mix_ref, um_ref, kvm_ref, gq_ref, gk_ref, wout_ref, x_ref, gffn_ref,
                      xo_ref, hp_ref, cat_ref):
    cat_ref[:, :MIX_WIDTH] = mix_ref[...]
    _mem_attention(um_ref, kvm_ref, gq_ref, gk_ref, cat_ref)
    _tail_epilogue(cat_ref, wout_ref, x_ref, gffn_ref, xo_ref, hp_ref)


def _tail_common_specs(tm):
    per_seq = SEQ // tm
    return [pl.BlockSpec((N_MEM, 2 * MEM_WIDTH), lambda i: (i // per_seq, 0)),
            pl.BlockSpec((1, HEAD_DIM), lambda i: (0, 0)),
            pl.BlockSpec((1, HEAD_DIM), lambda i: (0, 0)),
            pl.BlockSpec((D_MODEL, D_MODEL), lambda i: (0, 0), pipeline_mode=pl.Buffered(1)),
            pl.BlockSpec((tm, D_MODEL), lambda i: (i, 0)),
            pl.BlockSpec((1, D_MODEL), lambda i: (0, 0))]


def _tail_outs(tm):
    return dict(
        out_specs=[pl.BlockSpec((tm, D_MODEL), lambda i: (i, 0)),
                   pl.BlockSpec((tm, HALF_D), lambda i: (i, 0))],
        out_shape=[jax.ShapeDtypeStruct((N_TOK, D_MODEL), f32),
                   jax.ShapeDtypeStruct((N_TOK, HALF_D), jnp.uint32)],
        scratch_shapes=[pltpu.VMEM((tm, D_MODEL), bf16)],
        compiler_params=_cparams(("arbitrary",)),
    )


def conv_tail(u, conv_w, kvm, g_mq, g_mk, w_out, x, g_ffn, tm=512):
    halo = 16
    um_blk = 3 * MIX_WIDTH // MEM_WIDTH
    prev = lambda i: jnp.maximum(i * (tm // halo) - 1, 0)
    in_specs = [pl.BlockSpec((tm, MIX_WIDTH), lambda i: (i, 0)),
                pl.BlockSpec((tm, MIX_WIDTH), lambda i: (i, 1)),
                pl.BlockSpec((tm, MIX_WIDTH), lambda i: (i, 2)),
                pl.BlockSpec((tm, MEM_WIDTH), lambda i: (i, um_blk)),
                pl.BlockSpec((halo, MIX_WIDTH), lambda i: (prev(i), 0)),
                pl.BlockSpec((halo, MIX_WIDTH), lambda i: (prev(i), 2)),
                pl.BlockSpec((CONV_WIDTH, MIX_WIDTH), lambda i: (0, 0))] + _tail_common_specs(tm)
    return pl.pallas_call(
        functools.partial(_conv_tail_kernel, tm=tm),
        grid=(N_TOK // tm,), in_specs=in_specs, name="conv_tail", **_tail_outs(tm),
    )(u, u, u, u, u, u, conv_w, kvm, g_mq, g_mk, w_out, x, g_ffn)


def attn_tail(mix, um_src, um_blk, kvm, g_mq, g_mk, w_out, x, g_ffn, tm=512):
    in_specs = [pl.BlockSpec((tm, MIX_WIDTH), lambda i: (i, 0)),
                pl.BlockSpec((tm, MEM_WIDTH), lambda i: (i, um_blk))] + _tail_common_specs(tm)
    return pl.pallas_call(
        _attn_tail_kernel,
        grid=(N_TOK // tm,), in_specs=in_specs, name="attn_tail", **_tail_outs(tm),
    )(mix, um_src, kvm, g_mq, g_mk, w_out, x, g_ffn)


def _ffn_kernel(exp_ref, row0_ref, nsub_ref, tail_ref, xs_hbm, wg_ref, wu_ref, wd_ref, y_hbm,
                x_s, acc, wg_s, wu_s, wd_s, sem):
    s = pl.program_id(0)
    f = pl.program_id(1)
    nsub = nsub_ref[s]
    row0 = row0_ref[s]
    last_f = f == pl.num_programs(1) - 1
    last_step = (s == pl.num_programs(0) - 1) & last_f

    def x_copy(j):
        r = pl.multiple_of(j * FFN_SUB, FFN_SUB)
        g = pl.multiple_of(row0 + r, FFN_SUB)
        return pltpu.make_async_copy(xs_hbm.at[pl.ds(g, FFN_SUB)], x_s.at[pl.ds(r, FFN_SUB)], sem)

    def y_copy(r, rows):
        g = pl.multiple_of(row0 + r, FFN_SUB)
        return pltpu.make_async_copy(acc.at[pl.ds(r, rows)], y_hbm.at[pl.ds(g, rows)], sem)

    def for_subtiles(fn):
        def body(j, c):
            fn(j)
            return c
        lax.fori_loop(0, nsub, body, 0)

    @pl.when(f == 0)
    def _():
        for_subtiles(lambda j: x_copy(j).start())

        def zero(j):
            r = pl.multiple_of(j * FFN_SUB, FFN_SUB)
            acc[pl.ds(r, FFN_SUB), :] = jnp.zeros((FFN_SUB, D_MODEL), f32)
        for_subtiles(zero)
        for_subtiles(lambda j: x_copy(j).wait())

    def swiglu_rows(r, rows, wg, wu, wd):
        lo, hi = _unpack_halves(x_s[pl.ds(r, rows), :])
        g = _dot(lo, wg[:HALF_D, :]) + _dot(hi, wg[HALF_D:, :])
        u = _dot(lo, wu[:HALF_D, :]) + _dot(hi, wu[HALF_D:, :])
        a = (g * jax.nn.sigmoid(g) * u).astype(bf16)
        acc[pl.ds(r, rows), :] += _dot(a, wd[...])

        @pl.when(last_f)
        def _():
            y_copy(r, rows).start()

    rem = nsub - 1
    n_quad = lax.shift_right_logical(rem, 2)
    quad_row = lambda j: pl.multiple_of(FFN_SUB + j * 4 * FFN_SUB, FFN_SUB)
    pair_row = pl.multiple_of(FFN_SUB + n_quad * 4 * FFN_SUB, FFN_SUB)
    last_row = pl.multiple_of(rem * FFN_SUB, FFN_SUB)

    def for_groups(first, fn):
        first()

        def body(j, c):
            fn(quad_row(j), 4 * FFN_SUB)
            return c
        lax.fori_loop(0, n_quad, body, 0)

        @pl.when((rem & 2) != 0)
        def _():
            fn(pair_row, 2 * FFN_SUB)

        @pl.when((rem & 1) != 0)
        def _():
            fn(last_row, FFN_SUB)

    def first_group():
        wg = wg_ref[0].astype(bf16)
        wu = wu_ref[0].astype(bf16)
        wd = wd_ref[0].astype(bf16)
        wg_s[...] = wg
        wu_s[...] = wu
        wd_s[...] = wd
        swiglu_rows(0, FFN_SUB, wg, wu, wd)

    for_groups(first_group, lambda r, rows: swiglu_rows(r, rows, wg_s, wu_s, wd_s))

    @pl.when(last_f)
    def _():
        for_groups(lambda: y_copy(0, FFN_SUB).wait(), lambda r, rows: y_copy(r, rows).wait())

    @pl.when(last_step)
    def _():
        acc[pl.ds(0, FFN_SUB), :] = jnp.zeros((FFN_SUB, D_MODEL), f32)

        def tail_copy(j):
            g = pl.multiple_of(j * FFN_SUB, FFN_SUB)
            return pltpu.make_async_copy(acc.at[pl.ds(0, FFN_SUB)], y_hbm.at[pl.ds(g, FFN_SUB)], sem)

        n_tail = y_hbm.shape[0] // FFN_SUB
        lax.fori_loop(tail_ref[0], n_tail, lambda j, c: (tail_copy(j).start(), c)[1], 0)
        lax.fori_loop(tail_ref[0], n_tail, lambda j, c: (tail_copy(j).wait(), c)[1], 0)


def swiglu_ffn(xs, w_gate, w_up, w_down, st_expert, st_row0, st_nsub, st_tail):
    rows = xs.shape[0]
    nf = D_FF // FFN_TF

    grid_spec = pltpu.PrefetchScalarGridSpec(
        num_scalar_prefetch=4,
        grid=(st_tail[1], nf),
        in_specs=[pl.BlockSpec(memory_space=pl.ANY),
                  pl.BlockSpec((1, D_MODEL, FFN_TF), lambda s, f, e, r, n, t: (e[s], 0, f)),
                  pl.BlockSpec((1, D_MODEL, FFN_TF), lambda s, f, e, r, n, t: (e[s], 0, f)),
                  pl.BlockSpec((1, FFN_TF, D_MODEL), lambda s, f, e, r, n, t: (e[s], f, 0))],
        out_specs=pl.BlockSpec(memory_space=pl.ANY),
        scratch_shapes=[pltpu.VMEM((FFN_SUPER * FFN_SUB, HALF_D), jnp.uint32),
                        pltpu.VMEM((FFN_SUPER * FFN_SUB, D_MODEL), f32),
                        pltpu.VMEM((D_MODEL, FFN_TF), bf16),
                        pltpu.VMEM((D_MODEL, FFN_TF), bf16),
                        pltpu.VMEM((FFN_TF, D_MODEL), bf16),
                        pltpu.SemaphoreType.DMA(())],
    )
    return pl.pallas_call(
        _ffn_kernel,
        grid_spec=grid_spec,
        out_shape=jax.ShapeDtypeStruct((rows, D_MODEL), f32),
        compiler_params=_cparams(("arbitrary", "arbitrary")),
        name="swiglu_ffn",
    )(st_expert, st_row0, st_nsub, st_tail, xs, w_gate, w_up, w_down)


def _super_tiles(padded, offsets, n_super):
    cap = FFN_SUB * FFN_SUPER
    per_group = (padded + cap - 1) // cap
    ends = jnp.cumsum(per_group)
    t = jnp.arange(n_super, dtype=jnp.int32)
    grp = jnp.minimum(jnp.sum((t[:, None] >= ends[None, :]).astype(jnp.int32), axis=1), padded.shape[0] - 1)
    k = t - (ends[grp] - per_group[grp])
    used = t < ends[-1]
    nsub = jnp.where(used, jnp.minimum(FFN_SUPER, (padded[grp] - k * cap) // FFN_SUB), 0)
    row0 = jnp.where(used, offsets[grp] + k * cap, 0)
    last_grp = grp[jnp.maximum(ends[-1] - 1, 0)]
    grp = jnp.where(used, grp, last_grp)
    tail = jnp.stack([jnp.sum(padded) // FFN_SUB, ends[-1]])
    return grp.astype(jnp.int32), row0.astype(jnp.int32), nsub.astype(jnp.int32), tail.astype(jnp.int32)


def _rope(x, cos, sin):
    return x * cos + pltpu.roll(x, LANES // 2, 1) * sin


def _dot_split(a, sel):
    hi = a.astype(bf16)
    lo = (a - hi.astype(f32)).astype(bf16)
    return _dot(hi, sel) + _dot(lo, sel)


def _mla_kvq_kernel(x0_ref, y_ref, pos_ref, freq_ref, sign_ref, gkv_ref, wkva_ref, gkva_ref, wkvb_ref, gkn_ref, gkr_ref,
                    gmix_ref, win_ref, gqa_ref, wqb_ref, gqn_ref, gqr_ref, sel_ref,
                    x1_ref, k_ref, v_ref, q_ref, um_ref):
    x1 = x0_ref[...] + y_ref[...]
    x1_ref[...] = x1
    ang = pos_ref[...].astype(f32) * freq_ref[...]
    cos = jnp.cos(ang)
    sin = jnp.sin(ang) * sign_ref[...]

    tm = x1.shape[0]
    kva = _dot(_rms(x1, gkv_ref[...]).astype(bf16), wkva_ref[...])
    u = _dot(_rms(x1, gmix_ref[...]).astype(bf16), win_ref[...])
    um_ref[...] = u[:, Q_LORA:].astype(bf16)
    c_kv = _rms(kva[:, :KV_LORA], gkva_ref[...]).astype(bf16)
    kv = _dot(c_kv, wkvb_ref[...])
    c_q = _rms(u[:, :Q_LORA], gqa_ref[...]).astype(bf16)
    q = _dot(c_q, wqb_ref[...])
    v_ref[...] = kv[:, MIX_WIDTH:].astype(bf16)

    kpe = kva[:, KV_LORA:]
    ss_pe = jnp.sum(kpe * kpe, axis=-1, keepdims=True)
    sck = lax.rsqrt((_dot_split(kv[:, :MIX_WIDTH] * kv[:, :MIX_WIDTH], sel_ref[...]) + ss_pe) * (1.0 / QK_HEAD) + EPS)
    q2 = jnp.concatenate([q[:, h * HEAD_PAD:h * HEAD_PAD + LANES] * q[:, h * HEAD_PAD:h * HEAD_PAD + LANES]
                          + q[:, h * HEAD_PAD + LANES:(h + 1) * HEAD_PAD] * q[:, h * HEAD_PAD + LANES:(h + 1) * HEAD_PAD]
                          for h in range(N_HEADS)], axis=-1)
    scq = lax.rsqrt(_dot_split(q2, sel_ref[...]) * (1.0 / QK_HEAD) + EPS)
    k_rot = _rope(kpe * gkr_ref[...], cos, sin)
    for h in range(N_HEADS):
        sch = jnp.broadcast_to(sck[:, h:h + 1], (tm, LANES))
        k_ref[:, h * HEAD_PAD:h * HEAD_PAD + LANES] = (
            kv[:, h * HEAD_DIM:(h + 1) * HEAD_DIM] * sch * gkn_ref[...]).astype(bf16)
        k_ref[:, h * HEAD_PAD + LANES:(h + 1) * HEAD_PAD] = (k_rot * sch).astype(bf16)
        sch = jnp.broadcast_to(scq[:, h:h + 1], (tm, LANES))
        qn = q[:, h * HEAD_PAD:h * HEAD_PAD + LANES]
        qr = q[:, h * HEAD_PAD + LANES:(h + 1) * HEAD_PAD]
        q_ref[:, h * HEAD_PAD:h * HEAD_PAD + LANES] = (qn * sch * gqn_ref[...]).astype(bf16)
        q_ref[:, h * HEAD_PAD + LANES:(h + 1) * HEAD_PAD] = (_rope(qr * gqr_ref[...], cos, sin) * sch).astype(bf16)


def mla_kvq(x0, y, pos, rope_freq, rope_sign, g_kv, w_kva, g_kva, w_kvb, gk_n, gk_r, g_mix, w_in, g_qa, w_qb, gq_n, gq_r, tm=256):
    row = lambda w: pl.BlockSpec((tm, w), lambda i: (i, 0))
    full = lambda a: pl.BlockSpec(a.shape, lambda i: (0, 0))
    head = jnp.arange(MIX_WIDTH, dtype=jnp.int32) // HEAD_DIM
    sel = (head[:, None] == jnp.arange(LANES, dtype=jnp.int32)[None, :]).astype(bf16)
    consts = (rope_freq, rope_sign, g_kv, w_kva, g_kva, w_kvb, gk_n, gk_r, g_mix, w_in, g_qa, w_qb, gq_n, gq_r, sel)
    return pl.pallas_call(
        _mla_kvq_kernel,
        grid=(N_TOK // tm,),
        in_specs=[row(D_MODEL), row(D_MODEL), row(1)] + [full(a) for a in consts],
        out_specs=[row(D_MODEL), row(N_HEADS * HEAD_PAD), row(MIX_WIDTH), row(N_HEADS * HEAD_PAD), row(MEM_WIDTH)],
        out_shape=[jax.ShapeDtypeStruct((N_TOK, D_MODEL), f32),
                   jax.ShapeDtypeStruct((N_TOK, N_HEADS * HEAD_PAD), bf16),
                   jax.ShapeDtypeStruct((N_TOK, MIX_WIDTH), bf16),
                   jax.ShapeDtypeStruct((N_TOK, N_HEADS * HEAD_PAD), bf16),
                   jax.ShapeDtypeStruct((N_TOK, MEM_WIDTH), bf16)],
        compiler_params=_cparams(("arbitrary",)),
        name="mla_kvq",
    )(x0, y, pos, *consts)


def _flash_kernel(q_ref, k_ref, v_ref, o_ref, m_s, l_s, acc_s, *, tq, heads):
    qi = pl.program_id(2)
    c_exp = (QK_HEAD ** -0.5) * 1.4426950408889634
    m_s[...] = jnp.full(m_s.shape, -jnp.inf, f32)
    l_s[...] = jnp.zeros(l_s.shape, f32)
    acc_s[...] = jnp.zeros(acc_s.shape, f32)
    n_blk = tq // LANES

    def chunk(kj, masked):
        r = pl.multiple_of(kj * tq, tq)

        def scores(h):
            q = q_ref[:, h * HEAD_PAD:(h + 1) * HEAD_PAD]
            s = _dot_nt(q, k_ref[pl.ds(r, tq), h * HEAD_PAD:(h + 1) * HEAD_PAD])
            if masked:
                qpos = lax.broadcasted_iota(jnp.int32, (tq, tq), 0)
                kpos = lax.broadcasted_iota(jnp.int32, (tq, tq), 1)
                s = jnp.where(kpos <= qpos, s, -jnp.inf)
            return s

        def update(h, s):
            blocks = [s[:, b * LANES:(b + 1) * LANES] for b in range(n_blk)]
            lane_max = functools.reduce(jnp.maximum, blocks)
            m_old = m_s[h]
            m_new = jnp.maximum(m_old, jnp.max(lane_max, axis=-1, keepdims=True))
            alpha = jnp.exp2((m_old - m_new) * c_exp)
            p = [jnp.exp2((blk - m_new) * c_exp) for blk in blocks]
            l_s[h] = alpha * l_s[h] + functools.reduce(jnp.add, p)
            pv = _dot(jnp.concatenate(p, axis=-1).astype(bf16),
                      v_ref[pl.ds(r, tq), h * HEAD_DIM:(h + 1) * HEAD_DIM])
            acc_s[h] = alpha * acc_s[h] + pv
            m_s[h] = m_new

        s_next = scores(0)
        for h in range(heads):
            s_cur = s_next
            if h + 1 < heads:
                s_next = scores(h + 1)
            update(h, s_cur)

    def body(kj, c):
        chunk(kj, False)
        return c

    lax.fori_loop(0, qi, body, 0)
    chunk(qi, True)
    for h in range(heads):
        l = jnp.sum(l_s[h], axis=-1, keepdims=True)
        o_ref[:, h * HEAD_DIM:(h + 1) * HEAD_DIM] = (acc_s[h] / l).astype(o_ref.dtype)


def flash_attention(q, k, v, tq=512, heads=6):
    per_seq = SEQ // tq
    return pl.pallas_call(
        functools.partial(_flash_kernel, tq=tq, heads=heads),
        grid=(BATCH, N_HEADS // heads, per_seq),
        in_specs=[pl.BlockSpec((tq, heads * HEAD_PAD), lambda b, h, i: (b * per_seq + i, h)),
                  pl.BlockSpec((SEQ, heads * HEAD_PAD), lambda b, h, i: (b, h)),
                  pl.BlockSpec((SEQ, heads * HEAD_DIM), lambda b, h, i: (b, h))],
        out_specs=pl.BlockSpec((tq, heads * HEAD_DIM), lambda b, h, i: (b * per_seq + i, h)),
        out_shape=jax.ShapeDtypeStruct((N_TOK, MIX_WIDTH), bf16),
        scratch_shapes=[pltpu.VMEM((heads, tq, LANES), f32),
                        pltpu.VMEM((heads, tq, LANES), f32),
                        pltpu.VMEM((heads, tq, HEAD_DIM), f32)],
        compiler_params=_cparams(("arbitrary", "arbitrary", "arbitrary")),
        name="flash_attention",
    )(q, k, v)


def _router_kernel(x_ref, g_ref, wr_ref, mi_ref, mf_ref, cnt_ref, carry, *, tm):
    i = pl.program_id(0)

    @pl.when(i == 0)
    def _():
        carry[...] = jnp.zeros_like(carry)

    h = _rms(x_ref[...], g_ref[...])
    w = wr_ref[...]
    h_hi, w_hi = h.astype(bf16), w.astype(bf16)
    h_lo, w_lo = (h - h_hi.astype(f32)).astype(bf16), (w - w_hi.astype(f32)).astype(bf16)
    logits = _dot(h_hi, w_hi) + (_dot(h_hi, w_lo) + _dot(h_lo, w_hi))
    lane = lax.broadcasted_iota(jnp.int32, (tm, LANES), 1)
    logits = jnp.where(lane < N_EXPERTS, logits, -jnp.inf)
    lane_f = lane.astype(f32)
    v1 = jnp.max(logits, axis=-1, keepdims=True)
    e1 = jnp.min(jnp.where(logits == v1, lane_f, float(LANES)), axis=-1, keepdims=True).astype(jnp.int32)
    rest = jnp.where(lane == e1, -jnp.inf, logits)
    v2 = jnp.max(rest, axis=-1, keepdims=True)
    e2 = jnp.min(jnp.where(rest == v2, lane_f, float(LANES)), axis=-1, keepdims=True).astype(jnp.int32)
    t = jnp.exp(v2 - v1)
    w1 = 1.0 / (1.0 + t)
    w2 = t / (1.0 + t)

    hot = jnp.where((lane == e1) | (lane == e2), 1.0, 0.0)
    r_io = lax.broadcasted_iota(jnp.int32, (tm, tm), 0)
    c_io = lax.broadcasted_iota(jnp.int32, (tm, tm), 1)
    below = jnp.where(c_io < r_io, 1.0, 0.0).astype(bf16)
    rank = _dot(below, hot.astype(bf16)) + carry[...]
    carry[...] += jnp.sum(hot, axis=0, keepdims=True)
    rank1 = jnp.sum(jnp.where(lane == e1, rank, 0.0), axis=-1, keepdims=True).astype(jnp.int32)
    rank2 = jnp.sum(jnp.where(lane == e2, rank, 0.0), axis=-1, keepdims=True).astype(jnp.int32)

    meta = jnp.where(lane == 0, e1, jnp.where(lane == 1, e2, jnp.where(lane == 2, rank1,
                     jnp.where(lane == 3, rank2, 0))))
    mi_ref[...] = meta.T[:8, :]
    mf_ref[...] = jnp.where(lane == 0, w1, jnp.where(lane == 1, w2, 0.0))
    cnt_ref[...] = carry[...].astype(jnp.int32)


def router(x, g, w_router_pad, tm=512):
    return pl.pallas_call(
        functools.partial(_router_kernel, tm=tm),
        grid=(N_TOK // tm,),
        in_specs=[pl.BlockSpec((tm, D_MODEL), lambda i: (i, 0)),
                  pl.BlockSpec((1, D_MODEL), lambda i: (0, 0)),
                  pl.BlockSpec((D_MODEL, LANES), lambda i: (0, 0))],
        out_specs=[pl.BlockSpec((8, tm), lambda i: (0, i)),
                   pl.BlockSpec((tm, LANES), lambda i: (i, 0)),
                   pl.BlockSpec((1, LANES), lambda i: (0, 0))],
        out_shape=[jax.ShapeDtypeStruct((8, N_TOK), jnp.int32),
                   jax.ShapeDtypeStruct((N_TOK, LANES), f32),
                   jax.ShapeDtypeStruct((1, LANES), jnp.int32)],
        scratch_shapes=[pltpu.VMEM((1, LANES), f32)],
        compiler_params=_cparams(("arbitrary",)),
        name="router",
    )(x, g, w_router_pad)


def _dispatch_kernel(p1_ref, p2_ref, pad0_ref, padn_ref, hp_ref, xs_out, zrow, sem, *, tm):
    base = pl.program_id(0) * tm

    @pl.when(pl.program_id(0) == 0)
    def _():
        zrow[...] = jnp.zeros(zrow.shape, zrow.dtype)

        def pad_copy(e, i):
            return pltpu.make_async_copy(zrow.at[pl.ds(0, 1)], xs_out.at[pl.ds(pad0_ref[e] + i, 1)], sem)

        def tail_copy(j):
            r = pl.multiple_of(j * FFN_SUB, FFN_SUB)
            return pltpu.make_async_copy(zrow, xs_out.at[pl.ds(r, FFN_SUB)], sem)

        tail0 = lax.shift_right_logical(pad0_ref[N_EXPERTS], FFN_SUB.bit_length() - 1)
        n_tail = xs_out.shape[0] // FFN_SUB
        for e in range(N_EXPERTS):
            lax.fori_loop(0, padn_ref[e], lambda i, c: (pad_copy(e, i).start(), c)[1], 0)
        lax.fori_loop(tail0, n_tail, lambda j, c: (tail_copy(j).start(), c)[1], 0)
        for e in range(N_EXPERTS):
            lax.fori_loop(0, padn_ref[e], lambda i, c: (pad_copy(e, i).wait(), c)[1], 0)
        lax.fori_loop(tail0, n_tail, lambda j, c: (tail_copy(j).wait(), c)[1], 0)

    def copy(g, u, pos_ref):
        r = pl.multiple_of(g * SUBLANES, SUBLANES) + u
        return pltpu.make_async_copy(hp_ref.at[pl.ds(r, 1)], xs_out.at[pl.ds(pos_ref[base + r], 1)], sem)

    def start(g, c):
        for u in range(SUBLANES):
            copy(g, u, p1_ref).start()
            copy(g, u, p2_ref).start()
        return c

    def wait(g, c):
        for u in range(SUBLANES):
            copy(g, u, p1_ref).wait()
            copy(g, u, p2_ref).wait()
        return c

    lax.fori_loop(0, tm // SUBLANES, start, 0)
    lax.fori_loop(0, tm // SUBLANES, wait, 0)


def dispatch(pos1, pos2, pad_start, pad_len, hp, tm=512):
    grid_spec = pltpu.PrefetchScalarGridSpec(
        num_scalar_prefetch=4,
        grid=(N_TOK // tm,),
        in_specs=[pl.BlockSpec((tm, HALF_D), lambda i, *_: (i, 0))],
        out_specs=pl.BlockSpec(memory_space=pl.ANY),
        scratch_shapes=[pltpu.VMEM((FFN_SUB, HALF_D), jnp.uint32), pltpu.SemaphoreType.DMA(())],
    )
    return pl.pallas_call(
        functools.partial(_dispatch_kernel, tm=tm),
        grid_spec=grid_spec,
        out_shape=jax.ShapeDtypeStruct((MOE_ROWS, HALF_D), jnp.uint32),
        compiler_params=_cparams(("arbitrary",)),
        name="dispatch",
    )(pos1, pos2, pad_start, pad_len, hp)


def _combine_kernel(p1_ref, p2_ref, x_ref, mf_ref, y_hbm, o_ref, buf, sem, *, tm):
    i = pl.program_id(0)

    def for_tile(t, fn):
        slot = t & 1
        base = t * tm

        def body(g, c):
            for u in range(SUBLANES):
                r = pl.multiple_of(g * SUBLANES, SUBLANES) + u
                for k, pos_ref in enumerate((p1_ref, p2_ref)):
                    fn(pltpu.make_async_copy(y_hbm.at[pl.ds(pos_ref[base + r], 1)],
                                             buf.at[slot, k, pl.ds(r, 1)], sem.at[slot]))
            return c
        lax.fori_loop(0, tm // SUBLANES, body, 0)

    @pl.when(i == 0)
    def _():
        for_tile(0, lambda c: c.start())

    @pl.when(i + 1 < pl.num_programs(0))
    def _():
        for_tile(i + 1, lambda c: c.start())

    for_tile(i, lambda c: c.wait())
    w = mf_ref[...]
    slot = i & 1
    o_ref[...] = x_ref[...] + w[:, 0:1] * buf[slot, 0] + w[:, 1:2] * buf[slot, 1]


def combine(pos1, pos2, x, mf, y, tm=256):
    grid_spec = pltpu.PrefetchScalarGridSpec(
        num_scalar_prefetch=2,
        grid=(N_TOK // tm,),
        in_specs=[pl.BlockSpec((tm, D_MODEL), lambda i, p1, p2: (i, 0)),
                  pl.BlockSpec((tm, LANES), lambda i, p1, p2: (i, 0)),
                  pl.BlockSpec(memory_space=pl.ANY)],
        out_specs=pl.BlockSpec((tm, D_MODEL), lambda i, p1, p2: (i, 0)),
        scratch_shapes=[pltpu.VMEM((2, TOP_K, tm, D_MODEL), f32), pltpu.SemaphoreType.DMA((2,))],
    )
    return pl.pallas_call(
        functools.partial(_combine_kernel, tm=tm),
        grid_spec=grid_spec,
        out_shape=jax.ShapeDtypeStruct((N_TOK, D_MODEL), f32),
        compiler_params=_cparams(("arbitrary",)),
        name="combine",
    )(pos1, pos2, x, mf, y)


def _rope_lanes(v):
    z = jnp.zeros(v.shape[:-1] + (QK_ROPE // 2,), v.dtype)
    return jnp.concatenate([v[..., :QK_ROPE // 2], z, v[..., QK_ROPE // 2:], z], axis=-1)


def kernel(x, mem, positions, g_mix, g_ffn, g_mem, w_mem_kv, g_mq, g_mk, w_out, a_w_in, a_conv_w, b_w_in,
           b_g_q_a, b_w_q_b, b_g_qn, g_kv, w_kv_a, g_kv_a, w_kv_b, g_kn, ffn_w_gate, ffn_w_up, ffn_w_down,
           moe_w_router, moe_w_gate, moe_w_up, moe_w_down):
    row = lambda v: v.reshape(1, -1).astype(f32)
    x0 = x.reshape(N_TOK, D_MODEL)
    mem2 = mem.reshape(BATCH * N_MEM, D_MODEL)

    w_kva = jnp.concatenate([w_kv_a[:, :KV_LORA], _rope_lanes(w_kv_a[:, KV_LORA:])], axis=1).astype(bf16)
    kvb = w_kv_b.reshape(KV_LORA, N_HEADS, QK_NOPE + HEAD_DIM)
    w_kvb = jnp.concatenate([kvb[:, :, :QK_NOPE].reshape(KV_LORA, MIX_WIDTH),
                             kvb[:, :, QK_NOPE:].reshape(KV_LORA, MIX_WIDTH)], axis=1).astype(bf16)
    qb = b_w_q_b[0].reshape(Q_LORA, N_HEADS, QK_HEAD)
    w_qb = jnp.concatenate([qb[:, :, :QK_NOPE], _rope_lanes(qb[:, :, QK_NOPE:])], axis=-1)
    w_qb = w_qb.reshape(Q_LORA, N_HEADS * HEAD_PAD).astype(bf16)
    gk_n, gk_r = row(g_kn[:QK_NOPE]), row(_rope_lanes(g_kn[QK_NOPE:]))
    gq_n, gq_r = row(b_g_qn[0, :QK_NOPE]), row(_rope_lanes(b_g_qn[0, QK_NOPE:]))
    w_router_pad = jnp.pad(moe_w_router[0].astype(f32), ((0, 0), (0, LANES - N_EXPERTS)))

    inv_freq = 1.0 / (ROPE_THETA ** (jnp.arange(0, QK_ROPE, 2, dtype=f32) / QK_ROPE))
    half = jnp.ones((QK_ROPE // 2,), f32)
    rope_freq = row(_rope_lanes(jnp.concatenate([inv_freq, inv_freq])))
    rope_sign = row(_rope_lanes(jnp.concatenate([-half, half])))
    pos = positions.reshape(N_TOK, 1).astype(jnp.int32)

    u0 = norm_matmul(x0, row(g_mix[0]), a_w_in[0], tm=1024, tn=1280)
    kvm0 = norm_matmul(mem2, row(g_mem[0]), w_mem_kv[0], tm=BATCH * N_MEM, tn=2 * MEM_WIDTH)
    xa, hp0 = conv_tail(u0, a_conv_w[0].astype(f32), kvm0, row(g_mq[0]), row(g_mk[0]),
                        w_out[0].astype(bf16), x0, row(g_ffn[0]))
    dense_tbl = _super_tiles(jnp.array([N_TOK], jnp.int32), jnp.array([0], jnp.int32),
                             -(-N_TOK // (FFN_SUB * FFN_SUPER)))
    y0 = swiglu_ffn(hp0, ffn_w_gate, ffn_w_up, ffn_w_down, *dense_tbl)

    x1, k, v, q, um1 = mla_kvq(xa, y0, pos, rope_freq, rope_sign, row(g_kv), w_kva, row(g_kv_a), w_kvb, gk_n, gk_r,
                               row(g_mix[1]), b_w_in[0].astype(bf16), row(b_g_q_a[0]), w_qb, gq_n, gq_r)
    mix1 = flash_attention(q, k, v)
    kvm1 = norm_matmul(mem2, row(g_mem[1]), w_mem_kv[1], tm=BATCH * N_MEM, tn=2 * MEM_WIDTH)
    x2, hp1 = attn_tail(mix1, um1, 0, kvm1, row(g_mq[1]), row(g_mk[1]), w_out[1].astype(bf16), x1, row(g_ffn[1]))

    mi, mf, cnt = router(x2, row(g_ffn[1]), w_router_pad)
    counts = cnt[0, :N_EXPERTS]
    padded = (counts + FFN_SUB - 1) // FFN_SUB * FFN_SUB
    offsets = jnp.cumsum(padded) - padded
    pos1 = offsets[mi[0]] + mi[2]
    pos2 = offsets[mi[1]] + mi[3]
    pad_start = jnp.concatenate([offsets + counts, jnp.sum(padded, keepdims=True)])
    xs = dispatch(pos1, pos2, pad_start, padded - counts, hp1)
    y1 = swiglu_ffn(xs, moe_w_gate[0], moe_w_up[0], moe_w_down[0], *_super_tiles(padded, offsets, MOE_SUPERS))
    out = combine(pos1, pos2, x2, mf, y1)
    return out.reshape(BATCH, SEQ, D_MODEL)
```

```python
import functools

import jax
import jax.numpy as jnp
from jax import lax
from jax.experimental import pallas as pl
from jax.experimental.pallas import tpu as pltpu

D_MODEL = 2048
BATCH = 2
SEQ = 4096
N_TOK = BATCH * SEQ
N_HEADS = 12
HEAD_DIM = 128
MIX_WIDTH = N_HEADS * HEAD_DIM
MEM_HEADS = 4
MEM_WIDTH = MEM_HEADS * HEAD_DIM
N_MEM = 256
CONV_WIDTH = 3
Q_LORA = 512
KV_LORA = 256
QK_NOPE = 128
QK_ROPE = 64
QK_HEAD = QK_NOPE + QK_ROPE
ROPE_THETA = 10000.0
D_FF = 7168
N_EXPERTS = 8
TOP_K = 2
EPS = 1e-6

LANES = 128
SUBLANES = 8
HEAD_PAD = 2 * LANES
HALF_D = D_MODEL // 2
VMEM_LIMIT = 56 * 1024 * 1024

FFN_SUB = 256
FFN_SUPER = 9
FFN_TF = 256
MOE_ROWS = N_TOK * TOP_K + N_EXPERTS * FFN_SUB
MOE_SUPERS = N_EXPERTS + -(-MOE_ROWS // (FFN_SUB * FFN_SUPER))

f32 = jnp.float32
bf16 = jnp.bfloat16


def _cparams(sem):
    return pltpu.CompilerParams(dimension_semantics=sem, vmem_limit_bytes=VMEM_LIMIT)


def _rms(x, g, n=None):
    n = x.shape[-1] if n is None else n
    ms = jnp.sum(x * x, axis=-1, keepdims=True) * (1.0 / n)
    return x * lax.rsqrt(ms + EPS) * g


def _pack_halves(h):
    hb = h.astype(bf16).astype(f32)
    return pltpu.pack_elementwise([hb[:, :HALF_D], hb[:, HALF_D:]], packed_dtype=bf16)


def _unpack_halves(p):
    lo = pltpu.unpack_elementwise(p, index=0, packed_dtype=bf16, unpacked_dtype=f32)
    hi = pltpu.unpack_elementwise(p, index=1, packed_dtype=bf16, unpacked_dtype=f32)
    return lo.astype(bf16), hi.astype(bf16)


def _dot(a, b):
    return jnp.dot(a, b, preferred_element_type=f32)


def _dot_nt(a, b):
    return lax.dot_general(a, b, (((1,), (1,)), ((), ())), preferred_element_type=f32)


def _norm_matmul_kernel(x_ref, g_ref, w_ref, o_ref, h_ref):
    @pl.when(pl.program_id(1) == 0)
    def _():
        h_ref[...] = _rms(x_ref[...], g_ref[...]).astype(bf16)

    o_ref[...] = _dot(h_ref[...], w_ref[...].astype(bf16)).astype(o_ref.dtype)


def norm_matmul(x, g, w, tm, tn, out_dtype=bf16):
    m, d = x.shape
    n = w.shape[1]
    return pl.pallas_call(
        _norm_matmul_kernel,
        grid=(m // tm, n // tn),
        in_specs=[pl.BlockSpec((tm, d), lambda i, j: (i, 0)),
                  pl.BlockSpec((1, d), lambda i, j: (0, 0)),
                  pl.BlockSpec((d, tn), lambda i, j: (0, j))],
        out_specs=pl.BlockSpec((tm, tn), lambda i, j: (i, j)),
        out_shape=jax.ShapeDtypeStruct((m, n), out_dtype),
        scratch_shapes=[pltpu.VMEM((tm, d), bf16)],
        compiler_params=_cparams(("arbitrary", "arbitrary")),
        name="norm_matmul",
    )(x, g, w)


def _mem_attention(um_ref, kvm_ref, gq_ref, gk_ref, cat_ref):
    scale = HEAD_DIM ** -0.5
    for h in range(MEM_HEADS):
        cs = slice(h * HEAD_DIM, (h + 1) * HEAD_DIM)
        q = _rms(um_ref[:, cs].astype(f32), gq_ref[...]).astype(bf16)
        k = _rms(kvm_ref[:, cs].astype(f32), gk_ref[...]).astype(bf16)
        v = kvm_ref[:, MEM_WIDTH + h * HEAD_DIM:MEM_WIDTH + (h + 1) * HEAD_DIM]
        s = _dot_nt(q, k) * scale
        p = jnp.exp(s - jnp.max(s, axis=-1, keepdims=True))
        o = _dot(p.astype(bf16), v) / jnp.sum(p, axis=-1, keepdims=True)
        cat_ref[:, MIX_WIDTH + h * HEAD_DIM:MIX_WIDTH + (h + 1) * HEAD_DIM] = o.astype(bf16)


def _tail_epilogue(cat_ref, wout_ref, x_ref, gffn_ref, xo_ref, hp_ref):
    xn = x_ref[...] + _dot(cat_ref[...], wout_ref[...])
    xo_ref[...] = xn
    hp_ref[...] = _pack_halves(_rms(xn, gffn_ref[...]))


def _conv_tail_kernel(xin_ref, gb_ref, gc_ref, um_ref, hx_ref, hg_ref, cw_ref, kvm_ref, gq_ref, gk_ref,
                      wout_ref, x_ref, gffn_ref, xo_ref, hp_ref, cat_ref, *, tm):
    i = pl.program_id(0)
    first = (i * tm) % SEQ == 0
    rows = lax.broadcasted_iota(jnp.int32, (tm, 1), 0)
    cc = 512
    for c in range(MIX_WIDTH // cc):
        cs = slice(c * cc, (c + 1) * cc)
        z = gc_ref[:, cs].astype(f32) * xin_ref[:, cs].astype(f32)
        zh = hg_ref[:, cs].astype(f32) * hx_ref[:, cs].astype(f32)
        zh = jnp.where(first, 0.0, zh)
        p1 = zh[15:16, :]
        p2 = zh[14:15, :]
        z1 = jnp.where(rows == 0, p1, pltpu.roll(z, 1, 0))
        z2 = jnp.where(rows == 0, p2, jnp.where(rows == 1, p1, pltpu.roll(z, 2, 0)))
        y = cw_ref[0:1, cs] * z2 + cw_ref[1:2, cs] * z1 + cw_ref[2:3, cs] * z
        cat_ref[:, cs] = (gb_ref[:, cs].astype(f32) * y).astype(bf16)
    _mem_attention(um_ref, kvm_ref, gq_ref, gk_ref, cat_ref)
    _tail_epilogue(cat_ref, wout_ref, x_ref, gffn_ref, xo_ref, hp_ref)


def _attn_tail_kernel(mix_ref, um_ref, kvm_ref, gq_ref, gk_ref, wout_ref, x_ref, gffn_ref,
                      xo_ref, hp_ref, cat_ref):
    cat_ref[:, :MIX_WIDTH] = mix_ref[...]
    _mem_attention(um_ref, kvm_ref, gq_ref, gk_ref, cat_ref)
    _tail_epilogue(cat_ref, wout_ref, x_ref, gffn_ref, xo_ref, hp_ref)


def _tail_common_specs(tm):
    per_seq = SEQ // tm
    return [pl.BlockSpec((N_MEM, 2 * MEM_WIDTH), lambda i: (i // per_seq, 0)),
            pl.BlockSpec((1, HEAD_DIM), lambda i: (0, 0)),
            pl.BlockSpec((1, HEAD_DIM), lambda i: (0, 0)),
            pl.BlockSpec((D_MODEL, D_MODEL), lambda i: (0, 0), pipeline_mode=pl.Buffered(1)),
            pl.BlockSpec((tm, D_MODEL), lambda i: (i, 0)),
            pl.BlockSpec((1, D_MODEL), lambda i: (0, 0))]


def _tail_outs(tm):
    return dict(
        out_specs=[pl.BlockSpec((tm, D_MODEL), lambda i: (i, 0)),
                   pl.BlockSpec((tm, HALF_D), lambda i: (i, 0))],
        out_shape=[jax.ShapeDtypeStruct((N_TOK, D_MODEL), f32),
                   jax.ShapeDtypeStruct((N_TOK, HALF_D), jnp.uint32)],
        scratch_shapes=[pltpu.VMEM((tm, D_MODEL), bf16)],
        compiler_params=_cparams(("arbitrary",)),
    )


def conv_tail(u, conv_w, kvm, g_mq, g_mk, w_out, x, g_ffn, tm=512):
    halo = 16
    um_blk = 3 * MIX_WIDTH // MEM_WIDTH
    prev = lambda i: jnp.maximum(i * (tm // halo) - 1, 0)
    in_specs = [pl.BlockSpec((tm, MIX_WIDTH), lambda i: (i, 0)),
                pl.BlockSpec((tm, MIX_WIDTH), lambda i: (i, 1)),
                pl.BlockSpec((tm, MIX_WIDTH), lambda i: (i, 2)),
                pl.BlockSpec((tm, MEM_WIDTH), lambda i: (i, um_blk)),
                pl.BlockSpec((halo, MIX_WIDTH), lambda i: (prev(i), 0)),
                pl.BlockSpec((halo, MIX_WIDTH), lambda i: (prev(i), 2)),
                pl.BlockSpec((CONV_WIDTH, MIX_WIDTH), lambda i: (0, 0))] + _tail_common_specs(tm)
    return pl.pallas_call(
        functools.partial(_conv_tail_kernel, tm=tm),
        grid=(N_TOK // tm,), in_specs=in_specs, name="conv_tail", **_tail_outs(tm),
    )(u, u, u, u, u, u, conv_w, kvm, g_mq, g_mk, w_out, x, g_ffn)


def attn_tail(mix, um_src, um_blk, kvm, g_mq, g_mk, w_out, x, g_ffn, tm=512):
    in_specs = [pl.BlockSpec((tm, MIX_WIDTH), lambda i: (i, 0)),
                pl.BlockSpec((tm, MEM_WIDTH), lambda i: (i, um_blk))] + _tail_common_specs(tm)
    return pl.pallas_call(
        _attn_tail_kernel,
        grid=(N_TOK // tm,), in_specs=in_specs, name="attn_tail", **_tail_outs(tm),
    )(mix, um_src, kvm, g_mq, g_mk, w_out, x, g_ffn)


def _ffn_kernel(exp_ref, row0_ref, nsub_ref, tail_ref, xs_hbm, wg_ref, wu_ref, wd_ref, y_hbm,
                x_s, acc, wg_s, wu_s, wd_s, sem):
    s = pl.program_id(0)
    f = pl.program_id(1)
    nsub = nsub_ref[s]
    row0 = row0_ref[s]
    last_f = f == pl.num_programs(1) - 1
    last_step = (s == pl.num_programs(0) - 1) & last_f

    def x_copy(j):
        r = pl.multiple_of(j * FFN_SUB, FFN_SUB)
        g = pl.multiple_of(row0 + r, FFN_SUB)
        return pltpu.make_async_copy(xs_hbm.at[pl.ds(g, FFN_SUB)], x_s.at[pl.ds(r, FFN_SUB)], sem)

    def y_copy(r, rows):
        g = pl.multiple_of(row0 + r, FFN_SUB)
        return pltpu.make_async_copy(acc.at[pl.ds(r, rows)], y_hbm.at[pl.ds(g, rows)], sem)

    def for_subtiles(fn):
        def body(j, c):
            fn(j)
            return c
        lax.fori_loop(0, nsub, body, 0)

    @pl.when(f == 0)
    def _():
        for_subtiles(lambda j: x_copy(j).start())

        def zero(j):
            r = pl.multiple_of(j * FFN_SUB, FFN_SUB)
            acc[pl.ds(r, FFN_SUB), :] = jnp.zeros((FFN_SUB, D_MODEL), f32)
        for_subtiles(zero)
        for_subtiles(lambda j: x_copy(j).wait())

    def swiglu_rows(r, rows, wg, wu, wd):
        lo, hi = _unpack_halves(x_s[pl.ds(r, rows), :])
        g = _dot(lo, wg[:HALF_D, :]) + _dot(hi, wg[HALF_D:, :])
        u = _dot(lo, wu[:HALF_D, :]) + _dot(hi, wu[HALF_D:, :])
        a = (g * jax.nn.sigmoid(g) * u).astype(bf16)
        acc[pl.ds(r, rows), :] += _dot(a, wd[...])

        @pl.when(last_f)
        def _():
            y_copy(r, rows).start()

    big_first = nsub >= 4
    n_first = jnp.where(big_first, 4, 1)
    rem = nsub - n_first
    n_quad = lax.shift_right_logical(rem, 2)
    first_rows = pl.multiple_of(n_first * FFN_SUB, FFN_SUB)
    quad_row = lambda j: pl.multiple_of(first_rows + j * 4 * FFN_SUB, FFN_SUB)
    pair_row = quad_row(n_quad)
    last_row = pl.multiple_of((nsub - 1) * FFN_SUB, FFN_SUB)

    def for_groups(first, fn):
        @pl.when(big_first)
        def _():
            first(4 * FFN_SUB)

        @pl.when(jnp.logical_not(big_first))
        def _():
            first(FFN_SUB)

        def body(j, c):
            fn(quad_row(j), 4 * FFN_SUB)
            return c
        lax.fori_loop(0, n_quad, body, 0)

        @pl.when((rem & 2) != 0)
        def _():
            fn(pair_row, 2 * FFN_SUB)

        @pl.when((rem & 1) != 0)
        def _():
            fn(last_row, FFN_SUB)

    def first_group(rows):
        wg = wg_ref[0].astype(bf16)
        wu = wu_ref[0].astype(bf16)
        wd = wd_ref[0].astype(bf16)
        wg_s[...] = wg
        wu_s[...] = wu
        wd_s[...] = wd
        swiglu_rows(0, rows, wg, wu, wd)

    for_groups(first_group, lambda r, rows: swiglu_rows(r, rows, wg_s, wu_s, wd_s))

    @pl.when(last_f)
    def _():
        for_groups(lambda rows: y_copy(0, rows).wait(), lambda r, rows: y_copy(r, rows).wait())

    @pl.when(last_step)
    def _():
        acc[pl.ds(0, FFN_SUB), :] = jnp.zeros((FFN_SUB, D_MODEL), f32)

        def tail_copy(j):
            g = pl.multiple_of(j * FFN_SUB, FFN_SUB)
            return pltpu.make_async_copy(acc.at[pl.ds(0, FFN_SUB)], y_hbm.at[pl.ds(g, FFN_SUB)], sem)

        n_tail = y_hbm.shape[0] // FFN_SUB
        lax.fori_loop(tail_ref[0], n_tail, lambda j, c: (tail_copy(j).start(), c)[1], 0)
        lax.fori_loop(tail_ref[0], n_tail, lambda j, c: (tail_copy(j).wait(), c)[1], 0)


def swiglu_ffn(xs, w_gate, w_up, w_down, st_expert, st_row0, st_nsub, st_tail):
    rows = xs.shape[0]
    nf = D_FF // FFN_TF

    grid_spec = pltpu.PrefetchScalarGridSpec(
        num_scalar_prefetch=4,
        grid=(st_tail[1], nf),
        in_specs=[pl.BlockSpec(memory_space=pl.ANY),
                  pl.BlockSpec((1, D_MODEL, FFN_TF), lambda s, f, e, r, n, t: (e[s], 0, f)),
                  pl.BlockSpec((1, D_MODEL, FFN_TF), lambda s, f, e, r, n, t: (e[s], 0, f)),
                  pl.BlockSpec((1, FFN_TF, D_MODEL), lambda s, f, e, r, n, t: (e[s], f, 0))],
        out_specs=pl.BlockSpec(memory_space=pl.ANY),
        scratch_shapes=[pltpu.VMEM((FFN_SUPER * FFN_SUB, HALF_D), jnp.uint32),
                        pltpu.VMEM((FFN_SUPER * FFN_SUB, D_MODEL), f32),
                        pltpu.VMEM((D_MODEL, FFN_TF), bf16),
                        pltpu.VMEM((D_MODEL, FFN_TF), bf16),
                        pltpu.VMEM((FFN_TF, D_MODEL), bf16),
                        pltpu.SemaphoreType.DMA(())],
    )
    return pl.pallas_call(
        _ffn_kernel,
        grid_spec=grid_spec,
        out_shape=jax.ShapeDtypeStruct((rows, D_MODEL), f32),
        compiler_params=_cparams(("arbitrary", "arbitrary")),
        name="swiglu_ffn",
    )(st_expert, st_row0, st_nsub, st_tail, xs, w_gate, w_up, w_down)


def _super_tiles(padded, offsets, n_super):
    cap = FFN_SUB * FFN_SUPER
    per_group = (padded + cap - 1) // cap
    ends = jnp.cumsum(per_group)
    t = jnp.arange(n_super, dtype=jnp.int32)
    grp = jnp.minimum(jnp.sum((t[:, None] >= ends[None, :]).astype(jnp.int32), axis=1), padded.shape[0] - 1)
    k = t - (ends[grp] - per_group[grp])
    used = t < ends[-1]
    nsub = jnp.where(used, jnp.minimum(FFN_SUPER, (padded[grp] - k * cap) // FFN_SUB), 0)
    row0 = jnp.where(used, offsets[grp] + k * cap, 0)
    last_grp = grp[jnp.maximum(ends[-1] - 1, 0)]
    grp = jnp.where(used, grp, last_grp)
    tail = jnp.stack([jnp.sum(padded) // FFN_SUB, ends[-1]])
    return grp.astype(jnp.int32), row0.astype(jnp.int32), nsub.astype(jnp.int32), tail.astype(jnp.int32)


def _rope(x, cos, sin):
    return x * cos + pltpu.roll(x, LANES // 2, 1) * sin


def _dot_split(a, sel):
    hi = a.astype(bf16)
    lo = (a - hi.astype(f32)).astype(bf16)
    return _dot(hi, sel) + _dot(lo, sel)


def _mla_kvq_kernel(x0_ref, y_ref, pos_ref, freq_ref, sign_ref, gkv_ref, wkva_ref, gkva_ref, wkvb_ref, gkn_ref, gkr_ref,
                    gmix_ref, win_ref, gqa_ref, wqb_ref, gqn_ref, gqr_ref, sel_ref,
                    x1_ref, k_ref, v_ref, q_ref, um_ref):
    x1 = x0_ref[...] + y_ref[...]
    x1_ref[...] = x1
    ang = pos_ref[...].astype(f32) * freq_ref[...]
    cos = jnp.cos(ang)
    sin = jnp.sin(ang) * sign_ref[...]

    tm = x1.shape[0]
    kva = _dot(_rms(x1, gkv_ref[...]).astype(bf16), wkva_ref[...])
    u = _dot(_rms(x1, gmix_ref[...]).astype(bf16), win_ref[...])
    um_ref[...] = u[:, Q_LORA:].astype(bf16)
    c_kv = _rms(kva[:, :KV_LORA], gkva_ref[...]).astype(bf16)
    kv = _dot(c_kv, wkvb_ref[...])
    c_q = _rms(u[:, :Q_LORA], gqa_ref[...]).astype(bf16)
    q = _dot(c_q, wqb_ref[...])
    v_ref[...] = kv[:, MIX_WIDTH:].astype(bf16)

    kpe = kva[:, KV_LORA:]
    ss_pe = jnp.sum(kpe * kpe, axis=-1, keepdims=True)
    sck = lax.rsqrt((_dot_split(kv[:, :MIX_WIDTH] * kv[:, :MIX_WIDTH], sel_ref[...]) + ss_pe) * (1.0 / QK_HEAD) + EPS)
    q2 = jnp.concatenate([q[:, h * HEAD_PAD:h * HEAD_PAD + LANES] * q[:, h * HEAD_PAD:h * HEAD_PAD + LANES]
                          + q[:, h * HEAD_PAD + LANES:(h + 1) * HEAD_PAD] * q[:, h * HEAD_PAD + LANES:(h + 1) * HEAD_PAD]
                          for h in range(N_HEADS)], axis=-1)
    scq = lax.rsqrt(_dot_split(q2, sel_ref[...]) * (1.0 / QK_HEAD) + EPS)
    k_rot = _rope(kpe * gkr_ref[...], cos, sin)
    for h in range(N_HEADS):
        sch = jnp.broadcast_to(sck[:, h:h + 1], (tm, LANES))
        k_ref[:, h * HEAD_PAD:h * HEAD_PAD + LANES] = (
            kv[:, h * HEAD_DIM:(h + 1) * HEAD_DIM] * sch * gkn_ref[...]).astype(bf16)
        k_ref[:, h * HEAD_PAD + LANES:(h + 1) * HEAD_PAD] = (k_rot * sch).astype(bf16)
        sch = jnp.broadcast_to(scq[:, h:h + 1], (tm, LANES))
        qn = q[:, h * HEAD_PAD:h * HEAD_PAD + LANES]
        qr = q[:, h * HEAD_PAD + LANES:(h + 1) * HEAD_PAD]
        q_ref[:, h * HEAD_PAD:h * HEAD_PAD + LANES] = (qn * sch * gqn_ref[...]).astype(bf16)
        q_ref[:, h * HEAD_PAD + LANES:(h + 1) * HEAD_PAD] = (_rope(qr * gqr_ref[...], cos, sin) * sch).astype(bf16)


def mla_kvq(x0, y, pos, rope_freq, rope_sign, g_kv, w_kva, g_kva, w_kvb, gk_n, gk_r, g_mix, w_in, g_qa, w_qb, gq_n, gq_r, tm=256):
    row = lambda w: pl.BlockSpec((tm, w), lambda i: (i, 0))
    full = lambda a: pl.BlockSpec(a.shape, lambda i: (0, 0))
    head = jnp.arange(MIX_WIDTH, dtype=jnp.int32) // HEAD_DIM
    sel = (head[:, None] == jnp.arange(LANES, dtype=jnp.int32)[None, :]).astype(bf16)
    consts = (rope_freq, rope_sign, g_kv, w_kva, g_kva, w_kvb, gk_n, gk_r, g_mix, w_in, g_qa, w_qb, gq_n, gq_r, sel)
    return pl.pallas_call(
        _mla_kvq_kernel,
        grid=(N_TOK // tm,),
        in_specs=[row(D_MODEL), row(D_MODEL), row(1)] + [full(a) for a in consts],
        out_specs=[row(D_MODEL), row(N_HEADS * HEAD_PAD), row(MIX_WIDTH), row(N_HEADS * HEAD_PAD), row(MEM_WIDTH)],
        out_shape=[jax.ShapeDtypeStruct((N_TOK, D_MODEL), f32),
                   jax.ShapeDtypeStruct((N_TOK, N_HEADS * HEAD_PAD), bf16),
                   jax.ShapeDtypeStruct((N_TOK, MIX_WIDTH), bf16),
                   jax.ShapeDtypeStruct((N_TOK, N_HEADS * HEAD_PAD), bf16),
                   jax.ShapeDtypeStruct((N_TOK, MEM_WIDTH), bf16)],
        compiler_params=_cparams(("arbitrary",)),
        name="mla_kvq",
    )(x0, y, pos, *consts)


def _flash_kernel(q_ref, k_ref, v_ref, o_ref, m_s, l_s, acc_s, *, tq, heads):
    qi = pl.program_id(2)
    c_exp = (QK_HEAD ** -0.5) * 1.4426950408889634
    m_s[...] = jnp.full(m_s.shape, -jnp.inf, f32)
    l_s[...] = jnp.zeros(l_s.shape, f32)
    acc_s[...] = jnp.zeros(acc_s.shape, f32)
    n_blk = tq // LANES

    def chunk(kj, masked):
        r = pl.multiple_of(kj * tq, tq)

        def scores(h):
            q = q_ref[:, h * HEAD_PAD:(h + 1) * HEAD_PAD]
            s = _dot_nt(q, k_ref[pl.ds(r, tq), h * HEAD_PAD:(h + 1) * HEAD_PAD])
            if masked:
                qpos = lax.broadcasted_iota(jnp.int32, (tq, tq), 0)
                kpos = lax.broadcasted_iota(jnp.int32, (tq, tq), 1)
                s = jnp.where(kpos <= qpos, s, -jnp.inf)
            return s

        def update(h, s):
            blocks = [s[:, b * LANES:(b + 1) * LANES] for b in range(n_blk)]
            lane_max = functools.reduce(jnp.maximum, blocks)
            m_old = m_s[h]
            m_new = jnp.maximum(m_old, jnp.max(lane_max, axis=-1, keepdims=True))
            alpha = jnp.exp2((m_old - m_new) * c_exp)
            p = [jnp.exp2((blk - m_new) * c_exp) for blk in blocks]
            l_s[h] = alpha * l_s[h] + functools.reduce(jnp.add, p)
            pv = _dot(jnp.concatenate(p, axis=-1).astype(bf16),
                      v_ref[pl.ds(r, tq), h * HEAD_DIM:(h + 1) * HEAD_DIM])
            acc_s[h] = alpha * acc_s[h] + pv
            m_s[h] = m_new

        ahead = 1
        pending = [scores(h) for h in range(min(ahead, heads))]
        for h in range(heads):
            if h + ahead < heads:
                pending.append(scores(h + ahead))
            update(h, pending[h])

    def body(kj, c):
        chunk(kj, False)
        return c

    lax.fori_loop(0, qi, body, 0)
    chunk(qi, True)
    for h in range(heads):
        l = jnp.sum(l_s[h], axis=-1, keepdims=True)
        o_ref[:, h * HEAD_DIM:(h + 1) * HEAD_DIM] = (acc_s[h] / l).astype(o_ref.dtype)


def flash_attention(q, k, v, tq=512, heads=6):
    per_seq = SEQ // tq
    return pl.pallas_call(
        functools.partial(_flash_kernel, tq=tq, heads=heads),
        grid=(BATCH, N_HEADS // heads, per_seq),
        in_specs=[pl.BlockSpec((tq, heads * HEAD_PAD), lambda b, h, i: (b * per_seq + i, h)),
                  pl.BlockSpec((SEQ, heads * HEAD_PAD), lambda b, h, i: (b, h)),
                  pl.BlockSpec((SEQ, heads * HEAD_DIM), lambda b, h, i: (b, h))],
        out_specs=pl.BlockSpec((tq, heads * HEAD_DIM), lambda b, h, i: (b * per_seq + i, h)),
        out_shape=jax.ShapeDtypeStruct((N_TOK, MIX_WIDTH), bf16),
        scratch_shapes=[pltpu.VMEM((heads, tq, LANES), f32),
                        pltpu.VMEM((heads, tq, LANES), f32),
                        pltpu.VMEM((heads, tq, HEAD_DIM), f32)],
        compiler_params=_cparams(("arbitrary", "arbitrary", "arbitrary")),
        name="flash_attention",
    )(q, k, v)


def _router_kernel(x_ref, g_ref, wr_ref, mi_ref, mf_ref, cnt_ref, carry, *, tm):
    i = pl.program_id(0)

    @pl.when(i == 0)
    def _():
        carry[...] = jnp.zeros_like(carry)

    h = _rms(x_ref[...], g_ref[...])
    w = wr_ref[...]
    h_hi, w_hi = h.astype(bf16), w.astype(bf16)
    h_lo, w_lo = (h - h_hi.astype(f32)).astype(bf16), (w - w_hi.astype(f32)).astype(bf16)
    logits = _dot(h_hi, w_hi) + (_dot(h_hi, w_lo) + _dot(h_lo, w_hi))
    lane = lax.broadcasted_iota(jnp.int32, (tm, LANES), 1)
    logits = jnp.where(lane < N_EXPERTS, logits, -jnp.inf)
    lane_f = lane.astype(f32)
    v1 = jnp.max(logits, axis=-1, keepdims=True)
    e1 = jnp.min(jnp.where(logits == v1, lane_f, float(LANES)), axis=-1, keepdims=True).astype(jnp.int32)
    rest = jnp.where(lane == e1, -jnp.inf, logits)
    v2 = jnp.max(rest, axis=-1, keepdims=True)
    e2 = jnp.min(jnp.where(rest == v2, lane_f, float(LANES)), axis=-1, keepdims=True).astype(jnp.int32)
    t = jnp.exp(v2 - v1)
    w1 = 1.0 / (1.0 + t)
    w2 = t / (1.0 + t)

    hot = jnp.where((lane == e1) | (lane == e2), 1.0, 0.0)
    r_io = lax.broadcasted_iota(jnp.int32, (tm, tm), 0)
    c_io = lax.broadcasted_iota(jnp.int32, (tm, tm), 1)
    below = jnp.where(c_io < r_io, 1.0, 0.0).astype(bf16)
    rank = _dot(below, hot.astype(bf16)) + carry[...]
    carry[...] += jnp.sum(hot, axis=0, keepdims=True)
    rank1 = jnp.sum(jnp.where(lane == e1, rank, 0.0), axis=-1, keepdims=True).astype(jnp.int32)
    rank2 = jnp.sum(jnp.where(lane == e2, rank, 0.0), axis=-1, keepdims=True).astype(jnp.int32)

    meta = jnp.where(lane == 0, e1, jnp.where(lane == 1, e2, jnp.where(lane == 2, rank1,
                     jnp.where(lane == 3, rank2, 0))))
    mi_ref[...] = meta.T[:8, :]
    mf_ref[...] = jnp.where(lane == 0, w1, jnp.where(lane == 1, w2, 0.0))
    cnt_ref[...] = carry[...].astype(jnp.int32)


def router(x, g, w_router_pad, tm=512):
    return pl.pallas_call(
        functools.partial(_router_kernel, tm=tm),
        grid=(N_TOK // tm,),
        in_specs=[pl.BlockSpec((tm, D_MODEL), lambda i: (i, 0)),
                  pl.BlockSpec((1, D_MODEL), lambda i: (0, 0)),
                  pl.BlockSpec((D_MODEL, LANES), lambda i: (0, 0))],
        out_specs=[pl.BlockSpec((8, tm), lambda i: (0, i)),
                   pl.BlockSpec((tm, LANES), lambda i: (i, 0)),
                   pl.BlockSpec((1, LANES), lambda i: (0, 0))],
        out_shape=[jax.ShapeDtypeStruct((8, N_TOK), jnp.int32),
                   jax.ShapeDtypeStruct((N_TOK, LANES), f32),
                   jax.ShapeDtypeStruct((1, LANES), jnp.int32)],
        scratch_shapes=[pltpu.VMEM((1, LANES), f32)],
        compiler_params=_cparams(("arbitrary",)),
        name="router",
    )(x, g, w_router_pad)


def _dispatch_kernel(p1_ref, p2_ref, pad0_ref, padn_ref, hp_ref, xs_out, zrow, sem, *, tm):
    base = pl.program_id(0) * tm

    @pl.when(pl.program_id(0) == 0)
    def _():
        zrow[...] = jnp.zeros(zrow.shape, zrow.dtype)

        def pad_copy(e, i):
            return pltpu.make_async_copy(zrow.at[pl.ds(0, 1)], xs_out.at[pl.ds(pad0_ref[e] + i, 1)], sem)

        def tail_copy(j):
            r = pl.multiple_of(j * FFN_SUB, FFN_SUB)
            return pltpu.make_async_copy(zrow, xs_out.at[pl.ds(r, FFN_SUB)], sem)

        tail0 = lax.shift_right_logical(pad0_ref[N_EXPERTS], FFN_SUB.bit_length() - 1)
        n_tail = xs_out.shape[0] // FFN_SUB
        for e in range(N_EXPERTS):
            lax.fori_loop(0, padn_ref[e], lambda i, c: (pad_copy(e, i).start(), c)[1], 0)
        lax.fori_loop(tail0, n_tail, lambda j, c: (tail_copy(j).start(), c)[1], 0)
        for e in range(N_EXPERTS):
            lax.fori_loop(0, padn_ref[e], lambda i, c: (pad_copy(e, i).wait(), c)[1], 0)
        lax.fori_loop(tail0, n_tail, lambda j, c: (tail_copy(j).wait(), c)[1], 0)

    def copy(g, u, pos_ref):
        r = pl.multiple_of(g * SUBLANES, SUBLANES) + u
        return pltpu.make_async_copy(hp_ref.at[pl.ds(r, 1)], xs_out.at[pl.ds(pos_ref[base + r], 1)], sem)

    def start(g, c):
        for u in range(SUBLANES):
            copy(g, u, p1_ref).start()
            copy(g, u, p2_ref).start()
        return c

    def wait(g, c):
        for u in range(SUBLANES):
            copy(g, u, p1_ref).wait()
            copy(g, u, p2_ref).wait()
        return c

    lax.fori_loop(0, tm // SUBLANES, start, 0)
    lax.fori_loop(0, tm // SUBLANES, wait, 0)


def dispatch(pos1, pos2, pad_start, pad_len, hp, tm=512):
    grid_spec = pltpu.PrefetchScalarGridSpec(
        num_scalar_prefetch=4,
        grid=(N_TOK // tm,),
        in_specs=[pl.BlockSpec((tm, HALF_D), lambda i, *_: (i, 0))],
        out_specs=pl.BlockSpec(memory_space=pl.ANY),
        scratch_shapes=[pltpu.VMEM((FFN_SUB, HALF_D), jnp.uint32), pltpu.SemaphoreType.DMA(())],
    )
    return pl.pallas_call(
        functools.partial(_dispatch_kernel, tm=tm),
        grid_spec=grid_spec,
        out_shape=jax.ShapeDtypeStruct((MOE_ROWS, HALF_D), jnp.uint32),
        compiler_params=_cparams(("arbitrary",)),
        name="dispatch",
    )(pos1, pos2, pad_start, pad_len, hp)


def _combine_kernel(p1_ref, p2_ref, x_ref, mf_ref, y_hbm, o_ref, buf, sem, *, tm):
    i = pl.program_id(0)

    def for_tile(t, fn):
        slot = t & 1
        base = t * tm

        def body(g, c):
            for u in range(SUBLANES):
                r = pl.multiple_of(g * SUBLANES, SUBLANES) + u
                for k, pos_ref in enumerate((p1_ref, p2_ref)):
                    fn(pltpu.make_async_copy(y_hbm.at[pl.ds(pos_ref[base + r], 1)],
                                             buf.at[slot, k, pl.ds(r, 1)], sem.at[slot]))
            return c
        lax.fori_loop(0, tm // SUBLANES, body, 0)

    @pl.when(i == 0)
    def _():
        for_tile(0, lambda c: c.start())

    @pl.when(i + 1 < pl.num_programs(0))
    def _():
        for_tile(i + 1, lambda c: c.start())

    for_tile(i, lambda c: c.wait())
    w = mf_ref[...]
    slot = i & 1
    o_ref[...] = x_ref[...] + w[:, 0:1] * buf[slot, 0] + w[:, 1:2] * buf[slot, 1]


def combine(pos1, pos2, x, mf, y, tm=256):
    grid_spec = pltpu.PrefetchScalarGridSpec(
        num_scalar_prefetch=2,
        grid=(N_TOK // tm,),
        in_specs=[pl.BlockSpec((tm, D_MODEL), lambda i, p1, p2: (i, 0)),
                  pl.BlockSpec((tm, LANES), lambda i, p1, p2: (i, 0)),
                  pl.BlockSpec(memory_space=pl.ANY)],
        out_specs=pl.BlockSpec((tm, D_MODEL), lambda i, p1, p2: (i, 0)),
        scratch_shapes=[pltpu.VMEM((2, TOP_K, tm, D_MODEL), f32), pltpu.SemaphoreType.DMA((2,))],
    )
    return pl.pallas_call(
        functools.partial(_combine_kernel, tm=tm),
        grid_spec=grid_spec,
        out_shape=jax.ShapeDtypeStruct((N_TOK, D_MODEL), f32),
        compiler_params=_cparams(("arbitrary",)),
        name="combine",
    )(pos1, pos2, x, mf, y)


def _rope_lanes(v):
    z = jnp.zeros(v.shape[:-1] + (QK_ROPE // 2,), v.dtype)
    return jnp.concatenate([v[..., :QK_ROPE // 2], z, v[..., QK_ROPE // 2:], z], axis=-1)


def kernel(x, mem, positions, g_mix, g_ffn, g_mem, w_mem_kv, g_mq, g_mk, w_out, a_w_in, a_conv_w, b_w_in,
           b_g_q_a, b_w_q_b, b_g_qn, g_kv, w_kv_a, g_kv_a, w_kv_b, g_kn, ffn_w_gate, ffn_w_up, ffn_w_down,
           moe_w_router, moe_w_gate, moe_w_up, moe_w_down):
    row = lambda v: v.reshape(1, -1).astype(f32)
    x0 = x.reshape(N_TOK, D_MODEL)
    mem2 = mem.reshape(BATCH * N_MEM, D_MODEL)

    w_kva = jnp.concatenate([w_kv_a[:, :KV_LORA], _rope_lanes(w_kv_a[:, KV_LORA:])], axis=1).astype(bf16)
    kvb = w_kv_b.reshape(KV_LORA, N_HEADS, QK_NOPE + HEAD_DIM)
    w_kvb = jnp.concatenate([kvb[:, :, :QK_NOPE].reshape(KV_LORA, MIX_WIDTH),
                             kvb[:, :, QK_NOPE:].reshape(KV_LORA, MIX_WIDTH)], axis=1).astype(bf16)
    qb = b_w_q_b[0].reshape(Q_LORA, N_HEADS, QK_HEAD)
    w_qb = jnp.concatenate([qb[:, :, :QK_NOPE], _rope_lanes(qb[:, :, QK_NOPE:])], axis=-1)
    w_qb = w_qb.reshape(Q_LORA, N_HEADS * HEAD_PAD).astype(bf16)
    gk_n, gk_r = row(g_kn[:QK_NOPE]), row(_rope_lanes(g_kn[QK_NOPE:]))
    gq_n, gq_r = row(b_g_qn[0, :QK_NOPE]), row(_rope_lanes(b_g_qn[0, QK_NOPE:]))
    w_router_pad = jnp.pad(moe_w_router[0].astype(f32), ((0, 0), (0, LANES - N_EXPERTS)))

    inv_freq = 1.0 / (ROPE_THETA ** (jnp.arange(0, QK_ROPE, 2, dtype=f32) / QK_ROPE))
    half = jnp.ones((QK_ROPE // 2,), f32)
    rope_freq = row(_rope_lanes(jnp.concatenate([inv_freq, inv_freq])))
    rope_sign = row(_rope_lanes(jnp.concatenate([-half, half])))
    pos = positions.reshape(N_TOK, 1).astype(jnp.int32)

    u0 = norm_matmul(x0, row(g_mix[0]), a_w_in[0], tm=1024, tn=1280)
    kvm0 = norm_matmul(mem2, row(g_mem[0]), w_mem_kv[0], tm=BATCH * N_MEM, tn=2 * MEM_WIDTH)
    xa, hp0 = conv_tail(u0, a_conv_w[0].astype(f32), kvm0, row(g_mq[0]), row(g_mk[0]),
                        w_out[0].astype(bf16), x0, row(g_ffn[0]))
    dense_tbl = _super_tiles(jnp.array([N_TOK], jnp.int32), jnp.array([0], jnp.int32),
                             -(-N_TOK // (FFN_SUB * FFN_SUPER)))
    y0 = swiglu_ffn(hp0, ffn_w_gate, ffn_w_up, ffn_w_down, *dense_tbl)

    x1, k, v, q, um1 = mla_kvq(xa, y0, pos, rope_freq, rope_sign, row(g_kv), w_kva, row(g_kv_a), w_kvb, gk_n, gk_r,
                               row(g_mix[1]), b_w_in[0].astype(bf16), row(b_g_q_a[0]), w_qb, gq_n, gq_r)
    mix1 = flash_attention(q, k, v)
    kvm1 = norm_matmul(mem2, row(g_mem[1]), w_mem_kv[1], tm=BATCH * N_MEM, tn=2 * MEM_WIDTH)
    x2, hp1 = attn_tail(mix1, um1, 0, kvm1, row(g_mq[1]), row(g_mk[1]), w_out[1].astype(bf16), x1, row(g_ffn[1]))

    mi, mf, cnt = router(x2, row(g_ffn[1]), w_router_pad)
    counts = cnt[0, :N_EXPERTS]
    padded = (counts + FFN_SUB - 1) // FFN_SUB * FFN_SUB
    offsets = jnp.cumsum(padded) - padded
    pos1 = offsets[mi[0]] + mi[2]
    pos2 = offsets[mi[1]] + mi[3]
    pad_start = jnp.concatenate([offsets + counts, jnp.sum(padded, keepdims=True)])
    xs = dispatch(pos1, pos2, pad_start, padded - counts, hp1)
    y1 = swiglu_ffn(xs, moe_w_gate[0], moe_w_up[0], moe_w_down[0], *_super_tiles(padded, offsets, MOE_SUPERS))
    out = combine(pos1, pos2, x2, mf, y1)
    return out.reshape(BATCH, SEQ, D_MODEL)
```

```python
import functools

import jax
import jax.numpy as jnp
from jax import lax
from jax.experimental import pallas as pl
from jax.experimental.pallas import tpu as pltpu

D_MODEL = 2048
BATCH = 2
SEQ = 4096
N_TOK = BATCH * SEQ
N_HEADS = 12
HEAD_DIM = 128
MIX_WIDTH = N_HEADS * HEAD_DIM
MEM_HEADS = 4
MEM_WIDTH = MEM_HEADS * HEAD_DIM
N_MEM = 256
CONV_WIDTH = 3
Q_LORA = 512
KV_LORA = 256
QK_NOPE = 128
QK_ROPE = 64
QK_HEAD = QK_NOPE + QK_ROPE
ROPE_THETA = 10000.0
D_FF = 7168
N_EXPERTS = 8
TOP_K = 2
EPS = 1e-6

LANES = 128
SUBLANES = 8
HEAD_PAD = 2 * LANES
HALF_D = D_MODEL // 2
VMEM_LIMIT = 56 * 1024 * 1024

FFN_SUB = 256
FFN_SUPER = 9
FFN_TF = 256
MOE_ROWS = N_TOK * TOP_K + N_EXPERTS * FFN_SUB
MOE_SUPERS = N_EXPERTS + -(-MOE_ROWS // (FFN_SUB * FFN_SUPER))

f32 = jnp.float32
bf16 = jnp.bfloat16


def _cparams(sem):
    return pltpu.CompilerParams(dimension_semantics=sem, vmem_limit_bytes=VMEM_LIMIT)


def _rms(x, g, n=None):
    n = x.shape[-1] if n is None else n
    ms = jnp.sum(x * x, axis=-1, keepdims=True) * (1.0 / n)
    return x * lax.rsqrt(ms + EPS) * g


def _pack_halves(h):
    hb = h.astype(bf16).astype(f32)
    return pltpu.pack_elementwise([hb[:, :HALF_D], hb[:, HALF_D:]], packed_dtype=bf16)


def _unpack_halves(p):
    lo = pltpu.unpack_elementwise(p, index=0, packed_dtype=bf16, unpacked_dtype=f32)
    hi = pltpu.unpack_elementwise(p, index=1, packed_dtype=bf16, unpacked_dtype=f32)
    return lo.astype(bf16), hi.astype(bf16)


def _dot(a, b):
    return jnp.dot(a, b, preferred_element_type=f32)


def _dot_nt(a, b):
    return lax.dot_general(a, b, (((1,), (1,)), ((), ())), preferred_element_type=f32)


def _norm_matmul_kernel(x_ref, g_ref, w_ref, o_ref, h_ref):
    @pl.when(pl.program_id(1) == 0)
    def _():
        h_ref[...] = _rms(x_ref[...], g_ref[...]).astype(bf16)

    o_ref[...] = _dot(h_ref[...], w_ref[...].astype(bf16)).astype(o_ref.dtype)


def norm_matmul(x, g, w, tm, tn, out_dtype=bf16):
    m, d = x.shape
    n = w.shape[1]
    return pl.pallas_call(
        _norm_matmul_kernel,
        grid=(m // tm, n // tn),
        in_specs=[pl.BlockSpec((tm, d), lambda i, j: (i, 0)),
                  pl.BlockSpec((1, d), lambda i, j: (0, 0)),
                  pl.BlockSpec((d, tn), lambda i, j: (0, j))],
        out_specs=pl.BlockSpec((tm, tn), lambda i, j: (i, j)),
        out_shape=jax.ShapeDtypeStruct((m, n), out_dtype),
        scratch_shapes=[pltpu.VMEM((tm, d), bf16)],
        compiler_params=_cparams(("arbitrary", "arbitrary")),
        name="norm_matmul",
    )(x, g, w)


def _mem_attention(um_ref, kvm_ref, gq_ref, gk_ref, cat_ref):
    scale = HEAD_DIM ** -0.5
    for h in range(MEM_HEADS):
        cs = slice(h * HEAD_DIM, (h + 1) * HEAD_DIM)
        q = _rms(um_ref[:, cs].astype(f32), gq_ref[...]).astype(bf16)
        k = _rms(kvm_ref[:, cs].astype(f32), gk_ref[...]).astype(bf16)
        v = kvm_ref[:, MEM_WIDTH + h * HEAD_DIM:MEM_WIDTH + (h + 1) * HEAD_DIM]
        s = _dot_nt(q, k) * scale
        p = jnp.exp(s - jnp.max(s, axis=-1, keepdims=True))
        o = _dot(p.astype(bf16), v) / jnp.sum(p, axis=-1, keepdims=True)
        cat_ref[:, MIX_WIDTH + h * HEAD_DIM:MIX_WIDTH + (h + 1) * HEAD_DIM] = o.astype(bf16)


def _tail_epilogue(cat_ref, wout_ref, x_ref, gffn_ref, xo_ref, hp_ref):
    xn = x_ref[...] + _dot(cat_ref[...], wout_ref[...])
    xo_ref[...] = xn
    hp_ref[...] = _pack_halves(_rms(xn, gffn_ref[...]))


def _conv_tail_kernel(xin_ref, gb_ref, gc_ref, um_ref, hx_ref, hg_ref, cw_ref, kvm_ref, gq_ref, gk_ref,
                      wout_ref, x_ref, gffn_ref, xo_ref, hp_ref, cat_ref, *, tm):
    i = pl.program_id(0)
    first = (i * tm) % SEQ == 0
    rows = lax.broadcasted_iota(jnp.int32, (tm, 1), 0)
    cc = 512
    for c in range(MIX_WIDTH // cc):
        cs = slice(c * cc, (c + 1) * cc)
        z = gc_ref[:, cs].astype(f32) * xin_ref[:, cs].astype(f32)
        zh = hg_ref[:, cs].astype(f32) * hx_ref[:, cs].astype(f32)
        zh = jnp.where(first, 0.0, zh)
        p1 = zh[15:16, :]
        p2 = zh[14:15, :]
        z1 = jnp.where(rows == 0, p1, pltpu.roll(z, 1, 0))
        z2 = jnp.where(rows == 0, p2, jnp.where(rows == 1, p1, pltpu.roll(z, 2, 0)))
        y = cw_ref[0:1, cs] * z2 + cw_ref[1:2, cs] * z1 + cw_ref[2:3, cs] * z
        cat_ref[:, cs] = (gb_ref[:, cs].astype(f32) * y).astype(bf16)
    _mem_attention(um_ref, kvm_ref, gq_ref, gk_ref, cat_ref)
    _tail_epilogue(cat_ref, wout_ref, x_ref, gffn_ref, xo_ref, hp_ref)


def _attn_tail_kernel(mix_ref, um_ref, kvm_ref, gq_ref, gk_ref, wout_ref, x_ref, gffn_ref,
                      xo_ref, hp_ref, cat_ref):
    cat_ref[:, :MIX_WIDTH] = mix_ref[...]
    _mem_attention(um_ref, kvm_ref, gq_ref, gk_ref, cat_ref)
    _tail_epilogue(cat_ref, wout_ref, x_ref, gffn_ref, xo_ref, hp_ref)


def _tail_common_specs(tm):
    per_seq = SEQ // tm
    return [pl.BlockSpec((N_MEM, 2 * MEM_WIDTH), lambda i: (i // per_seq, 0)),
            pl.BlockSpec((1, HEAD_DIM), lambda i: (0, 0)),
            pl.BlockSpec((1, HEAD_DIM), lambda i: (0, 0)),
            pl.BlockSpec((D_MODEL, D_MODEL), lambda i: (0, 0), pipeline_mode=pl.Buffered(1)),
            pl.BlockSpec((tm, D_MODEL), lambda i: (i, 0)),
            pl.BlockSpec((1, D_MODEL), lambda i: (0, 0))]


def _tail_outs(tm):
    return dict(
        out_specs=[pl.BlockSpec((tm, D_MODEL), lambda i: (i, 0)),
                   pl.BlockSpec((tm, HALF_D), lambda i: (i, 0))],
        out_shape=[jax.ShapeDtypeStruct((N_TOK, D_MODEL), f32),
                   jax.ShapeDtypeStruct((N_TOK, HALF_D), jnp.uint32)],
        scratch_shapes=[pltpu.VMEM((tm, D_MODEL), bf16)],
        compiler_params=_cparams(("arbitrary",)),
    )


def conv_tail(u, conv_w, kvm, g_mq, g_mk, w_out, x, g_ffn, tm=512):
    halo = 16
    um_blk = 3 * MIX_WIDTH // MEM_WIDTH
    prev = lambda i: jnp.maximum(i * (tm // halo) - 1, 0)
    in_specs = [pl.BlockSpec((tm, MIX_WIDTH), lambda i: (i, 0)),
                pl.BlockSpec((tm, MIX_WIDTH), lambda i: (i, 1)),
                pl.BlockSpec((tm, MIX_WIDTH), lambda i: (i, 2)),
                pl.BlockSpec((tm, MEM_WIDTH), lambda i: (i, um_blk)),
                pl.BlockSpec((halo, MIX_WIDTH), lambda i: (prev(i), 0)),
                pl.BlockSpec((halo, MIX_WIDTH), lambda i: (prev(i), 2)),
                pl.BlockSpec((CONV_WIDTH, MIX_WIDTH), lambda i: (0, 0))] + _tail_common_specs(tm)
    return pl.pallas_call(
        functools.partial(_conv_tail_kernel, tm=tm),
        grid=(N_TOK // tm,), in_specs=in_specs, name="conv_tail", **_tail_outs(tm),
    )(u, u, u, u, u, u, conv_w, kvm, g_mq, g_mk, w_out, x, g_ffn)


def attn_tail(mix, um_src, um_blk, kvm, g_mq, g_mk, w_out, x, g_ffn, tm=512):
    in_specs = [pl.BlockSpec((tm, MIX_WIDTH), lambda i: (i, 0)),
                pl.BlockSpec((tm, MEM_WIDTH), lambda i: (i, um_blk))] + _tail_common_specs(tm)
    return pl.pallas_call(
        _attn_tail_kernel,
        grid=(N_TOK // tm,), in_specs=in_specs, name="attn_tail", **_tail_outs(tm),
    )(mix, um_src, kvm, g_mq, g_mk, w_out, x, g_ffn)


def _ffn_kernel(exp_ref, row0_ref, nsub_ref, tail_ref, xs_hbm, wg_ref, wu_ref, wd_ref, y_hbm,
                x_s, acc, wg_s, wu_s, wd_s, sem):
    s = pl.program_id(0)
    f = pl.program_id(1)
    nsub = nsub_ref[s]
    row0 = row0_ref[s]
    last_f = f == pl.num_programs(1) - 1
    last_step = (s == pl.num_programs(0) - 1) & last_f

    def x_copy(j):
        r = pl.multiple_of(j * FFN_SUB, FFN_SUB)
        g = pl.multiple_of(row0 + r, FFN_SUB)
        return pltpu.make_async_copy(xs_hbm.at[pl.ds(g, FFN_SUB)], x_s.at[pl.ds(r, FFN_SUB)], sem)

    def y_copy(r, rows):
        g = pl.multiple_of(row0 + r, FFN_SUB)
        return pltpu.make_async_copy(acc.at[pl.ds(r, rows)], y_hbm.at[pl.ds(g, rows)], sem)

    def for_subtiles(fn):
        def body(j, c):
            fn(j)
            return c
        lax.fori_loop(0, nsub, body, 0)

    @pl.when(f == 0)
    def _():
        for_subtiles(lambda j: x_copy(j).start())

        def zero(j):
            r = pl.multiple_of(j * FFN_SUB, FFN_SUB)
            acc[pl.ds(r, FFN_SUB), :] = jnp.zeros((FFN_SUB, D_MODEL), f32)
        for_subtiles(zero)
        for_subtiles(lambda j: x_copy(j).wait())

    def swiglu_rows(r, rows, wg, wu, wd):
        lo, hi = _unpack_halves(x_s[pl.ds(r, rows), :])
        g = _dot(lo, wg[:HALF_D, :]) + _dot(hi, wg[HALF_D:, :])
        u = _dot(lo, wu[:HALF_D, :]) + _dot(hi, wu[HALF_D:, :])
        a = (g * jax.nn.sigmoid(g) * u).astype(bf16)
        acc[pl.ds(r, rows), :] += _dot(a, wd[...])

        @pl.when(last_f)
        def _():
            y_copy(r, rows).start()

    n_first = jnp.where(nsub >= 4, 4 + (nsub & 1), 1)
    rem = nsub - n_first
    n_quad = lax.shift_right_logical(rem, 2)
    first_rows = pl.multiple_of(n_first * FFN_SUB, FFN_SUB)
    quad_row = lambda j: pl.multiple_of(first_rows + j * 4 * FFN_SUB, FFN_SUB)
    pair_row = quad_row(n_quad)
    last_row = pl.multiple_of((nsub - 1) * FFN_SUB, FFN_SUB)

    def for_groups(first, fn):
        for n in (1, 4, 5):
            @pl.when(n_first == n)
            def _():
                first(n * FFN_SUB)

        def body(j, c):
            fn(quad_row(j), 4 * FFN_SUB)
            return c
        lax.fori_loop(0, n_quad, body, 0)

        @pl.when((rem & 2) != 0)
        def _():
            fn(pair_row, 2 * FFN_SUB)

        @pl.when((rem & 1) != 0)
        def _():
            fn(last_row, FFN_SUB)

    def first_group(rows):
        wg = wg_ref[0].astype(bf16)
        wu = wu_ref[0].astype(bf16)
        wd = wd_ref[0].astype(bf16)
        wg_s[...] = wg
        wu_s[...] = wu
        wd_s[...] = wd
        swiglu_rows(0, rows, wg, wu, wd)

    for_groups(first_group, lambda r, rows: swiglu_rows(r, rows, wg_s, wu_s, wd_s))

    @pl.when(last_f)
    def _():
        for_groups(lambda rows: y_copy(0, rows).wait(), lambda r, rows: y_copy(r, rows).wait())

    @pl.when(last_step)
    def _():
        acc[pl.ds(0, FFN_SUB), :] = jnp.zeros((FFN_SUB, D_MODEL), f32)

        def tail_copy(j):
            g = pl.multiple_of(j * FFN_SUB, FFN_SUB)
            return pltpu.make_async_copy(acc.at[pl.ds(0, FFN_SUB)], y_hbm.at[pl.ds(g, FFN_SUB)], sem)

        n_tail = y_hbm.shape[0] // FFN_SUB
        lax.fori_loop(tail_ref[0], n_tail, lambda j, c: (tail_copy(j).start(), c)[1], 0)
        lax.fori_loop(tail_ref[0], n_tail, lambda j, c: (tail_copy(j).wait(), c)[1], 0)


def swiglu_ffn(xs, w_gate, w_up, w_down, st_expert, st_row0, st_nsub, st_tail):
    rows = xs.shape[0]
    nf = D_FF // FFN_TF

    grid_spec = pltpu.PrefetchScalarGridSpec(
        num_scalar_prefetch=4,
        grid=(st_tail[1], nf),
        in_specs=[pl.BlockSpec(memory_space=pl.ANY),
                  pl.BlockSpec((1, D_MODEL, FFN_TF), lambda s, f, e, r, n, t: (e[s], 0, f)),
                  pl.BlockSpec((1, D_MODEL, FFN_TF), lambda s, f, e, r, n, t: (e[s], 0, f)),
                  pl.BlockSpec((1, FFN_TF, D_MODEL), lambda s, f, e, r, n, t: (e[s], f, 0))],
        out_specs=pl.BlockSpec(memory_space=pl.ANY),
        scratch_shapes=[pltpu.VMEM((FFN_SUPER * FFN_SUB, HALF_D), jnp.uint32),
                        pltpu.VMEM((FFN_SUPER * FFN_SUB, D_MODEL), f32),
                        pltpu.VMEM((D_MODEL, FFN_TF), bf16),
                        pltpu.VMEM((D_MODEL, FFN_TF), bf16),
                        pltpu.VMEM((FFN_TF, D_MODEL), bf16),
                        pltpu.SemaphoreType.DMA(())],
    )
    return pl.pallas_call(
        _ffn_kernel,
        grid_spec=grid_spec,
        out_shape=jax.ShapeDtypeStruct((rows, D_MODEL), f32),
        compiler_params=_cparams(("arbitrary", "arbitrary")),
        name="swiglu_ffn",
    )(st_expert, st_row0, st_nsub, st_tail, xs, w_gate, w_up, w_down)


def _super_tiles(padded, offsets, n_super):
    cap = FFN_SUB * FFN_SUPER
    per_group = (padded + cap - 1) // cap
    ends = jnp.cumsum(per_group)
    t = jnp.arange(n_super, dtype=jnp.int32)
    grp = jnp.minimum(jnp.sum((t[:, None] >= ends[None, :]).astype(jnp.int32), axis=1), padded.shape[0] - 1)
    k = t - (ends[grp] - per_group[grp])
    used = t < ends[-1]
    nsub = jnp.where(used, jnp.minimum(FFN_SUPER, (padded[grp] - k * cap) // FFN_SUB), 0)
    row0 = jnp.where(used, offsets[grp] + k * cap, 0)
    last_grp = grp[jnp.maximum(ends[-1] - 1, 0)]
    grp = jnp.where(used, grp, last_grp)
    tail = jnp.stack([jnp.sum(padded) // FFN_SUB, ends[-1]])
    return grp.astype(jnp.int32), row0.astype(jnp.int32), nsub.astype(jnp.int32), tail.astype(jnp.int32)


def _rope(x, cos, sin):
    return x * cos + pltpu.roll(x, LANES // 2, 1) * sin


def _dot_split(a, sel):
    hi = a.astype(bf16)
    lo = (a - hi.astype(f32)).astype(bf16)
    return _dot(hi, sel) + _dot(lo, sel)


def _mla_kvq_kernel(x0_ref, y_ref, pos_ref, freq_ref, sign_ref, gkv_ref, wkva_ref, gkva_ref, wkvb_ref, gkn_ref, gkr_ref,
                    gmix_ref, win_ref, gqa_ref, wqb_ref, gqn_ref, gqr_ref, sel_ref,
                    x1_ref, k_ref, v_ref, q_ref, um_ref):
    x1 = x0_ref[...] + y_ref[...]
    x1_ref[...] = x1
    ang = pos_ref[...].astype(f32) * freq_ref[...]
    cos = jnp.cos(ang)
    sin = jnp.sin(ang) * sign_ref[...]

    tm = x1.shape[0]
    kva = _dot(_rms(x1, gkv_ref[...]).astype(bf16), wkva_ref[...])
    u = _dot(_rms(x1, gmix_ref[...]).astype(bf16), win_ref[...])
    um_ref[...] = u[:, Q_LORA:].astype(bf16)
    c_kv = _rms(kva[:, :KV_LORA], gkva_ref[...]).astype(bf16)
    kv = _dot(c_kv, wkvb_ref[...])
    c_q = _rms(u[:, :Q_LORA], gqa_ref[...]).astype(bf16)
    q = _dot(c_q, wqb_ref[...])
    v_ref[...] = kv[:, MIX_WIDTH:].astype(bf16)

    kpe = kva[:, KV_LORA:]
    ss_pe = jnp.sum(kpe * kpe, axis=-1, keepdims=True)
    sck = lax.rsqrt((_dot_split(kv[:, :MIX_WIDTH] * kv[:, :MIX_WIDTH], sel_ref[...]) + ss_pe) * (1.0 / QK_HEAD) + EPS)
    q2 = jnp.concatenate([q[:, h * HEAD_PAD:h * HEAD_PAD + LANES] * q[:, h * HEAD_PAD:h * HEAD_PAD + LANES]
                          + q[:, h * HEAD_PAD + LANES:(h + 1) * HEAD_PAD] * q[:, h * HEAD_PAD + LANES:(h + 1) * HEAD_PAD]
                          for h in range(N_HEADS)], axis=-1)
    scq = lax.rsqrt(_dot_split(q2, sel_ref[...]) * (1.0 / QK_HEAD) + EPS)
    k_rot = _rope(kpe * gkr_ref[...], cos, sin)
    for h in range(N_HEADS):
        sch = jnp.broadcast_to(sck[:, h:h + 1], (tm, LANES))
        k_ref[:, h * HEAD_PAD:h * HEAD_PAD + LANES] = (
            kv[:, h * HEAD_DIM:(h + 1) * HEAD_DIM] * sch * gkn_ref[...]).astype(bf16)
        k_ref[:, h * HEAD_PAD + LANES:(h + 1) * HEAD_PAD] = (k_rot * sch).astype(bf16)
        sch = jnp.broadcast_to(scq[:, h:h + 1], (tm, LANES))
        qn = q[:, h * HEAD_PAD:h * HEAD_PAD + LANES]
        qr = q[:, h * HEAD_PAD + LANES:(h + 1) * HEAD_PAD]
        q_ref[:, h * HEAD_PAD:h * HEAD_PAD + LANES] = (qn * sch * gqn_ref[...]).astype(bf16)
        q_ref[:, h * HEAD_PAD + LANES:(h + 1) * HEAD_PAD] = (_rope(qr * gqr_ref[...], cos, sin) * sch).astype(bf16)


def mla_kvq(x0, y, pos, rope_freq, rope_sign, g_kv, w_kva, g_kva, w_kvb, gk_n, gk_r, g_mix, w_in, g_qa, w_qb, gq_n, gq_r, tm=256):
    row = lambda w: pl.BlockSpec((tm, w), lambda i: (i, 0))
    full = lambda a: pl.BlockSpec(a.shape, lambda i: (0, 0))
    head = jnp.arange(MIX_WIDTH, dtype=jnp.int32) // HEAD_DIM
    sel = (head[:, None] == jnp.arange(LANES, dtype=jnp.int32)[None, :]).astype(bf16)
    consts = (rope_freq, rope_sign, g_kv, w_kva, g_kva, w_kvb, gk_n, gk_r, g_mix, w_in, g_qa, w_qb, gq_n, gq_r, sel)
    return pl.pallas_call(
        _mla_kvq_kernel,
        grid=(N_TOK // tm,),
        in_specs=[row(D_MODEL), row(D_MODEL), row(1)] + [full(a) for a in consts],
        out_specs=[row(D_MODEL), row(N_HEADS * HEAD_PAD), row(MIX_WIDTH), row(N_HEADS * HEAD_PAD), row(MEM_WIDTH)],
        out_shape=[jax.ShapeDtypeStruct((N_TOK, D_MODEL), f32),
                   jax.ShapeDtypeStruct((N_TOK, N_HEADS * HEAD_PAD), bf16),
                   jax.ShapeDtypeStruct((N_TOK, MIX_WIDTH), bf16),
                   jax.ShapeDtypeStruct((N_TOK, N_HEADS * HEAD_PAD), bf16),
                   jax.ShapeDtypeStruct((N_TOK, MEM_WIDTH), bf16)],
        compiler_params=_cparams(("arbitrary",)),
        name="mla_kvq",
    )(x0, y, pos, *consts)


def _flash_kernel(q_ref, k_ref, v_ref, o_ref, m_s, l_s, acc_s, *, tq, heads):
    qi = pl.program_id(2)
    c_exp = (QK_HEAD ** -0.5) * 1.4426950408889634
    m_s[...] = jnp.full(m_s.shape, -jnp.inf, f32)
    l_s[...] = jnp.zeros(l_s.shape, f32)
    acc_s[...] = jnp.zeros(acc_s.shape, f32)
    n_blk = tq // LANES

    def chunk(kj, masked):
        r = pl.multiple_of(kj * tq, tq)

        def scores(h):
            q = q_ref[:, h * HEAD_PAD:(h + 1) * HEAD_PAD]
            s = _dot_nt(q, k_ref[pl.ds(r, tq), h * HEAD_PAD:(h + 1) * HEAD_PAD])
            if masked:
                qpos = lax.broadcasted_iota(jnp.int32, (tq, tq), 0)
                kpos = lax.broadcasted_iota(jnp.int32, (tq, tq), 1)
                s = jnp.where(kpos <= qpos, s, -jnp.inf)
            return s

        def update(h, s):
            blocks = [s[:, b * LANES:(b + 1) * LANES] for b in range(n_blk)]
            lane_max = functools.reduce(jnp.maximum, blocks)
            m_old = m_s[h]
            m_new = jnp.maximum(m_old, jnp.max(lane_max, axis=-1, keepdims=True))
            alpha = jnp.exp2((m_old - m_new) * c_exp)
            p = [jnp.exp2((blk - m_new) * c_exp) for blk in blocks]
            l_s[h] = alpha * l_s[h] + functools.reduce(jnp.add, p)
            pv = _dot(jnp.concatenate(p, axis=-1).astype(bf16),
                      v_ref[pl.ds(r, tq), h * HEAD_DIM:(h + 1) * HEAD_DIM])
            acc_s[h] = alpha * acc_s[h] + pv
            m_s[h] = m_new

        ahead = 1
        pending = [scores(h) for h in range(min(ahead, heads))]
        for h in range(heads):
            if h + ahead < heads:
                pending.append(scores(h + ahead))
            update(h, pending[h])

    def body(kj, c):
        chunk(kj, False)
        return c

    lax.fori_loop(0, qi, body, 0)
    chunk(qi, True)
    for h in range(heads):
        l = jnp.sum(l_s[h], axis=-1, keepdims=True)
        o_ref[:, h * HEAD_DIM:(h + 1) * HEAD_DIM] = (acc_s[h] / l).astype(o_ref.dtype)


def flash_attention(q, k, v, tq=512, heads=6):
    per_seq = SEQ // tq
    return pl.pallas_call(
        functools.partial(_flash_kernel, tq=tq, heads=heads),
        grid=(BATCH, N_HEADS // heads, per_seq),
        in_specs=[pl.BlockSpec((tq, heads * HEAD_PAD), lambda b, h, i: (b * per_seq + i, h)),
                  pl.BlockSpec((SEQ, heads * HEAD_PAD), lambda b, h, i: (b, h)),
                  pl.BlockSpec((SEQ, heads * HEAD_DIM), lambda b, h, i: (b, h))],
        out_specs=pl.BlockSpec((tq, heads * HEAD_DIM), lambda b, h, i: (b * per_seq + i, h)),
        out_shape=jax.ShapeDtypeStruct((N_TOK, MIX_WIDTH), bf16),
        scratch_shapes=[pltpu.VMEM((heads, tq, LANES), f32),
                        pltpu.VMEM((heads, tq, LANES), f32),
                        pltpu.VMEM((heads, tq, HEAD_DIM), f32)],
        compiler_params=_cparams(("arbitrary", "arbitrary", "arbitrary")),
        name="flash_attention",
    )(q, k, v)


def _router_kernel(x_ref, g_ref, wr_ref, mi_ref, mf_ref, cnt_ref, carry, *, tm):
    i = pl.program_id(0)

    @pl.when(i == 0)
    def _():
        carry[...] = jnp.zeros_like(carry)

    h = _rms(x_ref[...], g_ref[...])
    w = wr_ref[...]
    h_hi, w_hi = h.astype(bf16), w.astype(bf16)
    h_lo, w_lo = (h - h_hi.astype(f32)).astype(bf16), (w - w_hi.astype(f32)).astype(bf16)
    logits = _dot(h_hi, w_hi) + (_dot(h_hi, w_lo) + _dot(h_lo, w_hi))
    lane = lax.broadcasted_iota(jnp.int32, (tm, LANES), 1)
    logits = jnp.where(lane < N_EXPERTS, logits, -jnp.inf)
    lane_f = lane.astype(f32)
    v1 = jnp.max(logits, axis=-1, keepdims=True)
    e1 = jnp.min(jnp.where(logits == v1, lane_f, float(LANES)), axis=-1, keepdims=True).astype(jnp.int32)
    rest = jnp.where(lane == e1, -jnp.inf, logits)
    v2 = jnp.max(rest, axis=-1, keepdims=True)
    e2 = jnp.min(jnp.where(rest == v2, lane_f, float(LANES)), axis=-1, keepdims=True).astype(jnp.int32)
    t = jnp.exp(v2 - v1)
    w1 = 1.0 / (1.0 + t)
    w2 = t / (1.0 + t)

    hot = jnp.where((lane == e1) | (lane == e2), 1.0, 0.0)
    r_io = lax.broadcasted_iota(jnp.int32, (tm, tm), 0)
    c_io = lax.broadcasted_iota(jnp.int32, (tm, tm), 1)
    below = jnp.where(c_io < r_io, 1.0, 0.0).astype(bf16)
    rank = _dot(below, hot.astype(bf16)) + carry[...]
    carry[...] += jnp.sum(hot, axis=0, keepdims=True)
    rank1 = jnp.sum(jnp.where(lane == e1, rank, 0.0), axis=-1, keepdims=True).astype(jnp.int32)
    rank2 = jnp.sum(jnp.where(lane == e2, rank, 0.0), axis=-1, keepdims=True).astype(jnp.int32)

    meta = jnp.where(lane == 0, e1, jnp.where(lane == 1, e2, jnp.where(lane == 2, rank1,
                     jnp.where(lane == 3, rank2, 0))))
    mi_ref[...] = meta.T[:8, :]
    mf_ref[...] = jnp.where(lane == 0, w1, jnp.where(lane == 1, w2, 0.0))
    cnt_ref[...] = carry[...].astype(jnp.int32)


def router(x, g, w_router_pad, tm=512):
    return pl.pallas_call(
        functools.partial(_router_kernel, tm=tm),
        grid=(N_TOK // tm,),
        in_specs=[pl.BlockSpec((tm, D_MODEL), lambda i: (i, 0)),
                  pl.BlockSpec((1, D_MODEL), lambda i: (0, 0)),
                  pl.BlockSpec((D_MODEL, LANES), lambda i: (0, 0))],
        out_specs=[pl.BlockSpec((8, tm), lambda i: (0, i)),
                   pl.BlockSpec((tm, LANES), lambda i: (i, 0)),
                   pl.BlockSpec((1, LANES), lambda i: (0, 0))],
        out_shape=[jax.ShapeDtypeStruct((8, N_TOK), jnp.int32),
                   jax.ShapeDtypeStruct((N_TOK, LANES), f32),
                   jax.ShapeDtypeStruct((1, LANES), jnp.int32)],
        scratch_shapes=[pltpu.VMEM((1, LANES), f32)],
        compiler_params=_cparams(("arbitrary",)),
        name="router",
    )(x, g, w_router_pad)


def _dispatch_kernel(p1_ref, p2_ref, pad0_ref, padn_ref, hp_ref, xs_out, zrow, sem, *, tm):
    base = pl.program_id(0) * tm

    @pl.when(pl.program_id(0) == 0)
    def _():
        zrow[...] = jnp.zeros(zrow.shape, zrow.dtype)

        def pad_copy(e, i):
            return pltpu.make_async_copy(zrow.at[pl.ds(0, 1)], xs_out.at[pl.ds(pad0_ref[e] + i, 1)], sem)

        def tail_copy(j):
            r = pl.multiple_of(j * FFN_SUB, FFN_SUB)
            return pltpu.make_async_copy(zrow, xs_out.at[pl.ds(r, FFN_SUB)], sem)

        tail0 = lax.shift_right_logical(pad0_ref[N_EXPERTS], FFN_SUB.bit_length() - 1)
        n_tail = xs_out.shape[0] // FFN_SUB
        for e in range(N_EXPERTS):
            lax.fori_loop(0, padn_ref[e], lambda i, c: (pad_copy(e, i).start(), c)[1], 0)
        lax.fori_loop(tail0, n_tail, lambda j, c: (tail_copy(j).start(), c)[1], 0)
        for e in range(N_EXPERTS):
            lax.fori_loop(0, padn_ref[e], lambda i, c: (pad_copy(e, i).wait(), c)[1], 0)
        lax.fori_loop(tail0, n_tail, lambda j, c: (tail_copy(j).wait(), c)[1], 0)

    def copy(g, u, pos_ref):
        r = pl.multiple_of(g * SUBLANES, SUBLANES) + u
        return pltpu.make_async_copy(hp_ref.at[pl.ds(r, 1)], xs_out.at[pl.ds(pos_ref[base + r], 1)], sem)

    def start(g, c):
        for u in range(SUBLANES):
            copy(g, u, p1_ref).start()
            copy(g, u, p2_ref).start()
        return c

    def wait(g, c):
        for u in range(SUBLANES):
            copy(g, u, p1_ref).wait()
            copy(g, u, p2_ref).wait()
        return c

    lax.fori_loop(0, tm // SUBLANES, start, 0)
    lax.fori_loop(0, tm // SUBLANES, wait, 0)


def dispatch(pos1, pos2, pad_start, pad_len, hp, tm=512):
    grid_spec = pltpu.PrefetchScalarGridSpec(
        num_scalar_prefetch=4,
        grid=(N_TOK // tm,),
        in_specs=[pl.BlockSpec((tm, HALF_D), lambda i, *_: (i, 0))],
        out_specs=pl.BlockSpec(memory_space=pl.ANY),
        scratch_shapes=[pltpu.VMEM((FFN_SUB, HALF_D), jnp.uint32), pltpu.SemaphoreType.DMA(())],
    )
    return pl.pallas_call(
        functools.partial(_dispatch_kernel, tm=tm),
        grid_spec=grid_spec,
        out_shape=jax.ShapeDtypeStruct((MOE_ROWS, HALF_D), jnp.uint32),
        compiler_params=_cparams(("arbitrary",)),
        name="dispatch",
    )(pos1, pos2, pad_start, pad_len, hp)


def _combine_kernel(p1_ref, p2_ref, x_ref, mf_ref, y_hbm, o_ref, buf, sem, *, tm):
    i = pl.program_id(0)

    def for_tile(t, fn):
        slot = t & 1
        base = t * tm

        def body(g, c):
            for u in range(SUBLANES):
                r = pl.multiple_of(g * SUBLANES, SUBLANES) + u
                for k, pos_ref in enumerate((p1_ref, p2_ref)):
                    fn(pltpu.make_async_copy(y_hbm.at[pl.ds(pos_ref[base + r], 1)],
                                             buf.at[slot, k, pl.ds(r, 1)], sem.at[slot]))
            return c
        lax.fori_loop(0, tm // SUBLANES, body, 0)

    @pl.when(i == 0)
    def _():
        for_tile(0, lambda c: c.start())

    @pl.when(i + 1 < pl.num_programs(0))
    def _():
        for_tile(i + 1, lambda c: c.start())

    for_tile(i, lambda c: c.wait())
    w = mf_ref[...]
    slot = i & 1
    o_ref[...] = x_ref[...] + w[:, 0:1] * buf[slot, 0] + w[:, 1:2] * buf[slot, 1]


def combine(pos1, pos2, x, mf, y, tm=256):
    grid_spec = pltpu.PrefetchScalarGridSpec(
        num_scalar_prefetch=2,
        grid=(N_TOK // tm,),
        in_specs=[pl.BlockSpec((tm, D_MODEL), lambda i, p1, p2: (i, 0)),
                  pl.BlockSpec((tm, LANES), lambda i, p1, p2: (i, 0)),
                  pl.BlockSpec(memory_space=pl.ANY)],
        out_specs=pl.BlockSpec((tm, D_MODEL), lambda i, p1, p2: (i, 0)),
        scratch_shapes=[pltpu.VMEM((2, TOP_K, tm, D_MODEL), f32), pltpu.SemaphoreType.DMA((2,))],
    )
    return pl.pallas_call(
        functools.partial(_combine_kernel, tm=tm),
        grid_spec=grid_spec,
        out_shape=jax.ShapeDtypeStruct((N_TOK, D_MODEL), f32),
        compiler_params=_cparams(("arbitrary",)),
        name="combine",
    )(pos1, pos2, x, mf, y)


def _rope_lanes(v):
    z = jnp.zeros(v.shape[:-1] + (QK_ROPE // 2,), v.dtype)
    return jnp.concatenate([v[..., :QK_ROPE // 2], z, v[..., QK_ROPE // 2:], z], axis=-1)


def kernel(x, mem, positions, g_mix, g_ffn, g_mem, w_mem_kv, g_mq, g_mk, w_out, a_w_in, a_conv_w, b_w_in,
           b_g_q_a, b_w_q_b, b_g_qn, g_kv, w_kv_a, g_kv_a, w_kv_b, g_kn, ffn_w_gate, ffn_w_up, ffn_w_down,
           moe_w_router, moe_w_gate, moe_w_up, moe_w_down):
    row = lambda v: v.reshape(1, -1).astype(f32)
    x0 = x.reshape(N_TOK, D_MODEL)
    mem2 = mem.reshape(BATCH * N_MEM, D_MODEL)

    w_kva = jnp.concatenate([w_kv_a[:, :KV_LORA], _rope_lanes(w_kv_a[:, KV_LORA:])], axis=1).astype(bf16)
    kvb = w_kv_b.reshape(KV_LORA, N_HEADS, QK_NOPE + HEAD_DIM)
    w_kvb = jnp.concatenate([kvb[:, :, :QK_NOPE].reshape(KV_LORA, MIX_WIDTH),
                             kvb[:, :, QK_NOPE:].reshape(KV_LORA, MIX_WIDTH)], axis=1).astype(bf16)
    qb = b_w_q_b[0].reshape(Q_LORA, N_HEADS, QK_HEAD)
    w_qb = jnp.concatenate([qb[:, :, :QK_NOPE], _rope_lanes(qb[:, :, QK_NOPE:])], axis=-1)
    w_qb = w_qb.reshape(Q_LORA, N_HEADS * HEAD_PAD).astype(bf16)
    gk_n, gk_r = row(g_kn[:QK_NOPE]), row(_rope_lanes(g_kn[QK_NOPE:]))
    gq_n, gq_r = row(b_g_qn[0, :QK_NOPE]), row(_rope_lanes(b_g_qn[0, QK_NOPE:]))
    w_router_pad = jnp.pad(moe_w_router[0].astype(f32), ((0, 0), (0, LANES - N_EXPERTS)))

    inv_freq = 1.0 / (ROPE_THETA ** (jnp.arange(0, QK_ROPE, 2, dtype=f32) / QK_ROPE))
    half = jnp.ones((QK_ROPE // 2,), f32)
    rope_freq = row(_rope_lanes(jnp.concatenate([inv_freq, inv_freq])))
    rope_sign = row(_rope_lanes(jnp.concatenate([-half, half])))
    pos = positions.reshape(N_TOK, 1).astype(jnp.int32)

    u0 = norm_matmul(x0, row(g_mix[0]), a_w_in[0], tm=1024, tn=1280)
    kvm0 = norm_matmul(mem2, row(g_mem[0]), w_mem_kv[0], tm=BATCH * N_MEM, tn=2 * MEM_WIDTH)
    xa, hp0 = conv_tail(u0, a_conv_w[0].astype(f32), kvm0, row(g_mq[0]), row(g_mk[0]),
                        w_out[0].astype(bf16), x0, row(g_ffn[0]))
    dense_tbl = _super_tiles(jnp.array([N_TOK], jnp.int32), jnp.array([0], jnp.int32),
                             -(-N_TOK // (FFN_SUB * FFN_SUPER)))
    y0 = swiglu_ffn(hp0, ffn_w_gate, ffn_w_up, ffn_w_down, *dense_tbl)

    x1, k, v, q, um1 = mla_kvq(xa, y0, pos, rope_freq, rope_sign, row(g_kv), w_kva, row(g_kv_a), w_kvb, gk_n, gk_r,
                               row(g_mix[1]), b_w_in[0].astype(bf16), row(b_g_q_a[0]), w_qb, gq_n, gq_r)
    mix1 = flash_attention(q, k, v)
    kvm1 = norm_matmul(mem2, row(g_mem[1]), w_mem_kv[1], tm=BATCH * N_MEM, tn=2 * MEM_WIDTH)
    x2, hp1 = attn_tail(mix1, um1, 0, kvm1, row(g_mq[1]), row(g_mk[1]), w_out[1].astype(bf16), x1, row(g_ffn[1]))

    mi, mf, cnt = router(x2, row(g_ffn[1]), w_router_pad)
    counts = cnt[0, :N_EXPERTS]
    padded = (counts + FFN_SUB - 1) // FFN_SUB * FFN_SUB
    offsets = jnp.cumsum(padded) - padded
    pos1 = offsets[mi[0]] + mi[2]
    pos2 = offsets[mi[1]] + mi[3]
    pad_start = jnp.concatenate([offsets + counts, jnp.sum(padded, keepdims=True)])
    xs = dispatch(pos1, pos2, pad_start, padded - counts, hp1)
    y1 = swiglu_ffn(xs, moe_w_gate[0], moe_w_up[0], moe_w_down[0], *_super_tiles(padded, offsets, MOE_SUPERS))
    out = combine(pos1, pos2, x2, mf, y1)
    return out.reshape(BATCH, SEQ, D_MODEL)
```

```python
import functools

import jax
import jax.numpy as jnp
from jax import lax
from jax.experimental import pallas as pl
from jax.experimental.pallas import tpu as pltpu

D_MODEL = 2048
BATCH = 2
SEQ = 4096
N_TOK = BATCH * SEQ
N_HEADS = 12
HEAD_DIM = 128
MIX_WIDTH = N_HEADS * HEAD_DIM
MEM_HEADS = 4
MEM_WIDTH = MEM_HEADS * HEAD_DIM
N_MEM = 256
CONV_WIDTH = 3
Q_LORA = 512
KV_LORA = 256
QK_NOPE = 128
QK_ROPE = 64
QK_HEAD = QK_NOPE + QK_ROPE
ROPE_THETA = 10000.0
D_FF = 7168
N_EXPERTS = 8
TOP_K = 2
EPS = 1e-6

LANES = 128
SUBLANES = 8
HEAD_PAD = 2 * LANES
HALF_D = D_MODEL // 2
VMEM_LIMIT = 56 * 1024 * 1024

FFN_SUB = 256
FFN_SUPER = 9
FFN_TAIL = 64
FFN_TF = 256
MOE_ROWS = N_TOK * TOP_K + N_EXPERTS * FFN_SUB
MOE_SUPERS = N_EXPERTS + -(-MOE_ROWS // (FFN_SUB * FFN_SUPER))

f32 = jnp.float32
bf16 = jnp.bfloat16


def _cparams(sem):
    return pltpu.CompilerParams(dimension_semantics=sem, vmem_limit_bytes=VMEM_LIMIT)


def _rms(x, g, n=None):
    n = x.shape[-1] if n is None else n
    ms = jnp.sum(x * x, axis=-1, keepdims=True) * (1.0 / n)
    return x * lax.rsqrt(ms + EPS) * g


def _pack_halves(h):
    hb = h.astype(bf16).astype(f32)
    return pltpu.pack_elementwise([hb[:, :HALF_D], hb[:, HALF_D:]], packed_dtype=bf16)


def _unpack_halves(p):
    lo = pltpu.unpack_elementwise(p, index=0, packed_dtype=bf16, unpacked_dtype=f32)
    hi = pltpu.unpack_elementwise(p, index=1, packed_dtype=bf16, unpacked_dtype=f32)
    return lo.astype(bf16), hi.astype(bf16)


def _dot(a, b):
    return jnp.dot(a, b, preferred_element_type=f32)


def _dot_nt(a, b):
    return lax.dot_general(a, b, (((1,), (1,)), ((), ())), preferred_element_type=f32)


def _norm_matmul_kernel(x_ref, g_ref, w_ref, o_ref, h_ref):
    @pl.when(pl.program_id(1) == 0)
    def _():
        h_ref[...] = _rms(x_ref[...], g_ref[...]).astype(bf16)

    o_ref[...] = _dot(h_ref[...], w_ref[...].astype(bf16)).astype(o_ref.dtype)


def norm_matmul(x, g, w, tm, tn, out_dtype=bf16):
    m, d = x.shape
    n = w.shape[1]
    return pl.pallas_call(
        _norm_matmul_kernel,
        grid=(m // tm, n // tn),
        in_specs=[pl.BlockSpec((tm, d), lambda i, j: (i, 0)),
                  pl.BlockSpec((1, d), lambda i, j: (0, 0)),
                  pl.BlockSpec((d, tn), lambda i, j: (0, j))],
        out_specs=pl.BlockSpec((tm, tn), lambda i, j: (i, j)),
        out_shape=jax.ShapeDtypeStruct((m, n), out_dtype),
        scratch_shapes=[pltpu.VMEM((tm, d), bf16)],
        compiler_params=_cparams(("arbitrary", "arbitrary")),
        name="norm_matmul",
    )(x, g, w)


def _mem_attention(um_ref, kvm_ref, gq_ref, gk_ref, cat_ref):
    scale = HEAD_DIM ** -0.5
    for h in range(MEM_HEADS):
        cs = slice(h * HEAD_DIM, (h + 1) * HEAD_DIM)
        q = _rms(um_ref[:, cs].astype(f32), gq_ref[...]).astype(bf16)
        k = _rms(kvm_ref[:, cs].astype(f32), gk_ref[...]).astype(bf16)
        v = kvm_ref[:, MEM_WIDTH + h * HEAD_DIM:MEM_WIDTH + (h + 1) * HEAD_DIM]
        s = _dot_nt(q, k) * scale
        p = jnp.exp(s - jnp.max(s, axis=-1, keepdims=True))
        o = _dot(p.astype(bf16), v) / jnp.sum(p, axis=-1, keepdims=True)
        cat_ref[:, MIX_WIDTH + h * HEAD_DIM:MIX_WIDTH + (h + 1) * HEAD_DIM] = o.astype(bf16)


def _tail_epilogue(cat_ref, wout_ref, x_ref, gffn_ref, xo_ref, hp_ref):
    xn = x_ref[...] + _dot(cat_ref[...], wout_ref[...])
    xo_ref[...] = xn
    hp_ref[...] = _pack_halves(_rms(xn, gffn_ref[...]))


def _conv_tail_kernel(xin_ref, gb_ref, gc_ref, um_ref, hx_ref, hg_ref, cw_ref, kvm_ref, gq_ref, gk_ref,
                      wout_ref, x_ref, gffn_ref, xo_ref, hp_ref, cat_ref, *, tm):
    i = pl.program_id(0)
    first = (i * tm) % SEQ == 0
    rows = lax.broadcasted_iota(jnp.int32, (tm, 1), 0)
    cc = 512
    for c in range(MIX_WIDTH // cc):
        cs = slice(c * cc, (c + 1) * cc)
        z = gc_ref[:, cs].astype(f32) * xin_ref[:, cs].astype(f32)
        zh = hg_ref[:, cs].astype(f32) * hx_ref[:, cs].astype(f32)
        zh = jnp.where(first, 0.0, zh)
        p1 = zh[15:16, :]
        p2 = zh[14:15, :]
        z1 = jnp.where(rows == 0, p1, pltpu.roll(z, 1, 0))
        z2 = jnp.where(rows == 0, p2, jnp.where(rows == 1, p1, pltpu.roll(z, 2, 0)))
        y = cw_ref[0:1, cs] * z2 + cw_ref[1:2, cs] * z1 + cw_ref[2:3, cs] * z
        cat_ref[:, cs] = (gb_ref[:, cs].astype(f32) * y).astype(bf16)
    _mem_attention(um_ref, kvm_ref, gq_ref, gk_ref, cat_ref)
    _tail_epilogue(cat_ref, wout_ref, x_ref, gffn_ref, xo_ref, hp_ref)


def _attn_tail_kernel(mix_ref, um_ref, kvm_ref, gq_ref, gk_ref, wout_ref, x_ref, gffn_ref,
                      xo_ref, hp_ref, cat_ref):
    cat_ref[:, :MIX_WIDTH] = mix_ref[...]
    _mem_attention(um_ref, kvm_ref, gq_ref, gk_ref, cat_ref)
    _tail_epilogue(cat_ref, wout_ref, x_ref, gffn_ref, xo_ref, hp_ref)


def _tail_common_specs(tm):
    per_seq = SEQ // tm
    return [pl.BlockSpec((N_MEM, 2 * MEM_WIDTH), lambda i: (i // per_seq, 0)),
            pl.BlockSpec((1, HEAD_DIM), lambda i: (0, 0)),
            pl.BlockSpec((1, HEAD_DIM), lambda i: (0, 0)),
            pl.BlockSpec((D_MODEL, D_MODEL), lambda i: (0, 0), pipeline_mode=pl.Buffered(1)),
            pl.BlockSpec((tm, D_MODEL), lambda i: (i, 0)),
            pl.BlockSpec((1, D_MODEL), lambda i: (0, 0))]


def _tail_outs(tm):
    return dict(
        out_specs=[pl.BlockSpec((tm, D_MODEL), lambda i: (i, 0)),
                   pl.BlockSpec((tm, HALF_D), lambda i: (i, 0))],
        out_shape=[jax.ShapeDtypeStruct((N_TOK, D_MODEL), f32),
                   jax.ShapeDtypeStruct((N_TOK, HALF_D), jnp.uint32)],
        scratch_shapes=[pltpu.VMEM((tm, D_MODEL), bf16)],
        compiler_params=_cparams(("arbitrary",)),
    )


def conv_tail(u, conv_w, kvm, g_mq, g_mk, w_out, x, g_ffn, tm=512):
    halo = 16
    um_blk = 3 * MIX_WIDTH // MEM_WIDTH
    prev = lambda i: jnp.maximum(i * (tm // halo) - 1, 0)
    in_specs = [pl.BlockSpec((tm, MIX_WIDTH), lambda i: (i, 0)),
                pl.BlockSpec((tm, MIX_WIDTH), lambda i: (i, 1)),
                pl.BlockSpec((tm, MIX_WIDTH), lambda i: (i, 2)),
                pl.BlockSpec((tm, MEM_WIDTH), lambda i: (i, um_blk)),
                pl.BlockSpec((halo, MIX_WIDTH), lambda i: (prev(i), 0)),
                pl.BlockSpec((halo, MIX_WIDTH), lambda i: (prev(i), 2)),
                pl.BlockSpec((CONV_WIDTH, MIX_WIDTH), lambda i: (0, 0))] + _tail_common_specs(tm)
    return pl.pallas_call(
        functools.partial(_conv_tail_kernel, tm=tm),
        grid=(N_TOK // tm,), in_specs=in_specs, name="conv_tail", **_tail_outs(tm),
    )(u, u, u, u, u, u, conv_w, kvm, g_mq, g_mk, w_out, x, g_ffn)


def attn_tail(mix, um_src, um_blk, kvm, g_mq, g_mk, w_out, x, g_ffn, tm=512):
    in_specs = [pl.BlockSpec((tm, MIX_WIDTH), lambda i: (i, 0)),
                pl.BlockSpec((tm, MEM_WIDTH), lambda i: (i, um_blk))] + _tail_common_specs(tm)
    return pl.pallas_call(
        _attn_tail_kernel,
        grid=(N_TOK // tm,), in_specs=in_specs, name="attn_tail", **_tail_outs(tm),
    )(mix, um_src, kvm, g_mq, g_mk, w_out, x, g_ffn)


def _ffn_kernel(exp_ref, row0_ref, nsub_ref, work_ref, tail_ref, xs_hbm, wg_ref, wu_ref, wd_ref, y_hbm,
                x_s, acc, wg_s, wu_s, wd_s, sem):
    s = pl.program_id(0)
    f = pl.program_id(1)
    nsub = nsub_ref[s]
    row0 = row0_ref[s]
    per_sub = FFN_SUB // FFN_TAIL
    nfull = lax.shift_right_logical(work_ref[s], per_sub.bit_length() - 1)
    ntail = work_ref[s] & (per_sub - 1)
    last_f = f == pl.num_programs(1) - 1
    last_step = (s == pl.num_programs(0) - 1) & last_f

    def x_copy(j):
        r = pl.multiple_of(j * FFN_SUB, FFN_SUB)
        g = pl.multiple_of(row0 + r, FFN_SUB)
        return pltpu.make_async_copy(xs_hbm.at[pl.ds(g, FFN_SUB)], x_s.at[pl.ds(r, FFN_SUB)], sem)

    def y_copy(r, rows):
        g = pl.multiple_of(row0 + r, FFN_SUB)
        return pltpu.make_async_copy(acc.at[pl.ds(r, rows)], y_hbm.at[pl.ds(g, rows)], sem)

    def for_subtiles(fn):
        def body(j, c):
            fn(j)
            return c
        lax.fori_loop(0, nsub, body, 0)

    @pl.when(f == 0)
    def _():
        for_subtiles(lambda j: x_copy(j).start())

        def zero(j):
            r = pl.multiple_of(j * FFN_SUB, FFN_SUB)
            acc[pl.ds(r, FFN_SUB), :] = jnp.zeros((FFN_SUB, D_MODEL), f32)
        for_subtiles(zero)
        for_subtiles(lambda j: x_copy(j).wait())

    def whole_subtiles(rows):
        return -(-rows // FFN_SUB) * FFN_SUB

    def swiglu_rows(r, rows, wg, wu, wd):
        lo, hi = _unpack_halves(x_s[pl.ds(r, rows), :])
        g = _dot(lo, wg[:HALF_D, :]) + _dot(hi, wg[HALF_D:, :])
        u = _dot(lo, wu[:HALF_D, :]) + _dot(hi, wu[HALF_D:, :])
        a = (g * jax.nn.sigmoid(g) * u).astype(bf16)
        acc[pl.ds(r, rows), :] += _dot(a, wd[...])

        @pl.when(last_f)
        def _():
            y_copy(r, whole_subtiles(rows)).start()

    n_first = jnp.where(nfull >= 4, 4 + (nfull & 1), 1)
    rem = nfull - n_first
    n_quad = lax.shift_right_logical(rem, 2)
    first_rows = pl.multiple_of(n_first * FFN_SUB, FFN_SUB)
    quad_row = lambda j: pl.multiple_of(first_rows + j * 4 * FFN_SUB, FFN_SUB)
    pair_row = quad_row(n_quad)
    last_row = pl.multiple_of((nfull - 1) * FFN_SUB, FFN_SUB)
    tail_row = pl.multiple_of(nfull * FFN_SUB, FFN_SUB)

    def for_groups(first, fn):
        for n in (1, 4, 5):
            @pl.when(n_first == n)
            def _():
                first(n * FFN_SUB)

        def body(j, c):
            fn(quad_row(j), 4 * FFN_SUB)
            return c
        lax.fori_loop(0, n_quad, body, 0)

        @pl.when((rem & 2) != 0)
        def _():
            fn(pair_row, 2 * FFN_SUB)

        @pl.when((rem & 1) != 0)
        def _():
            fn(last_row, FFN_SUB)

        for n in range(1, per_sub):
            @pl.when(ntail == n)
            def _():
                fn(tail_row, n * FFN_TAIL)

    def first_group(rows):
        wg = wg_ref[0].astype(bf16)
        wu = wu_ref[0].astype(bf16)
        wd = wd_ref[0].astype(bf16)
        wg_s[...] = wg
        wu_s[...] = wu
        wd_s[...] = wd
        swiglu_rows(0, rows, wg, wu, wd)

    for_groups(first_group, lambda r, rows: swiglu_rows(r, rows, wg_s, wu_s, wd_s))

    @pl.when(last_f)
    def _():
        for_groups(lambda rows: y_copy(0, rows).wait(), lambda r, rows: y_copy(r, whole_subtiles(rows)).wait())

    @pl.when(last_step)
    def _():
        acc[pl.ds(0, FFN_SUB), :] = jnp.zeros((FFN_SUB, D_MODEL), f32)

        def tail_copy(j):
            g = pl.multiple_of(j * FFN_SUB, FFN_SUB)
            return pltpu.make_async_copy(acc.at[pl.ds(0, FFN_SUB)], y_hbm.at[pl.ds(g, FFN_SUB)], sem)

        n_tail = y_hbm.shape[0] // FFN_SUB
        lax.fori_loop(tail_ref[0], n_tail, lambda j, c: (tail_copy(j).start(), c)[1], 0)
        lax.fori_loop(tail_ref[0], n_tail, lambda j, c: (tail_copy(j).wait(), c)[1], 0)


def swiglu_ffn(xs, w_gate, w_up, w_down, st_expert, st_row0, st_nsub, st_work, st_tail):
    rows = xs.shape[0]
    nf = D_FF // FFN_TF

    grid_spec = pltpu.PrefetchScalarGridSpec(
        num_scalar_prefetch=5,
        grid=(st_tail[1], nf),
        in_specs=[pl.BlockSpec(memory_space=pl.ANY),
                  pl.BlockSpec((1, D_MODEL, FFN_TF), lambda s, f, e, *_: (e[s], 0, f)),
                  pl.BlockSpec((1, D_MODEL, FFN_TF), lambda s, f, e, *_: (e[s], 0, f)),
                  pl.BlockSpec((1, FFN_TF, D_MODEL), lambda s, f, e, *_: (e[s], f, 0))],
        out_specs=pl.BlockSpec(memory_space=pl.ANY),
        scratch_shapes=[pltpu.VMEM((FFN_SUPER * FFN_SUB, HALF_D), jnp.uint32),
                        pltpu.VMEM((FFN_SUPER * FFN_SUB, D_MODEL), f32),
                        pltpu.VMEM((D_MODEL, FFN_TF), bf16),
                        pltpu.VMEM((D_MODEL, FFN_TF), bf16),
                        pltpu.VMEM((FFN_TF, D_MODEL), bf16),
                        pltpu.SemaphoreType.DMA(())],
    )
    return pl.pallas_call(
        _ffn_kernel,
        grid_spec=grid_spec,
        out_shape=jax.ShapeDtypeStruct((rows, D_MODEL), f32),
        compiler_params=_cparams(("arbitrary", "arbitrary")),
        name="swiglu_ffn",
    )(st_expert, st_row0, st_nsub, st_work, st_tail, xs, w_gate, w_up, w_down)


def _super_tiles(counts, padded, offsets, n_super):
    cap = FFN_SUB * FFN_SUPER
    per_group = (padded + cap - 1) // cap
    ends = jnp.cumsum(per_group)
    t = jnp.arange(n_super, dtype=jnp.int32)
    grp = jnp.minimum(jnp.sum((t[:, None] >= ends[None, :]).astype(jnp.int32), axis=1), padded.shape[0] - 1)
    k = t - (ends[grp] - per_group[grp])
    used = t < ends[-1]
    nsub = jnp.where(used, jnp.minimum(FFN_SUPER, (padded[grp] - k * cap) // FFN_SUB), 0)
    row0 = jnp.where(used, offsets[grp] + k * cap, 0)
    valid = jnp.clip(counts[grp] - k * cap, 0, nsub * FFN_SUB)
    work = jnp.where(used, jnp.maximum((valid + FFN_TAIL - 1) // FFN_TAIL, FFN_SUB // FFN_TAIL), 0)
    last_grp = grp[jnp.maximum(ends[-1] - 1, 0)]
    grp = jnp.where(used, grp, last_grp)
    tail = jnp.stack([jnp.sum(padded) // FFN_SUB, ends[-1]])
    return (grp.astype(jnp.int32), row0.astype(jnp.int32), nsub.astype(jnp.int32), work.astype(jnp.int32),
            tail.astype(jnp.int32))


def _rope(x, cos, sin):
    return x * cos + pltpu.roll(x, LANES // 2, 1) * sin


def _dot_split(a, sel):
    hi = a.astype(bf16)
    lo = (a - hi.astype(f32)).astype(bf16)
    return _dot(hi, sel) + _dot(lo, sel)


def _mla_kvq_kernel(x0_ref, y_ref, pos_ref, freq_ref, sign_ref, gkv_ref, wkva_ref, gkva_ref, wkvb_ref, gkn_ref, gkr_ref,
                    gmix_ref, win_ref, gqa_ref, wqb_ref, gqn_ref, gqr_ref, sel_ref,
                    x1_ref, k_ref, v_ref, q_ref, um_ref):
    x1 = x0_ref[...] + y_ref[...]
    x1_ref[...] = x1
    ang = pos_ref[...].astype(f32) * freq_ref[...]
    cos = jnp.cos(ang)
    sin = jnp.sin(ang) * sign_ref[...]

    tm = x1.shape[0]
    kva = _dot(_rms(x1, gkv_ref[...]).astype(bf16), wkva_ref[...])
    u = _dot(_rms(x1, gmix_ref[...]).astype(bf16), win_ref[...])
    um_ref[...] = u[:, Q_LORA:].astype(bf16)
    c_kv = _rms(kva[:, :KV_LORA], gkva_ref[...]).astype(bf16)
    kv = _dot(c_kv, wkvb_ref[...])
    c_q = _rms(u[:, :Q_LORA], gqa_ref[...]).astype(bf16)
    q = _dot(c_q, wqb_ref[...])
    v_ref[...] = kv[:, MIX_WIDTH:].astype(bf16)

    kpe = kva[:, KV_LORA:]
    ss_pe = jnp.sum(kpe * kpe, axis=-1, keepdims=True)
    sck = lax.rsqrt((_dot_split(kv[:, :MIX_WIDTH] * kv[:, :MIX_WIDTH], sel_ref[...]) + ss_pe) * (1.0 / QK_HEAD) + EPS)
    q2 = jnp.concatenate([q[:, h * HEAD_PAD:h * HEAD_PAD + LANES] * q[:, h * HEAD_PAD:h * HEAD_PAD + LANES]
                          + q[:, h * HEAD_PAD + LANES:(h + 1) * HEAD_PAD] * q[:, h * HEAD_PAD + LANES:(h + 1) * HEAD_PAD]
                          for h in range(N_HEADS)], axis=-1)
    scq = lax.rsqrt(_dot_split(q2, sel_ref[...]) * (1.0 / QK_HEAD) + EPS)
    k_rot = _rope(kpe * gkr_ref[...], cos, sin)
    for h in range(N_HEADS):
        sch = jnp.broadcast_to(sck[:, h:h + 1], (tm, LANES))
        k_ref[:, h * HEAD_PAD:h * HEAD_PAD + LANES] = (
            kv[:, h * HEAD_DIM:(h + 1) * HEAD_DIM] * sch * gkn_ref[...]).astype(bf16)
        k_ref[:, h * HEAD_PAD + LANES:(h + 1) * HEAD_PAD] = (k_rot * sch).astype(bf16)
        sch = jnp.broadcast_to(scq[:, h:h + 1], (tm, LANES))
        qn = q[:, h * HEAD_PAD:h * HEAD_PAD + LANES]
        qr = q[:, h * HEAD_PAD + LANES:(h + 1) * HEAD_PAD]
        q_ref[:, h * HEAD_PAD:h * HEAD_PAD + LANES] = (qn * sch * gqn_ref[...]).astype(bf16)
        q_ref[:, h * HEAD_PAD + LANES:(h + 1) * HEAD_PAD] = (_rope(qr * gqr_ref[...], cos, sin) * sch).astype(bf16)


def mla_kvq(x0, y, pos, rope_freq, rope_sign, g_kv, w_kva, g_kva, w_kvb, gk_n, gk_r, g_mix, w_in, g_qa, w_qb, gq_n, gq_r, tm=256):
    row = lambda w: pl.BlockSpec((tm, w), lambda i: (i, 0))
    full = lambda a: pl.BlockSpec(a.shape, lambda i: (0, 0))
    head = jnp.arange(MIX_WIDTH, dtype=jnp.int32) // HEAD_DIM
    sel = (head[:, None] == jnp.arange(LANES, dtype=jnp.int32)[None, :]).astype(bf16)
    consts = (rope_freq, rope_sign, g_kv, w_kva, g_kva, w_kvb, gk_n, gk_r, g_mix, w_in, g_qa, w_qb, gq_n, gq_r, sel)
    return pl.pallas_call(
        _mla_kvq_kernel,
        grid=(N_TOK // tm,),
        in_specs=[row(D_MODEL), row(D_MODEL), row(1)] + [full(a) for a in consts],
        out_specs=[row(D_MODEL), row(N_HEADS * HEAD_PAD), row(MIX_WIDTH), row(N_HEADS * HEAD_PAD), row(MEM_WIDTH)],
        out_shape=[jax.ShapeDtypeStruct((N_TOK, D_MODEL), f32),
                   jax.ShapeDtypeStruct((N_TOK, N_HEADS * HEAD_PAD), bf16),
                   jax.ShapeDtypeStruct((N_TOK, MIX_WIDTH), bf16),
                   jax.ShapeDtypeStruct((N_TOK, N_HEADS * HEAD_PAD), bf16),
                   jax.ShapeDtypeStruct((N_TOK, MEM_WIDTH), bf16)],
        compiler_params=_cparams(("arbitrary",)),
        name="mla_kvq",
    )(x0, y, pos, *consts)


def _flash_kernel(q_ref, k_ref, v_ref, o_ref, m_s, l_s, acc_s, *, tq, heads):
    qi = pl.program_id(2)
    c_exp = (QK_HEAD ** -0.5) * 1.4426950408889634
    m_s[...] = jnp.full(m_s.shape, -jnp.inf, f32)
    l_s[...] = jnp.zeros(l_s.shape, f32)
    acc_s[...] = jnp.zeros(acc_s.shape, f32)
    n_blk = tq // LANES

    def chunk(kj, masked):
        r = pl.multiple_of(kj * tq, tq)

        def scores(h):
            q = q_ref[:, h * HEAD_PAD:(h + 1) * HEAD_PAD]
            s = _dot_nt(q, k_ref[pl.ds(r, tq), h * HEAD_PAD:(h + 1) * HEAD_PAD])
            if masked:
                qpos = lax.broadcasted_iota(jnp.int32, (tq, tq), 0)
                kpos = lax.broadcasted_iota(jnp.int32, (tq, tq), 1)
                s = jnp.where(kpos <= qpos, s, -jnp.inf)
            return s

        def update(h, s):
            blocks = [s[:, b * LANES:(b + 1) * LANES] for b in range(n_blk)]
            lane_max = functools.reduce(jnp.maximum, blocks)
            m_old = m_s[h]
            m_new = jnp.maximum(m_old, jnp.max(lane_max, axis=-1, keepdims=True))
            alpha = jnp.exp2((m_old - m_new) * c_exp)
            p = [jnp.exp2((blk - m_new) * c_exp) for blk in blocks]
            l_s[h] = alpha * l_s[h] + functools.reduce(jnp.add, p)
            pv = _dot(jnp.concatenate(p, axis=-1).astype(bf16),
                      v_ref[pl.ds(r, tq), h * HEAD_DIM:(h + 1) * HEAD_DIM])
            acc_s[h] = alpha * acc_s[h] + pv
            m_s[h] = m_new

        ahead = 1
        pending = [scores(h) for h in range(min(ahead, heads))]
        for h in range(heads):
            if h + ahead < heads:
                pending.append(scores(h + ahead))
            update(h, pending[h])

    def body(kj, c):
        chunk(kj, False)
        return c

    lax.fori_loop(0, qi, body, 0)
    chunk(qi, True)
    for h in range(heads):
        l = jnp.sum(l_s[h], axis=-1, keepdims=True)
        o_ref[:, h * HEAD_DIM:(h + 1) * HEAD_DIM] = (acc_s[h] / l).astype(o_ref.dtype)


def flash_attention(q, k, v, tq=512, heads=6):
    per_seq = SEQ // tq
    return pl.pallas_call(
        functools.partial(_flash_kernel, tq=tq, heads=heads),
        grid=(BATCH, N_HEADS // heads, per_seq),
        in_specs=[pl.BlockSpec((tq, heads * HEAD_PAD), lambda b, h, i: (b * per_seq + i, h)),
                  pl.BlockSpec((SEQ, heads * HEAD_PAD), lambda b, h, i: (b, h)),
                  pl.BlockSpec((SEQ, heads * HEAD_DIM), lambda b, h, i: (b, h))],
        out_specs=pl.BlockSpec((tq, heads * HEAD_DIM), lambda b, h, i: (b * per_seq + i, h)),
        out_shape=jax.ShapeDtypeStruct((N_TOK, MIX_WIDTH), bf16),
        scratch_shapes=[pltpu.VMEM((heads, tq, LANES), f32),
                        pltpu.VMEM((heads, tq, LANES), f32),
                        pltpu.VMEM((heads, tq, HEAD_DIM), f32)],
        compiler_params=_cparams(("arbitrary", "arbitrary", "arbitrary")),
        name="flash_attention",
    )(q, k, v)


def _router_kernel(x_ref, g_ref, wr_ref, mi_ref, mf_ref, cnt_ref, carry, *, tm):
    i = pl.program_id(0)

    @pl.when(i == 0)
    def _():
        carry[...] = jnp.zeros_like(carry)

    h = _rms(x_ref[...], g_ref[...])
    w = wr_ref[...]
    h_hi, w_hi = h.astype(bf16), w.astype(bf16)
    h_lo, w_lo = (h - h_hi.astype(f32)).astype(bf16), (w - w_hi.astype(f32)).astype(bf16)
    logits = _dot(h_hi, w_hi) + (_dot(h_hi, w_lo) + _dot(h_lo, w_hi))
    lane = lax.broadcasted_iota(jnp.int32, (tm, LANES), 1)
    logits = jnp.where(lane < N_EXPERTS, logits, -jnp.inf)
    lane_f = lane.astype(f32)
    v1 = jnp.max(logits, axis=-1, keepdims=True)
    e1 = jnp.min(jnp.where(logits == v1, lane_f, float(LANES)), axis=-1, keepdims=True).astype(jnp.int32)
    rest = jnp.where(lane == e1, -jnp.inf, logits)
    v2 = jnp.max(rest, axis=-1, keepdims=True)
    e2 = jnp.min(jnp.where(rest == v2, lane_f, float(LANES)), axis=-1, keepdims=True).astype(jnp.int32)
    t = jnp.exp(v2 - v1)
    w1 = 1.0 / (1.0 + t)
    w2 = t / (1.0 + t)

    hot = jnp.where((lane == e1) | (lane == e2), 1.0, 0.0)
    r_io = lax.broadcasted_iota(jnp.int32, (tm, tm), 0)
    c_io = lax.broadcasted_iota(jnp.int32, (tm, tm), 1)
    below = jnp.where(c_io < r_io, 1.0, 0.0).astype(bf16)
    rank = _dot(below, hot.astype(bf16)) + carry[...]
    carry[...] += jnp.sum(hot, axis=0, keepdims=True)
    rank1 = jnp.sum(jnp.where(lane == e1, rank, 0.0), axis=-1, keepdims=True).astype(jnp.int32)
    rank2 = jnp.sum(jnp.where(lane == e2, rank, 0.0), axis=-1, keepdims=True).astype(jnp.int32)

    meta = jnp.where(lane == 0, e1, jnp.where(lane == 1, e2, jnp.where(lane == 2, rank1,
                     jnp.where(lane == 3, rank2, 0))))
    mi_ref[...] = meta.T[:8, :]
    mf_ref[...] = jnp.where(lane == 0, w1, jnp.where(lane == 1, w2, 0.0))
    cnt_ref[...] = carry[...].astype(jnp.int32)


def router(x, g, w_router_pad, tm=512):
    return pl.pallas_call(
        functools.partial(_router_kernel, tm=tm),
        grid=(N_TOK // tm,),
        in_specs=[pl.BlockSpec((tm, D_MODEL), lambda i: (i, 0)),
                  pl.BlockSpec((1, D_MODEL), lambda i: (0, 0)),
                  pl.BlockSpec((D_MODEL, LANES), lambda i: (0, 0))],
        out_specs=[pl.BlockSpec((8, tm), lambda i: (0, i)),
                   pl.BlockSpec((tm, LANES), lambda i: (i, 0)),
                   pl.BlockSpec((1, LANES), lambda i: (0, 0))],
        out_shape=[jax.ShapeDtypeStruct((8, N_TOK), jnp.int32),
                   jax.ShapeDtypeStruct((N_TOK, LANES), f32),
                   jax.ShapeDtypeStruct((1, LANES), jnp.int32)],
        scratch_shapes=[pltpu.VMEM((1, LANES), f32)],
        compiler_params=_cparams(("arbitrary",)),
        name="router",
    )(x, g, w_router_pad)


def _dispatch_kernel(p1_ref, p2_ref, pad0_ref, padn_ref, hp_ref, xs_out, zrow, sem, *, tm):
    base = pl.program_id(0) * tm

    @pl.when(pl.program_id(0) == 0)
    def _():
        zrow[...] = jnp.zeros(zrow.shape, zrow.dtype)

        def pad_copy(e, i):
            return pltpu.make_async_copy(zrow.at[pl.ds(0, 1)], xs_out.at[pl.ds(pad0_ref[e] + i, 1)], sem)

        def tail_copy(j):
            r = pl.multiple_of(j * FFN_SUB, FFN_SUB)
            return pltpu.make_async_copy(zrow, xs_out.at[pl.ds(r, FFN_SUB)], sem)

        tail0 = lax.shift_right_logical(pad0_ref[N_EXPERTS], FFN_SUB.bit_length() - 1)
        n_tail = xs_out.shape[0] // FFN_SUB
        for e in range(N_EXPERTS):
            lax.fori_loop(0, padn_ref[e], lambda i, c: (pad_copy(e, i).start(), c)[1], 0)
        lax.fori_loop(tail0, n_tail, lambda j, c: (tail_copy(j).start(), c)[1], 0)
        for e in range(N_EXPERTS):
            lax.fori_loop(0, padn_ref[e], lambda i, c: (pad_copy(e, i).wait(), c)[1], 0)
        lax.fori_loop(tail0, n_tail, lambda j, c: (tail_copy(j).wait(), c)[1], 0)

    def copy(g, u, pos_ref):
        r = pl.multiple_of(g * SUBLANES, SUBLANES) + u
        return pltpu.make_async_copy(hp_ref.at[pl.ds(r, 1)], xs_out.at[pl.ds(pos_ref[base + r], 1)], sem)

    def start(g, c):
        for u in range(SUBLANES):
            copy(g, u, p1_ref).start()
            copy(g, u, p2_ref).start()
        return c

    def wait(g, c):
        for u in range(SUBLANES):
            copy(g, u, p1_ref).wait()
            copy(g, u, p2_ref).wait()
        return c

    lax.fori_loop(0, tm // SUBLANES, start, 0)
    lax.fori_loop(0, tm // SUBLANES, wait, 0)


def dispatch(pos1, pos2, pad_start, pad_len, hp, tm=512):
    grid_spec = pltpu.PrefetchScalarGridSpec(
        num_scalar_prefetch=4,
        grid=(N_TOK // tm,),
        in_specs=[pl.BlockSpec((tm, HALF_D), lambda i, *_: (i, 0))],
        out_specs=pl.BlockSpec(memory_space=pl.ANY),
        scratch_shapes=[pltpu.VMEM((FFN_SUB, HALF_D), jnp.uint32), pltpu.SemaphoreType.DMA(())],
    )
    return pl.pallas_call(
        functools.partial(_dispatch_kernel, tm=tm),
        grid_spec=grid_spec,
        out_shape=jax.ShapeDtypeStruct((MOE_ROWS, HALF_D), jnp.uint32),
        compiler_params=_cparams(("arbitrary",)),
        name="dispatch",
    )(pos1, pos2, pad_start, pad_len, hp)


def _combine_kernel(p1_ref, p2_ref, x_ref, mf_ref, y_hbm, o_ref, buf, sem, *, tm):
    i = pl.program_id(0)

    def for_tile(t, fn):
        slot = t & 1
        base = t * tm

        def body(g, c):
            for u in range(SUBLANES):
                r = pl.multiple_of(g * SUBLANES, SUBLANES) + u
                for k, pos_ref in enumerate((p1_ref, p2_ref)):
                    fn(pltpu.make_async_copy(y_hbm.at[pl.ds(pos_ref[base + r], 1)],
                                             buf.at[slot, k, pl.ds(r, 1)], sem.at[slot]))
            return c
        lax.fori_loop(0, tm // SUBLANES, body, 0)

    @pl.when(i == 0)
    def _():
        for_tile(0, lambda c: c.start())

    @pl.when(i + 1 < pl.num_programs(0))
    def _():
        for_tile(i + 1, lambda c: c.start())

    for_tile(i, lambda c: c.wait())
    w = mf_ref[...]
    slot = i & 1
    o_ref[...] = x_ref[...] + w[:, 0:1] * buf[slot, 0] + w[:, 1:2] * buf[slot, 1]


def combine(pos1, pos2, x, mf, y, tm=256):
    grid_spec = pltpu.PrefetchScalarGridSpec(
        num_scalar_prefetch=2,
        grid=(N_TOK // tm,),
        in_specs=[pl.BlockSpec((tm, D_MODEL), lambda i, p1, p2: (i, 0)),
                  pl.BlockSpec((tm, LANES), lambda i, p1, p2: (i, 0)),
                  pl.BlockSpec(memory_space=pl.ANY)],
        out_specs=pl.BlockSpec((tm, D_MODEL), lambda i, p1, p2: (i, 0)),
        scratch_shapes=[pltpu.VMEM((2, TOP_K, tm, D_MODEL), f32), pltpu.SemaphoreType.DMA((2,))],
    )
    return pl.pallas_call(
        functools.partial(_combine_kernel, tm=tm),
        grid_spec=grid_spec,
        out_shape=jax.ShapeDtypeStruct((N_TOK, D_MODEL), f32),
        compiler_params=_cparams(("arbitrary",)),
        name="combine",
    )(pos1, pos2, x, mf, y)


def _rope_lanes(v):
    z = jnp.zeros(v.shape[:-1] + (QK_ROPE // 2,), v.dtype)
    return jnp.concatenate([v[..., :QK_ROPE // 2], z, v[..., QK_ROPE // 2:], z], axis=-1)


def kernel(x, mem, positions, g_mix, g_ffn, g_mem, w_mem_kv, g_mq, g_mk, w_out, a_w_in, a_conv_w, b_w_in,
           b_g_q_a, b_w_q_b, b_g_qn, g_kv, w_kv_a, g_kv_a, w_kv_b, g_kn, ffn_w_gate, ffn_w_up, ffn_w_down,
           moe_w_router, moe_w_gate, moe_w_up, moe_w_down):
    row = lambda v: v.reshape(1, -1).astype(f32)
    x0 = x.reshape(N_TOK, D_MODEL)
    mem2 = mem.reshape(BATCH * N_MEM, D_MODEL)

    w_kva = jnp.concatenate([w_kv_a[:, :KV_LORA], _rope_lanes(w_kv_a[:, KV_LORA:])], axis=1).astype(bf16)
    kvb = w_kv_b.reshape(KV_LORA, N_HEADS, QK_NOPE + HEAD_DIM)
    w_kvb = jnp.concatenate([kvb[:, :, :QK_NOPE].reshape(KV_LORA, MIX_WIDTH),
                             kvb[:, :, QK_NOPE:].reshape(KV_LORA, MIX_WIDTH)], axis=1).astype(bf16)
    qb = b_w_q_b[0].reshape(Q_LORA, N_HEADS, QK_HEAD)
    w_qb = jnp.concatenate([qb[:, :, :QK_NOPE], _rope_lanes(qb[:, :, QK_NOPE:])], axis=-1)
    w_qb = w_qb.reshape(Q_LORA, N_HEADS * HEAD_PAD).astype(bf16)
    gk_n, gk_r = row(g_kn[:QK_NOPE]), row(_rope_lanes(g_kn[QK_NOPE:]))
    gq_n, gq_r = row(b_g_qn[0, :QK_NOPE]), row(_rope_lanes(b_g_qn[0, QK_NOPE:]))
    w_router_pad = jnp.pad(moe_w_router[0].astype(f32), ((0, 0), (0, LANES - N_EXPERTS)))

    inv_freq = 1.0 / (ROPE_THETA ** (jnp.arange(0, QK_ROPE, 2, dtype=f32) / QK_ROPE))
    half = jnp.ones((QK_ROPE // 2,), f32)
    rope_freq = row(_rope_lanes(jnp.concatenate([inv_freq, inv_freq])))
    rope_sign = row(_rope_lanes(jnp.concatenate([-half, half])))
    pos = positions.reshape(N_TOK, 1).astype(jnp.int32)

    u0 = norm_matmul(x0, row(g_mix[0]), a_w_in[0], tm=1024, tn=1280)
    kvm0 = norm_matmul(mem2, row(g_mem[0]), w_mem_kv[0], tm=BATCH * N_MEM, tn=2 * MEM_WIDTH)
    xa, hp0 = conv_tail(u0, a_conv_w[0].astype(f32), kvm0, row(g_mq[0]), row(g_mk[0]),
                        w_out[0].astype(bf16), x0, row(g_ffn[0]))
    dense_tbl = _super_tiles(jnp.array([N_TOK], jnp.int32), jnp.array([N_TOK], jnp.int32), jnp.array([0], jnp.int32),
                             -(-N_TOK // (FFN_SUB * FFN_SUPER)))
    y0 = swiglu_ffn(hp0, ffn_w_gate, ffn_w_up, ffn_w_down, *dense_tbl)

    x1, k, v, q, um1 = mla_kvq(xa, y0, pos, rope_freq, rope_sign, row(g_kv), w_kva, row(g_kv_a), w_kvb, gk_n, gk_r,
                               row(g_mix[1]), b_w_in[0].astype(bf16), row(b_g_q_a[0]), w_qb, gq_n, gq_r)
    mix1 = flash_attention(q, k, v)
    kvm1 = norm_matmul(mem2, row(g_mem[1]), w_mem_kv[1], tm=BATCH * N_MEM, tn=2 * MEM_WIDTH)
    x2, hp1 = attn_tail(mix1, um1, 0, kvm1, row(g_mq[1]), row(g_mk[1]), w_out[1].astype(bf16), x1, row(g_ffn[1]))

    mi, mf, cnt = router(x2, row(g_ffn[1]), w_router_pad)
    counts = cnt[0, :N_EXPERTS]
    padded = (counts + FFN_SUB - 1) // FFN_SUB * FFN_SUB
    offsets = jnp.cumsum(padded) - padded
    pos1 = offsets[mi[0]] + mi[2]
    pos2 = offsets[mi[1]] + mi[3]
    pad_start = jnp.concatenate([offsets + counts, jnp.sum(padded, keepdims=True)])
    xs = dispatch(pos1, pos2, pad_start, padded - counts, hp1)
    y1 = swiglu_ffn(xs, moe_w_gate[0], moe_w_up[0], moe_w_down[0], *_super_tiles(counts, padded, offsets, MOE_SUPERS))
    out = combine(pos1, pos2, x2, mf, y1)
    return out.reshape(BATCH, SEQ, D_MODEL)
```

```python
import functools

import jax
import jax.numpy as jnp
from jax import lax
from jax.experimental import pallas as pl
from jax.experimental.pallas import tpu as pltpu

D_MODEL = 2048
BATCH = 2
SEQ = 4096
N_TOK = BATCH * SEQ
N_HEADS = 12
HEAD_DIM = 128
MIX_WIDTH = N_HEADS * HEAD_DIM
MEM_HEADS = 4
MEM_WIDTH = MEM_HEADS * HEAD_DIM
N_MEM = 256
CONV_WIDTH = 3
Q_LORA = 512
KV_LORA = 256
QK_NOPE = 128
QK_ROPE = 64
QK_HEAD = QK_NOPE + QK_ROPE
ROPE_THETA = 10000.0
D_FF = 7168
N_EXPERTS = 8
TOP_K = 2
EPS = 1e-6

LANES = 128
SUBLANES = 8
HEAD_PAD = 2 * LANES
HALF_D = D_MODEL // 2
VMEM_LIMIT = 56 * 1024 * 1024

FFN_SUB = 256
FFN_SUPER = 9
FFN_TAIL = 64
FFN_TF = 256
MOE_ROWS = N_TOK * TOP_K + N_EXPERTS * FFN_SUB
MOE_SUPERS = N_EXPERTS + -(-MOE_ROWS // (FFN_SUB * FFN_SUPER))

f32 = jnp.float32
bf16 = jnp.bfloat16


def _cparams(sem):
    return pltpu.CompilerParams(dimension_semantics=sem, vmem_limit_bytes=VMEM_LIMIT)


def _rms(x, g):
    ms = jnp.sum(x * x, axis=-1, keepdims=True) * (1.0 / x.shape[-1])
    return x * lax.rsqrt(ms + EPS) * g


def _pack_halves(h):
    hb = h.astype(bf16).astype(f32)
    return pltpu.pack_elementwise([hb[:, :HALF_D], hb[:, HALF_D:]], packed_dtype=bf16)


def _unpack_halves(p):
    lo = pltpu.unpack_elementwise(p, index=0, packed_dtype=bf16, unpacked_dtype=f32)
    hi = pltpu.unpack_elementwise(p, index=1, packed_dtype=bf16, unpacked_dtype=f32)
    return lo.astype(bf16), hi.astype(bf16)


def _dot(a, b):
    return jnp.dot(a, b, preferred_element_type=f32)


def _dot_nt(a, b):
    return lax.dot_general(a, b, (((1,), (1,)), ((), ())), preferred_element_type=f32)


def _norm_matmul_kernel(x_ref, g_ref, w_ref, o_ref, h_ref):
    @pl.when(pl.program_id(1) == 0)
    def _():
        h_ref[...] = _rms(x_ref[...], g_ref[...]).astype(bf16)

    o_ref[...] = _dot(h_ref[...], w_ref[...].astype(bf16)).astype(o_ref.dtype)


def norm_matmul(x, g, w, tm, tn, out_dtype=bf16):
    m, d = x.shape
    n = w.shape[1]
    return pl.pallas_call(
        _norm_matmul_kernel,
        grid=(m // tm, n // tn),
        in_specs=[pl.BlockSpec((tm, d), lambda i, j: (i, 0)),
                  pl.BlockSpec((1, d), lambda i, j: (0, 0)),
                  pl.BlockSpec((d, tn), lambda i, j: (0, j))],
        out_specs=pl.BlockSpec((tm, tn), lambda i, j: (i, j)),
        out_shape=jax.ShapeDtypeStruct((m, n), out_dtype),
        scratch_shapes=[pltpu.VMEM((tm, d), bf16)],
        compiler_params=_cparams(("arbitrary", "arbitrary")),
        name="norm_matmul",
    )(x, g, w)


def _mem_attention(um_ref, kvm_ref, gq_ref, gk_ref, cat_ref):
    scale = HEAD_DIM ** -0.5
    for h in range(MEM_HEADS):
        cs = slice(h * HEAD_DIM, (h + 1) * HEAD_DIM)
        q = _rms(um_ref[:, cs].astype(f32), gq_ref[...]).astype(bf16)
        k = _rms(kvm_ref[:, cs].astype(f32), gk_ref[...]).astype(bf16)
        v = kvm_ref[:, MEM_WIDTH + h * HEAD_DIM:MEM_WIDTH + (h + 1) * HEAD_DIM]
        s = _dot_nt(q, k) * scale
        p = jnp.exp(s - jnp.max(s, axis=-1, keepdims=True))
        o = _dot(p.astype(bf16), v) / jnp.sum(p, axis=-1, keepdims=True)
        cat_ref[:, MIX_WIDTH + h * HEAD_DIM:MIX_WIDTH + (h + 1) * HEAD_DIM] = o.astype(bf16)


def _tail_epilogue(cat_ref, wout_ref, x_ref, gffn_ref, xo_ref, hp_ref):
    xn = x_ref[...] + _dot(cat_ref[...], wout_ref[...])
    xo_ref[...] = xn
    hp_ref[...] = _pack_halves(_rms(xn, gffn_ref[...]))


def _conv_tail_kernel(xin_ref, gb_ref, gc_ref, um_ref, hx_ref, hg_ref, cw_ref, kvm_ref, gq_ref, gk_ref,
                      wout_ref, x_ref, gffn_ref, xo_ref, hp_ref, cat_ref, *, tm):
    i = pl.program_id(0)
    first = (i * tm) % SEQ == 0
    rows = lax.broadcasted_iota(jnp.int32, (tm, 1), 0)
    cc = 512
    for c in range(MIX_WIDTH // cc):
        cs = slice(c * cc, (c + 1) * cc)
        z = gc_ref[:, cs].astype(f32) * xin_ref[:, cs].astype(f32)
        zh = hg_ref[:, cs].astype(f32) * hx_ref[:, cs].astype(f32)
        zh = jnp.where(first, 0.0, zh)
        p1 = zh[15:16, :]
        p2 = zh[14:15, :]
        z1 = jnp.where(rows == 0, p1, pltpu.roll(z, 1, 0))
        z2 = jnp.where(rows == 0, p2, jnp.where(rows == 1, p1, pltpu.roll(z, 2, 0)))
        y = cw_ref[0:1, cs] * z2 + cw_ref[1:2, cs] * z1 + cw_ref[2:3, cs] * z
        cat_ref[:, cs] = (gb_ref[:, cs].astype(f32) * y).astype(bf16)
    _mem_attention(um_ref, kvm_ref, gq_ref, gk_ref, cat_ref)
    _tail_epilogue(cat_ref, wout_ref, x_ref, gffn_ref, xo_ref, hp_ref)


def _attn_tail_kernel(mix_ref, um_ref, kvm_ref, gq_ref, gk_ref, wout_ref, x_ref, gffn_ref,
                      xo_ref, hp_ref, cat_ref):
    cat_ref[:, :MIX_WIDTH] = mix_ref[...]
    _mem_attention(um_ref, kvm_ref, gq_ref, gk_ref, cat_ref)
    _tail_epilogue(cat_ref, wout_ref, x_ref, gffn_ref, xo_ref, hp_ref)


def _tail_common_specs(tm):
    per_seq = SEQ // tm
    return [pl.BlockSpec((N_MEM, 2 * MEM_WIDTH), lambda i: (i // per_seq, 0)),
            pl.BlockSpec((1, HEAD_DIM), lambda i: (0, 0)),
            pl.BlockSpec((1, HEAD_DIM), lambda i: (0, 0)),
            pl.BlockSpec((D_MODEL, D_MODEL), lambda i: (0, 0), pipeline_mode=pl.Buffered(1)),
            pl.BlockSpec((tm, D_MODEL), lambda i: (i, 0)),
            pl.BlockSpec((1, D_MODEL), lambda i: (0, 0))]


def _tail_outs(tm):
    return dict(
        out_specs=[pl.BlockSpec((tm, D_MODEL), lambda i: (i, 0)),
                   pl.BlockSpec((tm, HALF_D), lambda i: (i, 0))],
        out_shape=[jax.ShapeDtypeStruct((N_TOK, D_MODEL), f32),
                   jax.ShapeDtypeStruct((N_TOK, HALF_D), jnp.uint32)],
        scratch_shapes=[pltpu.VMEM((tm, D_MODEL), bf16)],
        compiler_params=_cparams(("arbitrary",)),
    )


def conv_tail(u, conv_w, kvm, g_mq, g_mk, w_out, x, g_ffn, tm=512):
    halo = 16
    um_blk = 3 * MIX_WIDTH // MEM_WIDTH
    prev = lambda i: jnp.maximum(i * (tm // halo) - 1, 0)
    in_specs = [pl.BlockSpec((tm, MIX_WIDTH), lambda i: (i, 0)),
                pl.BlockSpec((tm, MIX_WIDTH), lambda i: (i, 1)),
                pl.BlockSpec((tm, MIX_WIDTH), lambda i: (i, 2)),
                pl.BlockSpec((tm, MEM_WIDTH), lambda i: (i, um_blk)),
                pl.BlockSpec((halo, MIX_WIDTH), lambda i: (prev(i), 0)),
                pl.BlockSpec((halo, MIX_WIDTH), lambda i: (prev(i), 2)),
                pl.BlockSpec((CONV_WIDTH, MIX_WIDTH), lambda i: (0, 0))] + _tail_common_specs(tm)
    return pl.pallas_call(
        functools.partial(_conv_tail_kernel, tm=tm),
        grid=(N_TOK // tm,), in_specs=in_specs, name="conv_tail", **_tail_outs(tm),
    )(u, u, u, u, u, u, conv_w, kvm, g_mq, g_mk, w_out, x, g_ffn)


def attn_tail(mix, um_src, um_blk, kvm, g_mq, g_mk, w_out, x, g_ffn, tm=512):
    in_specs = [pl.BlockSpec((tm, MIX_WIDTH), lambda i: (i, 0)),
                pl.BlockSpec((tm, MEM_WIDTH), lambda i: (i, um_blk))] + _tail_common_specs(tm)
    return pl.pallas_call(
        _attn_tail_kernel,
        grid=(N_TOK // tm,), in_specs=in_specs, name="attn_tail", **_tail_outs(tm),
    )(mix, um_src, kvm, g_mq, g_mk, w_out, x, g_ffn)


def _ffn_kernel(exp_ref, row0_ref, nsub_ref, work_ref, tail_ref, xs_hbm, wg_ref, wu_ref, wd_ref, y_hbm,
                x_s, acc, wg_s, wu_s, wd_s, sem):
    s = pl.program_id(0)
    f = pl.program_id(1)
    nsub = nsub_ref[s]
    row0 = row0_ref[s]
    per_sub = FFN_SUB // FFN_TAIL
    nfull = lax.shift_right_logical(work_ref[s], per_sub.bit_length() - 1)
    ntail = work_ref[s] & (per_sub - 1)
    last_f = f == pl.num_programs(1) - 1
    last_step = (s == pl.num_programs(0) - 1) & last_f

    def x_copy(j):
        r = pl.multiple_of(j * FFN_SUB, FFN_SUB)
        g = pl.multiple_of(row0 + r, FFN_SUB)
        return pltpu.make_async_copy(xs_hbm.at[pl.ds(g, FFN_SUB)], x_s.at[pl.ds(r, FFN_SUB)], sem)

    def y_copy(r, rows):
        g = pl.multiple_of(row0 + r, FFN_SUB)
        return pltpu.make_async_copy(acc.at[pl.ds(r, rows)], y_hbm.at[pl.ds(g, rows)], sem)

    def for_subtiles(fn):
        def body(j, c):
            fn(j)
            return c
        lax.fori_loop(0, nsub, body, 0)

    @pl.when(f == 0)
    def _():
        for_subtiles(lambda j: x_copy(j).start())

        def zero(j):
            r = pl.multiple_of(j * FFN_SUB, FFN_SUB)
            acc[pl.ds(r, FFN_SUB), :] = jnp.zeros((FFN_SUB, D_MODEL), f32)
        for_subtiles(zero)
        for_subtiles(lambda j: x_copy(j).wait())

    def whole_subtiles(rows):
        return -(-rows // FFN_SUB) * FFN_SUB

    def swiglu_rows(r, rows, wg, wu, wd):
        lo, hi = _unpack_halves(x_s[pl.ds(r, rows), :])
        g = _dot(lo, wg[:HALF_D, :]) + _dot(hi, wg[HALF_D:, :])
        u = _dot(lo, wu[:HALF_D, :]) + _dot(hi, wu[HALF_D:, :])
        a = (g * jax.nn.sigmoid(g) * u).astype(bf16)
        acc[pl.ds(r, rows), :] += _dot(a, wd[...])

        @pl.when(last_f)
        def _():
            y_copy(r, whole_subtiles(rows)).start()

    n_first = jnp.where(nfull >= 4, 4 + (nfull & 1), 1)
    rem = nfull - n_first
    n_quad = lax.shift_right_logical(rem, 2)
    first_rows = pl.multiple_of(n_first * FFN_SUB, FFN_SUB)
    quad_row = lambda j: pl.multiple_of(first_rows + j * 4 * FFN_SUB, FFN_SUB)
    pair_row = quad_row(n_quad)
    last_row = pl.multiple_of((nfull - 1) * FFN_SUB, FFN_SUB)
    tail_row = pl.multiple_of(nfull * FFN_SUB, FFN_SUB)

    def for_groups(first, fn):
        for n in (1, 4, 5):
            @pl.when(n_first == n)
            def _():
                first(n * FFN_SUB)

        def body(j, c):
            fn(quad_row(j), 4 * FFN_SUB)
            return c
        lax.fori_loop(0, n_quad, body, 0)

        @pl.when((rem & 2) != 0)
        def _():
            fn(pair_row, 2 * FFN_SUB)

        @pl.when((rem & 1) != 0)
        def _():
            fn(last_row, FFN_SUB)

        for n in range(1, per_sub):
            @pl.when(ntail == n)
            def _():
                fn(tail_row, n * FFN_TAIL)

    def first_group(rows):
        wg = wg_ref[0].astype(bf16)
        wu = wu_ref[0].astype(bf16)
        wd = wd_ref[0].astype(bf16)
        wg_s[...] = wg
        wu_s[...] = wu
        wd_s[...] = wd
        swiglu_rows(0, rows, wg, wu, wd)

    for_groups(first_group, lambda r, rows: swiglu_rows(r, rows, wg_s, wu_s, wd_s))

    @pl.when(last_f)
    def _():
        for_groups(lambda rows: y_copy(0, rows).wait(), lambda r, rows: y_copy(r, whole_subtiles(rows)).wait())

    @pl.when(last_step)
    def _():
        acc[pl.ds(0, FFN_SUB), :] = jnp.zeros((FFN_SUB, D_MODEL), f32)

        def tail_copy(j):
            g = pl.multiple_of(j * FFN_SUB, FFN_SUB)
            return pltpu.make_async_copy(acc.at[pl.ds(0, FFN_SUB)], y_hbm.at[pl.ds(g, FFN_SUB)], sem)

        n_tail = y_hbm.shape[0] // FFN_SUB
        lax.fori_loop(tail_ref[0], n_tail, lambda j, c: (tail_copy(j).start(), c)[1], 0)
        lax.fori_loop(tail_ref[0], n_tail, lambda j, c: (tail_copy(j).wait(), c)[1], 0)


def swiglu_ffn(xs, w_gate, w_up, w_down, st_expert, st_row0, st_nsub, st_work, st_tail):
    rows = xs.shape[0]
    nf = D_FF // FFN_TF

    grid_spec = pltpu.PrefetchScalarGridSpec(
        num_scalar_prefetch=5,
        grid=(st_tail[1], nf),
        in_specs=[pl.BlockSpec(memory_space=pl.ANY),
                  pl.BlockSpec((1, D_MODEL, FFN_TF), lambda s, f, e, *_: (e[s], 0, f)),
                  pl.BlockSpec((1, D_MODEL, FFN_TF), lambda s, f, e, *_: (e[s], 0, f)),
                  pl.BlockSpec((1, FFN_TF, D_MODEL), lambda s, f, e, *_: (e[s], f, 0))],
        out_specs=pl.BlockSpec(memory_space=pl.ANY),
        scratch_shapes=[pltpu.VMEM((FFN_SUPER * FFN_SUB, HALF_D), jnp.uint32),
                        pltpu.VMEM((FFN_SUPER * FFN_SUB, D_MODEL), f32),
                        pltpu.VMEM((D_MODEL, FFN_TF), bf16),
                        pltpu.VMEM((D_MODEL, FFN_TF), bf16),
                        pltpu.VMEM((FFN_TF, D_MODEL), bf16),
                        pltpu.SemaphoreType.DMA(())],
    )
    return pl.pallas_call(
        _ffn_kernel,
        grid_spec=grid_spec,
        out_shape=jax.ShapeDtypeStruct((rows, D_MODEL), f32),
        compiler_params=_cparams(("arbitrary", "arbitrary")),
        name="swiglu_ffn",
    )(st_expert, st_row0, st_nsub, st_work, st_tail, xs, w_gate, w_up, w_down)


def _super_tiles(counts, padded, offsets, n_super):
    cap = FFN_SUB * FFN_SUPER
    per_group = (padded + cap - 1) // cap
    ends = jnp.cumsum(per_group)
    t = jnp.arange(n_super, dtype=jnp.int32)
    grp = jnp.minimum(jnp.sum((t[:, None] >= ends[None, :]).astype(jnp.int32), axis=1), padded.shape[0] - 1)
    k = t - (ends[grp] - per_group[grp])
    used = t < ends[-1]
    nsub = jnp.where(used, jnp.minimum(FFN_SUPER, (padded[grp] - k * cap) // FFN_SUB), 0)
    row0 = jnp.where(used, offsets[grp] + k * cap, 0)
    valid = jnp.clip(counts[grp] - k * cap, 0, nsub * FFN_SUB)
    work = jnp.where(used, jnp.maximum((valid + FFN_TAIL - 1) // FFN_TAIL, FFN_SUB // FFN_TAIL), 0)
    last_grp = grp[jnp.maximum(ends[-1] - 1, 0)]
    grp = jnp.where(used, grp, last_grp)
    tail = jnp.stack([jnp.sum(padded) // FFN_SUB, ends[-1]])
    return (grp.astype(jnp.int32), row0.astype(jnp.int32), nsub.astype(jnp.int32), work.astype(jnp.int32),
            tail.astype(jnp.int32))


def _rope(x, cos, sin):
    return x * cos + pltpu.roll(x, LANES // 2, 1) * sin


def _dot_split(a, sel):
    hi = a.astype(bf16)
    lo = (a - hi.astype(f32)).astype(bf16)
    return _dot(hi, sel) + _dot(lo, sel)


def _mla_kvq_kernel(x0_ref, y_ref, pos_ref, freq_ref, sign_ref, gkv_ref, wkva_ref, gkva_ref, wkvb_ref, gkn_ref, gkr_ref,
                    gmix_ref, win_ref, gqa_ref, wqb_ref, gqn_ref, gqr_ref, sel_ref,
                    x1_ref, k_ref, v_ref, q_ref, um_ref):
    x1 = x0_ref[...] + y_ref[...]
    x1_ref[...] = x1
    ang = pos_ref[...].astype(f32) * freq_ref[...]
    cos = jnp.cos(ang)
    sin = jnp.sin(ang) * sign_ref[...]

    tm = x1.shape[0]
    kva = _dot(_rms(x1, gkv_ref[...]).astype(bf16), wkva_ref[...])
    u = _dot(_rms(x1, gmix_ref[...]).astype(bf16), win_ref[...])
    um_ref[...] = u[:, Q_LORA:].astype(bf16)
    c_kv = _rms(kva[:, :KV_LORA], gkva_ref[...]).astype(bf16)
    kv = _dot(c_kv, wkvb_ref[...])
    c_q = _rms(u[:, :Q_LORA], gqa_ref[...]).astype(bf16)
    q = _dot(c_q, wqb_ref[...])
    v_ref[...] = kv[:, MIX_WIDTH:].astype(bf16)

    kpe = kva[:, KV_LORA:]
    ss_pe = jnp.sum(kpe * kpe, axis=-1, keepdims=True)
    sck = lax.rsqrt((_dot_split(kv[:, :MIX_WIDTH] * kv[:, :MIX_WIDTH], sel_ref[...]) + ss_pe) * (1.0 / QK_HEAD) + EPS)
    q2 = jnp.concatenate([q[:, h * HEAD_PAD:h * HEAD_PAD + LANES] * q[:, h * HEAD_PAD:h * HEAD_PAD + LANES]
                          + q[:, h * HEAD_PAD + LANES:(h + 1) * HEAD_PAD] * q[:, h * HEAD_PAD + LANES:(h + 1) * HEAD_PAD]
                          for h in range(N_HEADS)], axis=-1)
    scq = lax.rsqrt(_dot_split(q2, sel_ref[...]) * (1.0 / QK_HEAD) + EPS)
    k_rot = _rope(kpe * gkr_ref[...], cos, sin)
    for h in range(N_HEADS):
        sch = jnp.broadcast_to(sck[:, h:h + 1], (tm, LANES))
        k_ref[:, h * HEAD_PAD:h * HEAD_PAD + LANES] = (
            kv[:, h * HEAD_DIM:(h + 1) * HEAD_DIM] * sch * gkn_ref[...]).astype(bf16)
        k_ref[:, h * HEAD_PAD + LANES:(h + 1) * HEAD_PAD] = (k_rot * sch).astype(bf16)
        sch = jnp.broadcast_to(scq[:, h:h + 1], (tm, LANES))
        qn = q[:, h * HEAD_PAD:h * HEAD_PAD + LANES]
        qr = q[:, h * HEAD_PAD + LANES:(h + 1) * HEAD_PAD]
        q_ref[:, h * HEAD_PAD:h * HEAD_PAD + LANES] = (qn * sch * gqn_ref[...]).astype(bf16)
        q_ref[:, h * HEAD_PAD + LANES:(h + 1) * HEAD_PAD] = (_rope(qr * gqr_ref[...], cos, sin) * sch).astype(bf16)


def mla_kvq(x0, y, pos, rope_freq, rope_sign, g_kv, w_kva, g_kva, w_kvb, gk_n, gk_r, g_mix, w_in, g_qa, w_qb, gq_n, gq_r, tm=256):
    row = lambda w: pl.BlockSpec((tm, w), lambda i: (i, 0))
    full = lambda a: pl.BlockSpec(a.shape, lambda i: (0, 0))
    head = jnp.arange(MIX_WIDTH, dtype=jnp.int32) // HEAD_DIM
    sel = (head[:, None] == jnp.arange(LANES, dtype=jnp.int32)[None, :]).astype(bf16)
    consts = (rope_freq, rope_sign, g_kv, w_kva, g_kva, w_kvb, gk_n, gk_r, g_mix, w_in, g_qa, w_qb, gq_n, gq_r, sel)
    return pl.pallas_call(
        _mla_kvq_kernel,
        grid=(N_TOK // tm,),
        in_specs=[row(D_MODEL), row(D_MODEL), row(1)] + [full(a) for a in consts],
        out_specs=[row(D_MODEL), row(N_HEADS * HEAD_PAD), row(MIX_WIDTH), row(N_HEADS * HEAD_PAD), row(MEM_WIDTH)],
        out_shape=[jax.ShapeDtypeStruct((N_TOK, D_MODEL), f32),
                   jax.ShapeDtypeStruct((N_TOK, N_HEADS * HEAD_PAD), bf16),
                   jax.ShapeDtypeStruct((N_TOK, MIX_WIDTH), bf16),
                   jax.ShapeDtypeStruct((N_TOK, N_HEADS * HEAD_PAD), bf16),
                   jax.ShapeDtypeStruct((N_TOK, MEM_WIDTH), bf16)],
        compiler_params=_cparams(("arbitrary",)),
        name="mla_kvq",
    )(x0, y, pos, *consts)


def _flash_kernel(q_ref, k_ref, v_ref, o_ref, m_s, l_s, acc_s, *, tq, heads):
    qi = pl.program_id(2)
    c_exp = (QK_HEAD ** -0.5) * 1.4426950408889634
    m_s[...] = jnp.full(m_s.shape, -jnp.inf, f32)
    l_s[...] = jnp.zeros(l_s.shape, f32)
    acc_s[...] = jnp.zeros(acc_s.shape, f32)
    n_blk = tq // LANES

    def chunk(kj, masked):
        r = pl.multiple_of(kj * tq, tq)

        def scores(h):
            q = q_ref[:, h * HEAD_PAD:(h + 1) * HEAD_PAD]
            s = _dot_nt(q, k_ref[pl.ds(r, tq), h * HEAD_PAD:(h + 1) * HEAD_PAD])
            if masked:
                qpos = lax.broadcasted_iota(jnp.int32, (tq, tq), 0)
                kpos = lax.broadcasted_iota(jnp.int32, (tq, tq), 1)
                s = jnp.where(kpos <= qpos, s, -jnp.inf)
            return s

        def update(h, s):
            blocks = [s[:, b * LANES:(b + 1) * LANES] for b in range(n_blk)]
            lane_max = functools.reduce(jnp.maximum, blocks)
            m_old = m_s[h]
            m_new = jnp.maximum(m_old, jnp.max(lane_max, axis=-1, keepdims=True))
            alpha = jnp.exp2((m_old - m_new) * c_exp)
            p = [jnp.exp2((blk - m_new) * c_exp) for blk in blocks]
            l_s[h] = alpha * l_s[h] + functools.reduce(jnp.add, p)
            pv = _dot(jnp.concatenate(p, axis=-1).astype(bf16),
                      v_ref[pl.ds(r, tq), h * HEAD_DIM:(h + 1) * HEAD_DIM])
            acc_s[h] = alpha * acc_s[h] + pv
            m_s[h] = m_new

        ahead = 1
        pending = [scores(h) for h in range(min(ahead, heads))]
        for h in range(heads):
            if h + ahead < heads:
                pending.append(scores(h + ahead))
            update(h, pending[h])

    def body(kj, c):
        chunk(kj, False)
        return c

    lax.fori_loop(0, qi, body, 0)
    chunk(qi, True)
    for h in range(heads):
        l = jnp.sum(l_s[h], axis=-1, keepdims=True)
        o_ref[:, h * HEAD_DIM:(h + 1) * HEAD_DIM] = (acc_s[h] / l).astype(o_ref.dtype)


def flash_attention(q, k, v, tq=512, heads=6):
    per_seq = SEQ // tq
    return pl.pallas_call(
        functools.partial(_flash_kernel, tq=tq, heads=heads),
        grid=(BATCH, N_HEADS // heads, per_seq),
        in_specs=[pl.BlockSpec((tq, heads * HEAD_PAD), lambda b, h, i: (b * per_seq + i, h)),
                  pl.BlockSpec((SEQ, heads * HEAD_PAD), lambda b, h, i: (b, h)),
                  pl.BlockSpec((SEQ, heads * HEAD_DIM), lambda b, h, i: (b, h))],
        out_specs=pl.BlockSpec((tq, heads * HEAD_DIM), lambda b, h, i: (b * per_seq + i, h)),
        out_shape=jax.ShapeDtypeStruct((N_TOK, MIX_WIDTH), bf16),
        scratch_shapes=[pltpu.VMEM((heads, tq, LANES), f32),
                        pltpu.VMEM((heads, tq, LANES), f32),
                        pltpu.VMEM((heads, tq, HEAD_DIM), f32)],
        compiler_params=_cparams(("arbitrary", "arbitrary", "arbitrary")),
        name="flash_attention",
    )(q, k, v)


def _router_kernel(x_ref, g_ref, wr_ref, mi_ref, mf_ref, cnt_ref, carry, *, tm):
    i = pl.program_id(0)

    @pl.when(i == 0)
    def _():
        carry[...] = jnp.zeros_like(carry)

    h = _rms(x_ref[...], g_ref[...])
    w = wr_ref[...]
    h_hi, w_hi = h.astype(bf16), w.astype(bf16)
    h_lo, w_lo = (h - h_hi.astype(f32)).astype(bf16), (w - w_hi.astype(f32)).astype(bf16)
    logits = _dot(h_hi, w_hi) + (_dot(h_hi, w_lo) + _dot(h_lo, w_hi))
    lane = lax.broadcasted_iota(jnp.int32, (tm, LANES), 1)
    logits = jnp.where(lane < N_EXPERTS, logits, -jnp.inf)
    lane_f = lane.astype(f32)
    v1 = jnp.max(logits, axis=-1, keepdims=True)
    e1 = jnp.min(jnp.where(logits == v1, lane_f, float(LANES)), axis=-1, keepdims=True).astype(jnp.int32)
    rest = jnp.where(lane == e1, -jnp.inf, logits)
    v2 = jnp.max(rest, axis=-1, keepdims=True)
    e2 = jnp.min(jnp.where(rest == v2, lane_f, float(LANES)), axis=-1, keepdims=True).astype(jnp.int32)
    t = jnp.exp(v2 - v1)
    w1 = 1.0 / (1.0 + t)
    w2 = t / (1.0 + t)

    hot = jnp.where((lane == e1) | (lane == e2), 1.0, 0.0)
    r_io = lax.broadcasted_iota(jnp.int32, (tm, tm), 0)
    c_io = lax.broadcasted_iota(jnp.int32, (tm, tm), 1)
    below = jnp.where(c_io < r_io, 1.0, 0.0).astype(bf16)
    rank = _dot(below, hot.astype(bf16)) + carry[...]
    carry[...] += jnp.sum(hot, axis=0, keepdims=True)
    rank1 = jnp.sum(jnp.where(lane == e1, rank, 0.0), axis=-1, keepdims=True).astype(jnp.int32)
    rank2 = jnp.sum(jnp.where(lane == e2, rank, 0.0), axis=-1, keepdims=True).astype(jnp.int32)

    meta = jnp.where(lane == 0, e1, jnp.where(lane == 1, e2, jnp.where(lane == 2, rank1,
                     jnp.where(lane == 3, rank2, 0))))
    mi_ref[...] = meta.T[:8, :]
    mf_ref[...] = jnp.where(lane == 0, w1, jnp.where(lane == 1, w2, 0.0))
    cnt_ref[...] = carry[...].astype(jnp.int32)


def router(x, g, w_router_pad, tm=512):
    return pl.pallas_call(
        functools.partial(_router_kernel, tm=tm),
        grid=(N_TOK // tm,),
        in_specs=[pl.BlockSpec((tm, D_MODEL), lambda i: (i, 0)),
                  pl.BlockSpec((1, D_MODEL), lambda i: (0, 0)),
                  pl.BlockSpec((D_MODEL, LANES), lambda i: (0, 0))],
        out_specs=[pl.BlockSpec((8, tm), lambda i: (0, i)),
                   pl.BlockSpec((tm, LANES), lambda i: (i, 0)),
                   pl.BlockSpec((1, LANES), lambda i: (0, 0))],
        out_shape=[jax.ShapeDtypeStruct((8, N_TOK), jnp.int32),
                   jax.ShapeDtypeStruct((N_TOK, LANES), f32),
                   jax.ShapeDtypeStruct((1, LANES), jnp.int32)],
        scratch_shapes=[pltpu.VMEM((1, LANES), f32)],
        compiler_params=_cparams(("arbitrary",)),
        name="router",
    )(x, g, w_router_pad)


def _dispatch_kernel(p1_ref, p2_ref, pad0_ref, padn_ref, hp_ref, xs_out, zrow, sem, *, tm):
    base = pl.program_id(0) * tm

    @pl.when(pl.program_id(0) == 0)
    def _():
        zrow[...] = jnp.zeros(zrow.shape, zrow.dtype)

        def pad_copy(e, i):
            return pltpu.make_async_copy(zrow.at[pl.ds(0, 1)], xs_out.at[pl.ds(pad0_ref[e] + i, 1)], sem)

        def tail_copy(j):
            r = pl.multiple_of(j * FFN_SUB, FFN_SUB)
            return pltpu.make_async_copy(zrow, xs_out.at[pl.ds(r, FFN_SUB)], sem)

        tail0 = lax.shift_right_logical(pad0_ref[N_EXPERTS], FFN_SUB.bit_length() - 1)
        n_tail = xs_out.shape[0] // FFN_SUB
        for e in range(N_EXPERTS):
            lax.fori_loop(0, padn_ref[e], lambda i, c: (pad_copy(e, i).start(), c)[1], 0)
        lax.fori_loop(tail0, n_tail, lambda j, c: (tail_copy(j).start(), c)[1], 0)
        for e in range(N_EXPERTS):
            lax.fori_loop(0, padn_ref[e], lambda i, c: (pad_copy(e, i).wait(), c)[1], 0)
        lax.fori_loop(tail0, n_tail, lambda j, c: (tail_copy(j).wait(), c)[1], 0)

    def copy(g, u, pos_ref):
        r = pl.multiple_of(g * SUBLANES, SUBLANES) + u
        return pltpu.make_async_copy(hp_ref.at[pl.ds(r, 1)], xs_out.at[pl.ds(pos_ref[base + r], 1)], sem)

    def start(g, c):
        for u in range(SUBLANES):
            copy(g, u, p1_ref).start()
            copy(g, u, p2_ref).start()
        return c

    def wait(g, c):
        for u in range(SUBLANES):
            copy(g, u, p1_ref).wait()
            copy(g, u, p2_ref).wait()
        return c

    lax.fori_loop(0, tm // SUBLANES, start, 0)
    lax.fori_loop(0, tm // SUBLANES, wait, 0)


def dispatch(pos1, pos2, pad_start, pad_len, hp, tm=512):
    grid_spec = pltpu.PrefetchScalarGridSpec(
        num_scalar_prefetch=4,
        grid=(N_TOK // tm,),
        in_specs=[pl.BlockSpec((tm, HALF_D), lambda i, *_: (i, 0))],
        out_specs=pl.BlockSpec(memory_space=pl.ANY),
        scratch_shapes=[pltpu.VMEM((FFN_SUB, HALF_D), jnp.uint32), pltpu.SemaphoreType.DMA(())],
    )
    return pl.pallas_call(
        functools.partial(_dispatch_kernel, tm=tm),
        grid_spec=grid_spec,
        out_shape=jax.ShapeDtypeStruct((MOE_ROWS, HALF_D), jnp.uint32),
        compiler_params=_cparams(("arbitrary",)),
        name="dispatch",
    )(pos1, pos2, pad_start, pad_len, hp)


def _combine_kernel(p1_ref, p2_ref, x_ref, mf_ref, y_hbm, o_ref, buf, sem, *, tm):
    i = pl.program_id(0)

    def for_tile(t, fn):
        slot = t & 1
        base = t * tm

        def body(g, c):
            for u in range(SUBLANES):
                r = pl.multiple_of(g * SUBLANES, SUBLANES) + u
                for k, pos_ref in enumerate((p1_ref, p2_ref)):
                    fn(pltpu.make_async_copy(y_hbm.at[pl.ds(pos_ref[base + r], 1)],
                                             buf.at[slot, k, pl.ds(r, 1)], sem.at[slot]))
            return c
        lax.fori_loop(0, tm // SUBLANES, body, 0)

    @pl.when(i == 0)
    def _():
        for_tile(0, lambda c: c.start())

    @pl.when(i + 1 < pl.num_programs(0))
    def _():
        for_tile(i + 1, lambda c: c.start())

    for_tile(i, lambda c: c.wait())
    w = mf_ref[...]
    slot = i & 1
    o_ref[...] = x_ref[...] + w[:, 0:1] * buf[slot, 0] + w[:, 1:2] * buf[slot, 1]


def combine(pos1, pos2, x, mf, y, tm=256):
    grid_spec = pltpu.PrefetchScalarGridSpec(
        num_scalar_prefetch=2,
        grid=(N_TOK // tm,),
        in_specs=[pl.BlockSpec((tm, D_MODEL), lambda i, p1, p2: (i, 0)),
                  pl.BlockSpec((tm, LANES), lambda i, p1, p2: (i, 0)),
                  pl.BlockSpec(memory_space=pl.ANY)],
        out_specs=pl.BlockSpec((tm, D_MODEL), lambda i, p1, p2: (i, 0)),
        scratch_shapes=[pltpu.VMEM((2, TOP_K, tm, D_MODEL), f32), pltpu.SemaphoreType.DMA((2,))],
    )
    return pl.pallas_call(
        functools.partial(_combine_kernel, tm=tm),
        grid_spec=grid_spec,
        out_shape=jax.ShapeDtypeStruct((N_TOK, D_MODEL), f32),
        compiler_params=_cparams(("arbitrary",)),
        name="combine",
    )(pos1, pos2, x, mf, y)


def _rope_lanes(v):
    z = jnp.zeros(v.shape[:-1] + (QK_ROPE // 2,), v.dtype)
    return jnp.concatenate([v[..., :QK_ROPE // 2], z, v[..., QK_ROPE // 2:], z], axis=-1)


def kernel(x, mem, positions, g_mix, g_ffn, g_mem, w_mem_kv, g_mq, g_mk, w_out, a_w_in, a_conv_w, b_w_in,
           b_g_q_a, b_w_q_b, b_g_qn, g_kv, w_kv_a, g_kv_a, w_kv_b, g_kn, ffn_w_gate, ffn_w_up, ffn_w_down,
           moe_w_router, moe_w_gate, moe_w_up, moe_w_down):
    assert x.shape == (BATCH, SEQ, D_MODEL) and mem.shape == (BATCH, N_MEM, D_MODEL), (x.shape, mem.shape)
    assert positions.shape == (BATCH, SEQ) and moe_w_gate.shape == (1, N_EXPERTS, D_MODEL, D_FF)
    assert ffn_w_gate.shape == (1, D_MODEL, D_FF) and a_w_in.shape == (1, D_MODEL, 3 * MIX_WIDTH + MEM_WIDTH)
    row = lambda v: v.reshape(1, -1).astype(f32)
    x0 = x.reshape(N_TOK, D_MODEL)
    mem2 = mem.reshape(BATCH * N_MEM, D_MODEL)

    w_kva = jnp.concatenate([w_kv_a[:, :KV_LORA], _rope_lanes(w_kv_a[:, KV_LORA:])], axis=1).astype(bf16)
    kvb = w_kv_b.reshape(KV_LORA, N_HEADS, QK_NOPE + HEAD_DIM)
    w_kvb = jnp.concatenate([kvb[:, :, :QK_NOPE].reshape(KV_LORA, MIX_WIDTH),
                             kvb[:, :, QK_NOPE:].reshape(KV_LORA, MIX_WIDTH)], axis=1).astype(bf16)
    qb = b_w_q_b[0].reshape(Q_LORA, N_HEADS, QK_HEAD)
    w_qb = jnp.concatenate([qb[:, :, :QK_NOPE], _rope_lanes(qb[:, :, QK_NOPE:])], axis=-1)
    w_qb = w_qb.reshape(Q_LORA, N_HEADS * HEAD_PAD).astype(bf16)
    gk_n, gk_r = row(g_kn[:QK_NOPE]), row(_rope_lanes(g_kn[QK_NOPE:]))
    gq_n, gq_r = row(b_g_qn[0, :QK_NOPE]), row(_rope_lanes(b_g_qn[0, QK_NOPE:]))
    w_router_pad = jnp.pad(moe_w_router[0].astype(f32), ((0, 0), (0, LANES - N_EXPERTS)))

    inv_freq = 1.0 / (ROPE_THETA ** (jnp.arange(0, QK_ROPE, 2, dtype=f32) / QK_ROPE))
    half = jnp.ones((QK_ROPE // 2,), f32)
    rope_freq = row(_rope_lanes(jnp.concatenate([inv_freq, inv_freq])))
    rope_sign = row(_rope_lanes(jnp.concatenate([-half, half])))
    pos = positions.reshape(N_TOK, 1).astype(jnp.int32)

    u0 = norm_matmul(x0, row(g_mix[0]), a_w_in[0], tm=1024, tn=1280)
    kvm0 = norm_matmul(mem2, row(g_mem[0]), w_mem_kv[0], tm=BATCH * N_MEM, tn=2 * MEM_WIDTH)
    xa, hp0 = conv_tail(u0, a_conv_w[0].astype(f32), kvm0, row(g_mq[0]), row(g_mk[0]),
                        w_out[0].astype(bf16), x0, row(g_ffn[0]))
    dense_tbl = _super_tiles(jnp.array([N_TOK], jnp.int32), jnp.array([N_TOK], jnp.int32), jnp.array([0], jnp.int32),
                             -(-N_TOK // (FFN_SUB * FFN_SUPER)))
    y0 = swiglu_ffn(hp0, ffn_w_gate, ffn_w_up, ffn_w_down, *dense_tbl)

    x1, k, v, q, um1 = mla_kvq(xa, y0, pos, rope_freq, rope_sign, row(g_kv), w_kva, row(g_kv_a), w_kvb, gk_n, gk_r,
                               row(g_mix[1]), b_w_in[0].astype(bf16), row(b_g_q_a[0]), w_qb, gq_n, gq_r)
    mix1 = flash_attention(q, k, v)
    kvm1 = norm_matmul(mem2, row(g_mem[1]), w_mem_kv[1], tm=BATCH * N_MEM, tn=2 * MEM_WIDTH)
    x2, hp1 = attn_tail(mix1, um1, 0, kvm1, row(g_mq[1]), row(g_mk[1]), w_out[1].astype(bf16), x1, row(g_ffn[1]))

    mi, mf, cnt = router(x2, row(g_ffn[1]), w_router_pad)
    counts = cnt[0, :N_EXPERTS]
    padded = (counts + FFN_SUB - 1) // FFN_SUB * FFN_SUB
    offsets = jnp.cumsum(padded) - padded
    pos1 = offsets[mi[0]] + mi[2]
    pos2 = offsets[mi[1]] + mi[3]
    pad_start = jnp.concatenate([offsets + counts, jnp.sum(padded, keepdims=True)])
    xs = dispatch(pos1, pos2, pad_start, padded - counts, hp1)
    y1 = swiglu_ffn(xs, moe_w_gate[0], moe_w_up[0], moe_w_down[0], *_super_tiles(counts, padded, offsets, MOE_SUPERS))
    out = combine(pos1, pos2, x2, mf, y1)
    return out.reshape(BATCH, SEQ, D_MODEL)
```

```python
import functools

import jax
import jax.numpy as jnp
from jax import lax
from jax.experimental import pallas as pl
from jax.experimental.pallas import tpu as pltpu

D_MODEL = 2048
BATCH = 2
SEQ = 4096
N_TOK = BATCH * SEQ
N_HEADS = 12
HEAD_DIM = 128
MIX_WIDTH = N_HEADS * HEAD_DIM
MEM_HEADS = 4
MEM_WIDTH = MEM_HEADS * HEAD_DIM
N_MEM = 256
CONV_WIDTH = 3
Q_LORA = 512
KV_LORA = 256
QK_NOPE = 128
QK_ROPE = 64
QK_HEAD = QK_NOPE + QK_ROPE
ROPE_THETA = 10000.0
D_FF = 7168
N_EXPERTS = 8
TOP_K = 2
EPS = 1e-6

LANES = 128
SUBLANES = 8
HEAD_PAD = 2 * LANES
HALF_D = D_MODEL // 2
VMEM_LIMIT = 56 * 1024 * 1024

FFN_SUB = 256
FFN_SUPER = 9
FFN_TAIL = 64
FFN_TF = 256
MOE_ROWS = N_TOK * TOP_K + N_EXPERTS * FFN_SUB
MOE_SUPERS = N_EXPERTS + -(-MOE_ROWS // (FFN_SUB * FFN_SUPER))

f32 = jnp.float32
bf16 = jnp.bfloat16


def _cparams(sem):
    return pltpu.CompilerParams(dimension_semantics=sem, vmem_limit_bytes=VMEM_LIMIT)


def _rms(x, g):
    ms = jnp.sum(x * x, axis=-1, keepdims=True) * (1.0 / x.shape[-1])
    return x * lax.rsqrt(ms + EPS) * g


def _pack_halves(h):
    hb = h.astype(bf16).astype(f32)
    return pltpu.pack_elementwise([hb[:, :HALF_D], hb[:, HALF_D:]], packed_dtype=bf16)


def _unpack_halves(p):
    lo = pltpu.unpack_elementwise(p, index=0, packed_dtype=bf16, unpacked_dtype=f32)
    hi = pltpu.unpack_elementwise(p, index=1, packed_dtype=bf16, unpacked_dtype=f32)
    return lo.astype(bf16), hi.astype(bf16)


def _dot(a, b):
    return jnp.dot(a, b, preferred_element_type=f32)


def _dot_nt(a, b):
    return lax.dot_general(a, b, (((1,), (1,)), ((), ())), preferred_element_type=f32)


def _norm_matmul_kernel(x_ref, g_ref, w_ref, o_ref, h_ref):
    @pl.when(pl.program_id(1) == 0)
    def _():
        h = _rms(x_ref[...], g_ref[...]).astype(bf16)
        h_ref[...] = h
        o_ref[...] = _dot(h, w_ref[...].astype(bf16)).astype(o_ref.dtype)

    @pl.when(pl.program_id(1) != 0)
    def _():
        o_ref[...] = _dot(h_ref[...], w_ref[...].astype(bf16)).astype(o_ref.dtype)


def norm_matmul(x, g, w, tm, tn, out_dtype=bf16):
    m, d = x.shape
    n = w.shape[1]
    return pl.pallas_call(
        _norm_matmul_kernel,
        grid=(m // tm, n // tn),
        in_specs=[pl.BlockSpec((tm, d), lambda i, j: (i, 0)),
                  pl.BlockSpec((1, d), lambda i, j: (0, 0)),
                  pl.BlockSpec((d, tn), lambda i, j: (0, j))],
        out_specs=pl.BlockSpec((tm, tn), lambda i, j: (i, j)),
        out_shape=jax.ShapeDtypeStruct((m, n), out_dtype),
        scratch_shapes=[pltpu.VMEM((tm, d), bf16)],
        compiler_params=_cparams(("arbitrary", "arbitrary")),
        name="norm_matmul",
    )(x, g, w)


def _mem_attention(um_ref, kvm_ref, gq_ref, gk_ref, cat_ref):
    scale = HEAD_DIM ** -0.5
    for h in range(MEM_HEADS):
        cs = slice(h * HEAD_DIM, (h + 1) * HEAD_DIM)
        q = _rms(um_ref[:, cs].astype(f32), gq_ref[...]).astype(bf16)
        k = _rms(kvm_ref[:, cs].astype(f32), gk_ref[...]).astype(bf16)
        v = kvm_ref[:, MEM_WIDTH + h * HEAD_DIM:MEM_WIDTH + (h + 1) * HEAD_DIM]
        s = _dot_nt(q, k) * scale
        p = jnp.exp(s - jnp.max(s, axis=-1, keepdims=True))
        o = _dot(p.astype(bf16), v) / jnp.sum(p, axis=-1, keepdims=True)
        cat_ref[:, MIX_WIDTH + h * HEAD_DIM:MIX_WIDTH + (h + 1) * HEAD_DIM] = o.astype(bf16)


def _tail_epilogue(cat_ref, wout_ref, x_ref, gffn_ref, xo_ref, hp_ref):
    xn = x_ref[...] + _dot(cat_ref[...], wout_ref[...])
    xo_ref[...] = xn
    hp_ref[...] = _pack_halves(_rms(xn, gffn_ref[...]))


def _conv_tail_kernel(xin_ref, gb_ref, gc_ref, um_ref, hx_ref, hg_ref, cw_ref, kvm_ref, gq_ref, gk_ref,
                      wout_ref, x_ref, gffn_ref, xo_ref, hp_ref, cat_ref, *, tm):
    i = pl.program_id(0)
    first = (i * tm) % SEQ == 0
    rows = lax.broadcasted_iota(jnp.int32, (tm, 1), 0)
    cc = 512
    for c in range(MIX_WIDTH // cc):
        cs = slice(c * cc, (c + 1) * cc)
        z = gc_ref[:, cs].astype(f32) * xin_ref[:, cs].astype(f32)
        zh = hg_ref[:, cs].astype(f32) * hx_ref[:, cs].astype(f32)
        zh = jnp.where(first, 0.0, zh)
        p1 = zh[15:16, :]
        p2 = zh[14:15, :]
        z1 = jnp.where(rows == 0, p1, pltpu.roll(z, 1, 0))
        z2 = jnp.where(rows == 0, p2, jnp.where(rows == 1, p1, pltpu.roll(z, 2, 0)))
        y = cw_ref[0:1, cs] * z2 + cw_ref[1:2, cs] * z1 + cw_ref[2:3, cs] * z
        cat_ref[:, cs] = (gb_ref[:, cs].astype(f32) * y).astype(bf16)
    _mem_attention(um_ref, kvm_ref, gq_ref, gk_ref, cat_ref)
    _tail_epilogue(cat_ref, wout_ref, x_ref, gffn_ref, xo_ref, hp_ref)


def _attn_tail_kernel(mix_ref, um_ref, kvm_ref, gq_ref, gk_ref, wout_ref, x_ref, gffn_ref,
                      xo_ref, hp_ref, cat_ref):
    cat_ref[:, :MIX_WIDTH] = mix_ref[...]
    _mem_attention(um_ref, kvm_ref, gq_ref, gk_ref, cat_ref)
    _tail_epilogue(cat_ref, wout_ref, x_ref, gffn_ref, xo_ref, hp_ref)


def _tail_common_specs(tm):
    per_seq = SEQ // tm
    return [pl.BlockSpec((N_MEM, 2 * MEM_WIDTH), lambda i: (i // per_seq, 0)),
            pl.BlockSpec((1, HEAD_DIM), lambda i: (0, 0)),
            pl.BlockSpec((1, HEAD_DIM), lambda i: (0, 0)),
            pl.BlockSpec((D_MODEL, D_MODEL), lambda i: (0, 0), pipeline_mode=pl.Buffered(1)),
            pl.BlockSpec((tm, D_MODEL), lambda i: (i, 0)),
            pl.BlockSpec((1, D_MODEL), lambda i: (0, 0))]


def _tail_outs(tm):
    return dict(
        out_specs=[pl.BlockSpec((tm, D_MODEL), lambda i: (i, 0)),
                   pl.BlockSpec((tm, HALF_D), lambda i: (i, 0))],
        out_shape=[jax.ShapeDtypeStruct((N_TOK, D_MODEL), f32),
                   jax.ShapeDtypeStruct((N_TOK, HALF_D), jnp.uint32)],
        scratch_shapes=[pltpu.VMEM((tm, D_MODEL), bf16)],
        compiler_params=_cparams(("arbitrary",)),
    )


def conv_tail(u, conv_w, kvm, g_mq, g_mk, w_out, x, g_ffn, tm=512):
    halo = 16
    um_blk = 3 * MIX_WIDTH // MEM_WIDTH
    prev = lambda i: jnp.maximum(i * (tm // halo) - 1, 0)
    in_specs = [pl.BlockSpec((tm, MIX_WIDTH), lambda i: (i, 0)),
                pl.BlockSpec((tm, MIX_WIDTH), lambda i: (i, 1)),
                pl.BlockSpec((tm, MIX_WIDTH), lambda i: (i, 2)),
                pl.BlockSpec((tm, MEM_WIDTH), lambda i: (i, um_blk)),
                pl.BlockSpec((halo, MIX_WIDTH), lambda i: (prev(i), 0)),
                pl.BlockSpec((halo, MIX_WIDTH), lambda i: (prev(i), 2)),
                pl.BlockSpec((CONV_WIDTH, MIX_WIDTH), lambda i: (0, 0))] + _tail_common_specs(tm)
    return pl.pallas_call(
        functools.partial(_conv_tail_kernel, tm=tm),
        grid=(N_TOK // tm,), in_specs=in_specs, name="conv_tail", **_tail_outs(tm),
    )(u, u, u, u, u, u, conv_w, kvm, g_mq, g_mk, w_out, x, g_ffn)


def attn_tail(mix, um_src, um_blk, kvm, g_mq, g_mk, w_out, x, g_ffn, tm=512):
    in_specs = [pl.BlockSpec((tm, MIX_WIDTH), lambda i: (i, 0)),
                pl.BlockSpec((tm, MEM_WIDTH), lambda i: (i, um_blk))] + _tail_common_specs(tm)
    return pl.pallas_call(
        _attn_tail_kernel,
        grid=(N_TOK // tm,), in_specs=in_specs, name="attn_tail", **_tail_outs(tm),
    )(mix, um_src, kvm, g_mq, g_mk, w_out, x, g_ffn)


def _ffn_kernel(exp_ref, row0_ref, nsub_ref, work_ref, tail_ref, xs_hbm, wg_ref, wu_ref, wd_ref, y_hbm,
                x_s, acc, wg_s, wu_s, wd_s, sem):
    s = pl.program_id(0)
    f = pl.program_id(1)
    nsub = nsub_ref[s]
    row0 = row0_ref[s]
    per_sub = FFN_SUB // FFN_TAIL
    nfull = lax.shift_right_logical(work_ref[s], per_sub.bit_length() - 1)
    ntail = work_ref[s] & (per_sub - 1)
    last_f = f == pl.num_programs(1) - 1
    last_step = (s == pl.num_programs(0) - 1) & last_f

    def x_copy(j):
        r = pl.multiple_of(j * FFN_SUB, FFN_SUB)
        g = pl.multiple_of(row0 + r, FFN_SUB)
        return pltpu.make_async_copy(xs_hbm.at[pl.ds(g, FFN_SUB)], x_s.at[pl.ds(r, FFN_SUB)], sem)

    def y_copy(r, rows):
        g = pl.multiple_of(row0 + r, FFN_SUB)
        return pltpu.make_async_copy(acc.at[pl.ds(r, rows)], y_hbm.at[pl.ds(g, rows)], sem)

    def for_subtiles(fn):
        def body(j, c):
            fn(j)
            return c
        lax.fori_loop(0, nsub, body, 0)

    @pl.when(f == 0)
    def _():
        for_subtiles(lambda j: x_copy(j).start())

        def zero(j):
            r = pl.multiple_of(j * FFN_SUB, FFN_SUB)
            acc[pl.ds(r, FFN_SUB), :] = jnp.zeros((FFN_SUB, D_MODEL), f32)
        for_subtiles(zero)
        for_subtiles(lambda j: x_copy(j).wait())

    def whole_subtiles(rows):
        return -(-rows // FFN_SUB) * FFN_SUB

    def swiglu_rows(r, rows, wg, wu, wd):
        lo, hi = _unpack_halves(x_s[pl.ds(r, rows), :])
        g = _dot(lo, wg[:HALF_D, :]) + _dot(hi, wg[HALF_D:, :])
        u = _dot(lo, wu[:HALF_D, :]) + _dot(hi, wu[HALF_D:, :])
        a = (g * jax.nn.sigmoid(g) * u).astype(bf16)
        acc[pl.ds(r, rows), :] += _dot(a, wd[...])

        @pl.when(last_f)
        def _():
            y_copy(r, whole_subtiles(rows)).start()

    n_first = jnp.where(nfull >= 4, 4 + (nfull & 1), 1)
    rem = nfull - n_first
    n_quad = lax.shift_right_logical(rem, 2)
    first_rows = pl.multiple_of(n_first * FFN_SUB, FFN_SUB)
    quad_row = lambda j: pl.multiple_of(first_rows + j * 4 * FFN_SUB, FFN_SUB)
    pair_row = quad_row(n_quad)
    last_row = pl.multiple_of((nfull - 1) * FFN_SUB, FFN_SUB)
    tail_row = pl.multiple_of(nfull * FFN_SUB, FFN_SUB)

    def for_groups(first, fn):
        for n in (1, 4, 5):
            @pl.when(n_first == n)
            def _():
                first(n * FFN_SUB)

        def body(j, c):
            fn(quad_row(j), 4 * FFN_SUB)
            return c
        lax.fori_loop(0, n_quad, body, 0)

        @pl.when((rem & 2) != 0)
        def _():
            fn(pair_row, 2 * FFN_SUB)

        @pl.when((rem & 1) != 0)
        def _():
            fn(last_row, FFN_SUB)

        for n in range(1, per_sub):
            @pl.when(ntail == n)
            def _():
                fn(tail_row, n * FFN_TAIL)

    def first_group(rows):
        wg = wg_ref[0].astype(bf16)
        wu = wu_ref[0].astype(bf16)
        wd = wd_ref[0].astype(bf16)
        wg_s[...] = wg
        wu_s[...] = wu
        wd_s[...] = wd
        swiglu_rows(0, rows, wg, wu, wd)

    for_groups(first_group, lambda r, rows: swiglu_rows(r, rows, wg_s, wu_s, wd_s))

    @pl.when(last_f)
    def _():
        for_groups(lambda rows: y_copy(0, rows).wait(), lambda r, rows: y_copy(r, whole_subtiles(rows)).wait())

    @pl.when(last_step)
    def _():
        acc[pl.ds(0, FFN_SUB), :] = jnp.zeros((FFN_SUB, D_MODEL), f32)

        def tail_copy(j):
            g = pl.multiple_of(j * FFN_SUB, FFN_SUB)
            return pltpu.make_async_copy(acc.at[pl.ds(0, FFN_SUB)], y_hbm.at[pl.ds(g, FFN_SUB)], sem)

        n_tail = y_hbm.shape[0] // FFN_SUB
        lax.fori_loop(tail_ref[0], n_tail, lambda j, c: (tail_copy(j).start(), c)[1], 0)
        lax.fori_loop(tail_ref[0], n_tail, lambda j, c: (tail_copy(j).wait(), c)[1], 0)


def swiglu_ffn(xs, w_gate, w_up, w_down, st_expert, st_row0, st_nsub, st_work, st_tail):
    rows = xs.shape[0]
    nf = D_FF // FFN_TF

    grid_spec = pltpu.PrefetchScalarGridSpec(
        num_scalar_prefetch=5,
        grid=(st_tail[1], nf),
        in_specs=[pl.BlockSpec(memory_space=pl.ANY),
                  pl.BlockSpec((1, D_MODEL, FFN_TF), lambda s, f, e, *_: (e[s], 0, f)),
                  pl.BlockSpec((1, D_MODEL, FFN_TF), lambda s, f, e, *_: (e[s], 0, f)),
                  pl.BlockSpec((1, FFN_TF, D_MODEL), lambda s, f, e, *_: (e[s], f, 0))],
        out_specs=pl.BlockSpec(memory_space=pl.ANY),
        scratch_shapes=[pltpu.VMEM((FFN_SUPER * FFN_SUB, HALF_D), jnp.uint32),
                        pltpu.VMEM((FFN_SUPER * FFN_SUB, D_MODEL), f32),
                        pltpu.VMEM((D_MODEL, FFN_TF), bf16),
                        pltpu.VMEM((D_MODEL, FFN_TF), bf16),
                        pltpu.VMEM((FFN_TF, D_MODEL), bf16),
                        pltpu.SemaphoreType.DMA(())],
    )
    return pl.pallas_call(
        _ffn_kernel,
        grid_spec=grid_spec,
        out_shape=jax.ShapeDtypeStruct((rows, D_MODEL), f32),
        compiler_params=_cparams(("arbitrary", "arbitrary")),
        name="swiglu_ffn",
    )(st_expert, st_row0, st_nsub, st_work, st_tail, xs, w_gate, w_up, w_down)


def _super_tiles(counts, padded, offsets, n_super):
    cap = FFN_SUB * FFN_SUPER
    per_group = (padded + cap - 1) // cap
    ends = jnp.cumsum(per_group)
    t = jnp.arange(n_super, dtype=jnp.int32)
    grp = jnp.minimum(jnp.sum((t[:, None] >= ends[None, :]).astype(jnp.int32), axis=1), padded.shape[0] - 1)
    k = t - (ends[grp] - per_group[grp])
    used = t < ends[-1]
    nsub = jnp.where(used, jnp.minimum(FFN_SUPER, (padded[grp] - k * cap) // FFN_SUB), 0)
    row0 = jnp.where(used, offsets[grp] + k * cap, 0)
    valid = jnp.clip(counts[grp] - k * cap, 0, nsub * FFN_SUB)
    work = jnp.where(used, jnp.maximum((valid + FFN_TAIL - 1) // FFN_TAIL, FFN_SUB // FFN_TAIL), 0)
    last_grp = grp[jnp.maximum(ends[-1] - 1, 0)]
    grp = jnp.where(used, grp, last_grp)
    tail = jnp.stack([jnp.sum(padded) // FFN_SUB, ends[-1]])
    return (grp.astype(jnp.int32), row0.astype(jnp.int32), nsub.astype(jnp.int32), work.astype(jnp.int32),
            tail.astype(jnp.int32))


def _rope(x, cos, sin):
    return x * cos + pltpu.roll(x, LANES // 2, 1) * sin


def _dot_split(a, sel):
    hi = a.astype(bf16)
    lo = (a - hi.astype(f32)).astype(bf16)
    return _dot(hi, sel) + _dot(lo, sel)


def _mla_kvq_kernel(x0_ref, y_ref, pos_ref, freq_ref, sign_ref, gkv_ref, wkva_ref, gkva_ref, wkvb_ref, gkn_ref, gkr_ref,
                    gmix_ref, win_ref, gqa_ref, wqb_ref, gqn_ref, gqr_ref, sel_ref,
                    x1_ref, k_ref, v_ref, q_ref, um_ref):
    x1 = x0_ref[...] + y_ref[...]
    x1_ref[...] = x1
    ang = pos_ref[...].astype(f32) * freq_ref[...]
    cos = jnp.cos(ang)
    sin = jnp.sin(ang) * sign_ref[...]

    tm = x1.shape[0]
    kva = _dot(_rms(x1, gkv_ref[...]).astype(bf16), wkva_ref[...])
    u = _dot(_rms(x1, gmix_ref[...]).astype(bf16), win_ref[...])
    um_ref[...] = u[:, Q_LORA:].astype(bf16)
    c_kv = _rms(kva[:, :KV_LORA], gkva_ref[...]).astype(bf16)
    kv = _dot(c_kv, wkvb_ref[...])
    c_q = _rms(u[:, :Q_LORA], gqa_ref[...]).astype(bf16)
    q = _dot(c_q, wqb_ref[...])
    v_ref[...] = kv[:, MIX_WIDTH:].astype(bf16)

    kpe = kva[:, KV_LORA:]
    ss_pe = jnp.sum(kpe * kpe, axis=-1, keepdims=True)
    sck = lax.rsqrt((_dot_split(kv[:, :MIX_WIDTH] * kv[:, :MIX_WIDTH], sel_ref[...]) + ss_pe) * (1.0 / QK_HEAD) + EPS)
    q2 = jnp.concatenate([q[:, h * HEAD_PAD:h * HEAD_PAD + LANES] * q[:, h * HEAD_PAD:h * HEAD_PAD + LANES]
                          + q[:, h * HEAD_PAD + LANES:(h + 1) * HEAD_PAD] * q[:, h * HEAD_PAD + LANES:(h + 1) * HEAD_PAD]
                          for h in range(N_HEADS)], axis=-1)
    scq = lax.rsqrt(_dot_split(q2, sel_ref[...]) * (1.0 / QK_HEAD) + EPS)
    k_rot = _rope(kpe * gkr_ref[...], cos, sin)
    for h in range(N_HEADS):
        sch = jnp.broadcast_to(sck[:, h:h + 1], (tm, LANES))
        k_ref[:, h * HEAD_PAD:h * HEAD_PAD + LANES] = (
            kv[:, h * HEAD_DIM:(h + 1) * HEAD_DIM] * sch * gkn_ref[...]).astype(bf16)
        k_ref[:, h * HEAD_PAD + LANES:(h + 1) * HEAD_PAD] = (k_rot * sch).astype(bf16)
        sch = jnp.broadcast_to(scq[:, h:h + 1], (tm, LANES))
        qn = q[:, h * HEAD_PAD:h * HEAD_PAD + LANES]
        qr = q[:, h * HEAD_PAD + LANES:(h + 1) * HEAD_PAD]
        q_ref[:, h * HEAD_PAD:h * HEAD_PAD + LANES] = (qn * sch * gqn_ref[...]).astype(bf16)
        q_ref[:, h * HEAD_PAD + LANES:(h + 1) * HEAD_PAD] = (_rope(qr * gqr_ref[...], cos, sin) * sch).astype(bf16)


def mla_kvq(x0, y, pos, rope_freq, rope_sign, g_kv, w_kva, g_kva, w_kvb, gk_n, gk_r, g_mix, w_in, g_qa, w_qb, gq_n, gq_r, tm=256):
    row = lambda w: pl.BlockSpec((tm, w), lambda i: (i, 0))
    full = lambda a: pl.BlockSpec(a.shape, lambda i: (0, 0))
    head = jnp.arange(MIX_WIDTH, dtype=jnp.int32) // HEAD_DIM
    sel = (head[:, None] == jnp.arange(LANES, dtype=jnp.int32)[None, :]).astype(bf16)
    consts = (rope_freq, rope_sign, g_kv, w_kva, g_kva, w_kvb, gk_n, gk_r, g_mix, w_in, g_qa, w_qb, gq_n, gq_r, sel)
    return pl.pallas_call(
        _mla_kvq_kernel,
        grid=(N_TOK // tm,),
        in_specs=[row(D_MODEL), row(D_MODEL), row(1)] + [full(a) for a in consts],
        out_specs=[row(D_MODEL), row(N_HEADS * HEAD_PAD), row(MIX_WIDTH), row(N_HEADS * HEAD_PAD), row(MEM_WIDTH)],
        out_shape=[jax.ShapeDtypeStruct((N_TOK, D_MODEL), f32),
                   jax.ShapeDtypeStruct((N_TOK, N_HEADS * HEAD_PAD), bf16),
                   jax.ShapeDtypeStruct((N_TOK, MIX_WIDTH), bf16),
                   jax.ShapeDtypeStruct((N_TOK, N_HEADS * HEAD_PAD), bf16),
                   jax.ShapeDtypeStruct((N_TOK, MEM_WIDTH), bf16)],
        compiler_params=_cparams(("arbitrary",)),
        name="mla_kvq",
    )(x0, y, pos, *consts)


def _flash_kernel(q_ref, k_ref, v_ref, o_ref, m_s, l_s, acc_s, *, tq, heads):
    qi = pl.program_id(2)
    c_exp = (QK_HEAD ** -0.5) * 1.4426950408889634
    m_s[...] = jnp.full(m_s.shape, -jnp.inf, f32)
    l_s[...] = jnp.zeros(l_s.shape, f32)
    acc_s[...] = jnp.zeros(acc_s.shape, f32)
    n_blk = tq // LANES

    def chunk(kj, masked):
        r = pl.multiple_of(kj * tq, tq)

        def scores(h):
            q = q_ref[:, h * HEAD_PAD:(h + 1) * HEAD_PAD]
            s = _dot_nt(q, k_ref[pl.ds(r, tq), h * HEAD_PAD:(h + 1) * HEAD_PAD])
            if masked:
                qpos = lax.broadcasted_iota(jnp.int32, (tq, tq), 0)
                kpos = lax.broadcasted_iota(jnp.int32, (tq, tq), 1)
                s = jnp.where(kpos <= qpos, s, -jnp.inf)
            return s

        def update(h, s):
            blocks = [s[:, b * LANES:(b + 1) * LANES] for b in range(n_blk)]
            lane_max = functools.reduce(jnp.maximum, blocks)
            m_old = m_s[h]
            m_new = jnp.maximum(m_old, jnp.max(lane_max, axis=-1, keepdims=True))
            alpha = jnp.exp2((m_old - m_new) * c_exp)
            p = [jnp.exp2((blk - m_new) * c_exp) for blk in blocks]
            l_s[h] = alpha * l_s[h] + functools.reduce(jnp.add, p)
            pv = _dot(jnp.concatenate(p, axis=-1).astype(bf16),
                      v_ref[pl.ds(r, tq), h * HEAD_DIM:(h + 1) * HEAD_DIM])
            acc_s[h] = alpha * acc_s[h] + pv
            m_s[h] = m_new

        ahead = 1
        pending = [scores(h) for h in range(min(ahead, heads))]
        for h in range(heads):
            if h + ahead < heads:
                pending.append(scores(h + ahead))
            update(h, pending[h])

    def body(kj, c):
        chunk(kj, False)
        return c

    lax.fori_loop(0, qi, body, 0)
    chunk(qi, True)
    for h in range(heads):
        l = jnp.sum(l_s[h], axis=-1, keepdims=True)
        o_ref[:, h * HEAD_DIM:(h + 1) * HEAD_DIM] = (acc_s[h] / l).astype(o_ref.dtype)


def flash_attention(q, k, v, tq=512, heads=6):
    per_seq = SEQ // tq
    return pl.pallas_call(
        functools.partial(_flash_kernel, tq=tq, heads=heads),
        grid=(BATCH, N_HEADS // heads, per_seq),
        in_specs=[pl.BlockSpec((tq, heads * HEAD_PAD), lambda b, h, i: (b * per_seq + i, h)),
                  pl.BlockSpec((SEQ, heads * HEAD_PAD), lambda b, h, i: (b, h)),
                  pl.BlockSpec((SEQ, heads * HEAD_DIM), lambda b, h, i: (b, h))],
        out_specs=pl.BlockSpec((tq, heads * HEAD_DIM), lambda b, h, i: (b * per_seq + i, h)),
        out_shape=jax.ShapeDtypeStruct((N_TOK, MIX_WIDTH), bf16),
        scratch_shapes=[pltpu.VMEM((heads, tq, LANES), f32),
                        pltpu.VMEM((heads, tq, LANES), f32),
                        pltpu.VMEM((heads, tq, HEAD_DIM), f32)],
        compiler_params=_cparams(("arbitrary", "arbitrary", "arbitrary")),
        name="flash_attention",
    )(q, k, v)


def _router_kernel(x_ref, g_ref, wr_ref, mi_ref, mf_ref, cnt_ref, carry, *, tm):
    i = pl.program_id(0)

    @pl.when(i == 0)
    def _():
        carry[...] = jnp.zeros_like(carry)

    h = _rms(x_ref[...], g_ref[...])
    w = wr_ref[...]
    h_hi, w_hi = h.astype(bf16), w.astype(bf16)
    h_lo, w_lo = (h - h_hi.astype(f32)).astype(bf16), (w - w_hi.astype(f32)).astype(bf16)
    logits = _dot(h_hi, w_hi) + (_dot(h_hi, w_lo) + _dot(h_lo, w_hi))
    lane = lax.broadcasted_iota(jnp.int32, (tm, LANES), 1)
    logits = jnp.where(lane < N_EXPERTS, logits, -jnp.inf)
    lane_f = lane.astype(f32)
    v1 = jnp.max(logits, axis=-1, keepdims=True)
    e1 = jnp.min(jnp.where(logits == v1, lane_f, float(LANES)), axis=-1, keepdims=True).astype(jnp.int32)
    rest = jnp.where(lane == e1, -jnp.inf, logits)
    v2 = jnp.max(rest, axis=-1, keepdims=True)
    e2 = jnp.min(jnp.where(rest == v2, lane_f, float(LANES)), axis=-1, keepdims=True).astype(jnp.int32)
    t = jnp.exp(v2 - v1)
    w1 = 1.0 / (1.0 + t)
    w2 = t / (1.0 + t)

    hot = jnp.where((lane == e1) | (lane == e2), 1.0, 0.0)
    r_io = lax.broadcasted_iota(jnp.int32, (tm, tm), 0)
    c_io = lax.broadcasted_iota(jnp.int32, (tm, tm), 1)
    below = jnp.where(c_io < r_io, 1.0, 0.0).astype(bf16)
    rank = _dot(below, hot.astype(bf16)) + carry[...]
    carry[...] += jnp.sum(hot, axis=0, keepdims=True)
    rank1 = jnp.sum(jnp.where(lane == e1, rank, 0.0), axis=-1, keepdims=True).astype(jnp.int32)
    rank2 = jnp.sum(jnp.where(lane == e2, rank, 0.0), axis=-1, keepdims=True).astype(jnp.int32)

    meta = jnp.where(lane == 0, e1, jnp.where(lane == 1, e2, jnp.where(lane == 2, rank1,
                     jnp.where(lane == 3, rank2, 0))))
    mi_ref[...] = meta.T[:8, :]
    mf_ref[...] = jnp.where(lane == 0, w1, jnp.where(lane == 1, w2, 0.0))
    cnt_ref[...] = carry[...].astype(jnp.int32)


def router(x, g, w_router_pad, tm=512):
    return pl.pallas_call(
        functools.partial(_router_kernel, tm=tm),
        grid=(N_TOK // tm,),
        in_specs=[pl.BlockSpec((tm, D_MODEL), lambda i: (i, 0)),
                  pl.BlockSpec((1, D_MODEL), lambda i: (0, 0)),
                  pl.BlockSpec((D_MODEL, LANES), lambda i: (0, 0))],
        out_specs=[pl.BlockSpec((8, tm), lambda i: (0, i)),
                   pl.BlockSpec((tm, LANES), lambda i: (i, 0)),
                   pl.BlockSpec((1, LANES), lambda i: (0, 0))],
        out_shape=[jax.ShapeDtypeStruct((8, N_TOK), jnp.int32),
                   jax.ShapeDtypeStruct((N_TOK, LANES), f32),
                   jax.ShapeDtypeStruct((1, LANES), jnp.int32)],
        scratch_shapes=[pltpu.VMEM((1, LANES), f32)],
        compiler_params=_cparams(("arbitrary",)),
        name="router",
    )(x, g, w_router_pad)


def _dispatch_kernel(p1_ref, p2_ref, pad0_ref, padn_ref, hp_ref, xs_out, zrow, sem, *, tm):
    base = pl.program_id(0) * tm

    @pl.when(pl.program_id(0) == 0)
    def _():
        zrow[...] = jnp.zeros(zrow.shape, zrow.dtype)

        def pad_copy(e, i):
            return pltpu.make_async_copy(zrow.at[pl.ds(0, 1)], xs_out.at[pl.ds(pad0_ref[e] + i, 1)], sem)

        def tail_copy(j):
            r = pl.multiple_of(j * FFN_SUB, FFN_SUB)
            return pltpu.make_async_copy(zrow, xs_out.at[pl.ds(r, FFN_SUB)], sem)

        tail0 = lax.shift_right_logical(pad0_ref[N_EXPERTS], FFN_SUB.bit_length() - 1)
        n_tail = xs_out.shape[0] // FFN_SUB
        for e in range(N_EXPERTS):
            lax.fori_loop(0, padn_ref[e], lambda i, c: (pad_copy(e, i).start(), c)[1], 0)
        lax.fori_loop(tail0, n_tail, lambda j, c: (tail_copy(j).start(), c)[1], 0)
        for e in range(N_EXPERTS):
            lax.fori_loop(0, padn_ref[e], lambda i, c: (pad_copy(e, i).wait(), c)[1], 0)
        lax.fori_loop(tail0, n_tail, lambda j, c: (tail_copy(j).wait(), c)[1], 0)

    def copy(g, u, pos_ref):
        r = pl.multiple_of(g * SUBLANES, SUBLANES) + u
        return pltpu.make_async_copy(hp_ref.at[pl.ds(r, 1)], xs_out.at[pl.ds(pos_ref[base + r], 1)], sem)

    def start(g, c):
        for u in range(SUBLANES):
            copy(g, u, p1_ref).start()
            copy(g, u, p2_ref).start()
        return c

    def wait(g, c):
        for u in range(SUBLANES):
            copy(g, u, p1_ref).wait()
            copy(g, u, p2_ref).wait()
        return c

    lax.fori_loop(0, tm // SUBLANES, start, 0)
    lax.fori_loop(0, tm // SUBLANES, wait, 0)


def dispatch(pos1, pos2, pad_start, pad_len, hp, tm=512):
    grid_spec = pltpu.PrefetchScalarGridSpec(
        num_scalar_prefetch=4,
        grid=(N_TOK // tm,),
        in_specs=[pl.BlockSpec((tm, HALF_D), lambda i, *_: (i, 0))],
        out_specs=pl.BlockSpec(memory_space=pl.ANY),
        scratch_shapes=[pltpu.VMEM((FFN_SUB, HALF_D), jnp.uint32), pltpu.SemaphoreType.DMA(())],
    )
    return pl.pallas_call(
        functools.partial(_dispatch_kernel, tm=tm),
        grid_spec=grid_spec,
        out_shape=jax.ShapeDtypeStruct((MOE_ROWS, HALF_D), jnp.uint32),
        compiler_params=_cparams(("arbitrary",)),
        name="dispatch",
    )(pos1, pos2, pad_start, pad_len, hp)


def _combine_kernel(p1_ref, p2_ref, x_ref, mf_ref, y_hbm, o_ref, buf, sem, *, tm):
    i = pl.program_id(0)

    def for_tile(t, fn):
        slot = t & 1
        base = t * tm

        def body(g, c):
            for u in range(SUBLANES):
                r = pl.multiple_of(g * SUBLANES, SUBLANES) + u
                for k, pos_ref in enumerate((p1_ref, p2_ref)):
                    fn(pltpu.make_async_copy(y_hbm.at[pl.ds(pos_ref[base + r], 1)],
                                             buf.at[slot, k, pl.ds(r, 1)], sem.at[slot]))
            return c
        lax.fori_loop(0, tm // SUBLANES, body, 0)

    @pl.when(i == 0)
    def _():
        for_tile(0, lambda c: c.start())

    @pl.when(i + 1 < pl.num_programs(0))
    def _():
        for_tile(i + 1, lambda c: c.start())

    for_tile(i, lambda c: c.wait())
    w = mf_ref[...]
    slot = i & 1
    o_ref[...] = x_ref[...] + w[:, 0:1] * buf[slot, 0] + w[:, 1:2] * buf[slot, 1]


def combine(pos1, pos2, x, mf, y, tm=256):
    grid_spec = pltpu.PrefetchScalarGridSpec(
        num_scalar_prefetch=2,
        grid=(N_TOK // tm,),
        in_specs=[pl.BlockSpec((tm, D_MODEL), lambda i, p1, p2: (i, 0)),
                  pl.BlockSpec((tm, LANES), lambda i, p1, p2: (i, 0)),
                  pl.BlockSpec(memory_space=pl.ANY)],
        out_specs=pl.BlockSpec((tm, D_MODEL), lambda i, p1, p2: (i, 0)),
        scratch_shapes=[pltpu.VMEM((2, TOP_K, tm, D_MODEL), f32), pltpu.SemaphoreType.DMA((2,))],
    )
    return pl.pallas_call(
        functools.partial(_combine_kernel, tm=tm),
        grid_spec=grid_spec,
        out_shape=jax.ShapeDtypeStruct((N_TOK, D_MODEL), f32),
        compiler_params=_cparams(("arbitrary",)),
        name="combine",
    )(pos1, pos2, x, mf, y)


def _rope_lanes(v):
    z = jnp.zeros(v.shape[:-1] + (QK_ROPE // 2,), v.dtype)
    return jnp.concatenate([v[..., :QK_ROPE // 2], z, v[..., QK_ROPE // 2:], z], axis=-1)


def kernel(x, mem, positions, g_mix, g_ffn, g_mem, w_mem_kv, g_mq, g_mk, w_out, a_w_in, a_conv_w, b_w_in,
           b_g_q_a, b_w_q_b, b_g_qn, g_kv, w_kv_a, g_kv_a, w_kv_b, g_kn, ffn_w_gate, ffn_w_up, ffn_w_down,
           moe_w_router, moe_w_gate, moe_w_up, moe_w_down):
    assert x.shape == (BATCH, SEQ, D_MODEL) and mem.shape == (BATCH, N_MEM, D_MODEL), (x.shape, mem.shape)
    assert positions.shape == (BATCH, SEQ) and moe_w_gate.shape == (1, N_EXPERTS, D_MODEL, D_FF)
    assert ffn_w_gate.shape == (1, D_MODEL, D_FF) and a_w_in.shape == (1, D_MODEL, 3 * MIX_WIDTH + MEM_WIDTH)
    row = lambda v: v.reshape(1, -1).astype(f32)
    x0 = x.reshape(N_TOK, D_MODEL)
    mem2 = mem.reshape(BATCH * N_MEM, D_MODEL)

    w_kva = jnp.concatenate([w_kv_a[:, :KV_LORA], _rope_lanes(w_kv_a[:, KV_LORA:])], axis=1).astype(bf16)
    kvb = w_kv_b.reshape(KV_LORA, N_HEADS, QK_NOPE + HEAD_DIM)
    w_kvb = jnp.concatenate([kvb[:, :, :QK_NOPE].reshape(KV_LORA, MIX_WIDTH),
                             kvb[:, :, QK_NOPE:].reshape(KV_LORA, MIX_WIDTH)], axis=1).astype(bf16)
    qb = b_w_q_b[0].reshape(Q_LORA, N_HEADS, QK_HEAD)
    w_qb = jnp.concatenate([qb[:, :, :QK_NOPE], _rope_lanes(qb[:, :, QK_NOPE:])], axis=-1)
    w_qb = w_qb.reshape(Q_LORA, N_HEADS * HEAD_PAD).astype(bf16)
    gk_n, gk_r = row(g_kn[:QK_NOPE]), row(_rope_lanes(g_kn[QK_NOPE:]))
    gq_n, gq_r = row(b_g_qn[0, :QK_NOPE]), row(_rope_lanes(b_g_qn[0, QK_NOPE:]))
    w_router_pad = jnp.pad(moe_w_router[0].astype(f32), ((0, 0), (0, LANES - N_EXPERTS)))

    inv_freq = 1.0 / (ROPE_THETA ** (jnp.arange(0, QK_ROPE, 2, dtype=f32) / QK_ROPE))
    half = jnp.ones((QK_ROPE // 2,), f32)
    rope_freq = row(_rope_lanes(jnp.concatenate([inv_freq, inv_freq])))
    rope_sign = row(_rope_lanes(jnp.concatenate([-half, half])))
    pos = positions.reshape(N_TOK, 1).astype(jnp.int32)

    u0 = norm_matmul(x0, row(g_mix[0]), a_w_in[0], tm=1024, tn=1280)
    kvm0 = norm_matmul(mem2, row(g_mem[0]), w_mem_kv[0], tm=BATCH * N_MEM, tn=2 * MEM_WIDTH)
    xa, hp0 = conv_tail(u0, a_conv_w[0].astype(f32), kvm0, row(g_mq[0]), row(g_mk[0]),
                        w_out[0].astype(bf16), x0, row(g_ffn[0]))
    dense_tbl = _super_tiles(jnp.array([N_TOK], jnp.int32), jnp.array([N_TOK], jnp.int32), jnp.array([0], jnp.int32),
                             -(-N_TOK // (FFN_SUB * FFN_SUPER)))
    y0 = swiglu_ffn(hp0, ffn_w_gate, ffn_w_up, ffn_w_down, *dense_tbl)

    x1, k, v, q, um1 = mla_kvq(xa, y0, pos, rope_freq, rope_sign, row(g_kv), w_kva, row(g_kv_a), w_kvb, gk_n, gk_r,
                               row(g_mix[1]), b_w_in[0].astype(bf16), row(b_g_q_a[0]), w_qb, gq_n, gq_r)
    mix1 = flash_attention(q, k, v)
    kvm1 = norm_matmul(mem2, row(g_mem[1]), w_mem_kv[1], tm=BATCH * N_MEM, tn=2 * MEM_WIDTH)
    x2, hp1 = attn_tail(mix1, um1, 0, kvm1, row(g_mq[1]), row(g_mk[1]), w_out[1].astype(bf16), x1, row(g_ffn[1]))

    mi, mf, cnt = router(x2, row(g_ffn[1]), w_router_pad)
    counts = cnt[0, :N_EXPERTS]
    padded = (counts + FFN_SUB - 1) // FFN_SUB * FFN_SUB
    offsets = jnp.cumsum(padded) - padded
    pos1 = offsets[mi[0]] + mi[2]
    pos2 = offsets[mi[1]] + mi[3]
    pad_start = jnp.concatenate([offsets + counts, jnp.sum(padded, keepdims=True)])
    xs = dispatch(pos1, pos2, pad_start, padded - counts, hp1)
    y1 = swiglu_ffn(xs, moe_w_gate[0], moe_w_up[0], moe_w_down[0], *_super_tiles(counts, padded, offsets, MOE_SUPERS))
    out = combine(pos1, pos2, x2, mf, y1)
    return out.reshape(BATCH, SEQ, D_MODEL)
```

```python
import functools

import jax
import jax.numpy as jnp
from jax import lax
from jax.experimental import pallas as pl
from jax.experimental.pallas import tpu as pltpu

D_MODEL = 2048
BATCH = 2
SEQ = 4096
N_TOK = BATCH * SEQ
N_HEADS = 12
HEAD_DIM = 128
MIX_WIDTH = N_HEADS * HEAD_DIM
MEM_HEADS = 4
MEM_WIDTH = MEM_HEADS * HEAD_DIM
N_MEM = 256
CONV_WIDTH = 3
Q_LORA = 512
KV_LORA = 256
QK_NOPE = 128
QK_ROPE = 64
QK_HEAD = QK_NOPE + QK_ROPE
ROPE_THETA = 10000.0
D_FF = 7168
N_EXPERTS = 8
TOP_K = 2
EPS = 1e-6

LANES = 128
SUBLANES = 8
HEAD_PAD = 2 * LANES
HALF_D = D_MODEL // 2
VMEM_LIMIT = 56 * 1024 * 1024

FFN_SUB = 256
FFN_SUPER = 9
FFN_TAIL = 64
FFN_TF = 256
MOE_ROWS = N_TOK * TOP_K + N_EXPERTS * FFN_SUB
MOE_SUPERS = N_EXPERTS + -(-MOE_ROWS // (FFN_SUB * FFN_SUPER))

f32 = jnp.float32
bf16 = jnp.bfloat16


def _cparams(sem):
    return pltpu.CompilerParams(dimension_semantics=sem, vmem_limit_bytes=VMEM_LIMIT)


def _rms(x, g):
    ms = jnp.sum(x * x, axis=-1, keepdims=True) * (1.0 / x.shape[-1])
    return x * lax.rsqrt(ms + EPS) * g


def _pack_halves(h):
    hb = h.astype(bf16).astype(f32)
    return pltpu.pack_elementwise([hb[:, :HALF_D], hb[:, HALF_D:]], packed_dtype=bf16)


def _unpack_halves(p):
    lo = pltpu.unpack_elementwise(p, index=0, packed_dtype=bf16, unpacked_dtype=f32)
    hi = pltpu.unpack_elementwise(p, index=1, packed_dtype=bf16, unpacked_dtype=f32)
    return lo.astype(bf16), hi.astype(bf16)


def _dot(a, b):
    return jnp.dot(a, b, preferred_element_type=f32)


def _dot_nt(a, b):
    return lax.dot_general(a, b, (((1,), (1,)), ((), ())), preferred_element_type=f32)


def _norm_matmul_kernel(x_ref, g_ref, w_ref, o_ref, h_ref):
    @pl.when(pl.program_id(1) == 0)
    def _():
        h_ref[...] = _rms(x_ref[...], g_ref[...]).astype(bf16)

    o_ref[...] = _dot(h_ref[...], w_ref[...].astype(bf16)).astype(o_ref.dtype)


def norm_matmul(x, g, w, tm, tn, out_dtype=bf16):
    m, d = x.shape
    n = w.shape[1]
    return pl.pallas_call(
        _norm_matmul_kernel,
        grid=(m // tm, n // tn),
        in_specs=[pl.BlockSpec((tm, d), lambda i, j: (i, 0)),
                  pl.BlockSpec((1, d), lambda i, j: (0, 0)),
                  pl.BlockSpec((d, tn), lambda i, j: (0, j))],
        out_specs=pl.BlockSpec((tm, tn), lambda i, j: (i, j)),
        out_shape=jax.ShapeDtypeStruct((m, n), out_dtype),
        scratch_shapes=[pltpu.VMEM((tm, d), bf16)],
        compiler_params=_cparams(("arbitrary", "arbitrary")),
        name="norm_matmul",
    )(x, g, w)


def _mem_attention(um_ref, kvm_ref, gq_ref, gk_ref, cat_ref):
    scale = HEAD_DIM ** -0.5
    for h in range(MEM_HEADS):
        cs = slice(h * HEAD_DIM, (h + 1) * HEAD_DIM)
        q = _rms(um_ref[:, cs].astype(f32), gq_ref[...]).astype(bf16)
        k = _rms(kvm_ref[:, cs].astype(f32), gk_ref[...]).astype(bf16)
        v = kvm_ref[:, MEM_WIDTH + h * HEAD_DIM:MEM_WIDTH + (h + 1) * HEAD_DIM]
        s = _dot_nt(q, k) * scale
        p = jnp.exp(s - jnp.max(s, axis=-1, keepdims=True))
        o = _dot(p.astype(bf16), v) / jnp.sum(p, axis=-1, keepdims=True)
        cat_ref[:, MIX_WIDTH + h * HEAD_DIM:MIX_WIDTH + (h + 1) * HEAD_DIM] = o.astype(bf16)


def _tail_epilogue(cat_ref, wout_ref, x_ref, gffn_ref, xo_ref, hp_ref):
    xn = x_ref[...] + _dot(cat_ref[...], wout_ref[...])
    xo_ref[...] = xn
    hp_ref[...] = _pack_halves(_rms(xn, gffn_ref[...]))


def _conv_tail_kernel(xin_ref, gb_ref, gc_ref, um_ref, hx_ref, hg_ref, cw_ref, kvm_ref, gq_ref, gk_ref,
                      wout_ref, x_ref, gffn_ref, xo_ref, hp_ref, cat_ref, *, tm):
    i = pl.program_id(0)
    first = (i * tm) % SEQ == 0
    rows = lax.broadcasted_iota(jnp.int32, (tm, 1), 0)
    cc = 512
    for c in range(MIX_WIDTH // cc):
        cs = slice(c * cc, (c + 1) * cc)
        z = gc_ref[:, cs].astype(f32) * xin_ref[:, cs].astype(f32)
        zh = hg_ref[:, cs].astype(f32) * hx_ref[:, cs].astype(f32)
        zh = jnp.where(first, 0.0, zh)
        p1 = zh[15:16, :]
        p2 = zh[14:15, :]
        z1 = jnp.where(rows == 0, p1, pltpu.roll(z, 1, 0))
        z2 = jnp.where(rows == 0, p2, jnp.where(rows == 1, p1, pltpu.roll(z, 2, 0)))
        y = cw_ref[0:1, cs] * z2 + cw_ref[1:2, cs] * z1 + cw_ref[2:3, cs] * z
        cat_ref[:, cs] = (gb_ref[:, cs].astype(f32) * y).astype(bf16)
    _mem_attention(um_ref, kvm_ref, gq_ref, gk_ref, cat_ref)
    _tail_epilogue(cat_ref, wout_ref, x_ref, gffn_ref, xo_ref, hp_ref)


def _attn_tail_kernel(mix_ref, um_ref, kvm_ref, gq_ref, gk_ref, wout_ref, x_ref, gffn_ref,
                      xo_ref, hp_ref, cat_ref):
    cat_ref[:, :MIX_WIDTH] = mix_ref[...]
    _mem_attention(um_ref, kvm_ref, gq_ref, gk_ref, cat_ref)
    _tail_epilogue(cat_ref, wout_ref, x_ref, gffn_ref, xo_ref, hp_ref)


def _tail_common_specs(tm):
    per_seq = SEQ // tm
    return [pl.BlockSpec((N_MEM, 2 * MEM_WIDTH), lambda i: (i // per_seq, 0)),
            pl.BlockSpec((1, HEAD_DIM), lambda i: (0, 0)),
            pl.BlockSpec((1, HEAD_DIM), lambda i: (0, 0)),
            pl.BlockSpec((D_MODEL, D_MODEL), lambda i: (0, 0), pipeline_mode=pl.Buffered(1)),
            pl.BlockSpec((tm, D_MODEL), lambda i: (i, 0)),
            pl.BlockSpec((1, D_MODEL), lambda i: (0, 0))]


def _tail_outs(tm):
    return dict(
        out_specs=[pl.BlockSpec((tm, D_MODEL), lambda i: (i, 0)),
                   pl.BlockSpec((tm, HALF_D), lambda i: (i, 0))],
        out_shape=[jax.ShapeDtypeStruct((N_TOK, D_MODEL), f32),
                   jax.ShapeDtypeStruct((N_TOK, HALF_D), jnp.uint32)],
        scratch_shapes=[pltpu.VMEM((tm, D_MODEL), bf16)],
        compiler_params=_cparams(("arbitrary",)),
    )


def conv_tail(u, conv_w, kvm, g_mq, g_mk, w_out, x, g_ffn, tm=512):
    halo = 16
    um_blk = 3 * MIX_WIDTH // MEM_WIDTH
    prev = lambda i: jnp.maximum(i * (tm // halo) - 1, 0)
    in_specs = [pl.BlockSpec((tm, MIX_WIDTH), lambda i: (i, 0)),
                pl.BlockSpec((tm, MIX_WIDTH), lambda i: (i, 1)),
                pl.BlockSpec((tm, MIX_WIDTH), lambda i: (i, 2)),
                pl.BlockSpec((tm, MEM_WIDTH), lambda i: (i, um_blk)),
                pl.BlockSpec((halo, MIX_WIDTH), lambda i: (prev(i), 0)),
                pl.BlockSpec((halo, MIX_WIDTH), lambda i: (prev(i), 2)),
                pl.BlockSpec((CONV_WIDTH, MIX_WIDTH), lambda i: (0, 0))] + _tail_common_specs(tm)
    return pl.pallas_call(
        functools.partial(_conv_tail_kernel, tm=tm),
        grid=(N_TOK // tm,), in_specs=in_specs, name="conv_tail", **_tail_outs(tm),
    )(u, u, u, u, u, u, conv_w, kvm, g_mq, g_mk, w_out, x, g_ffn)


def attn_tail(mix, um_src, um_blk, kvm, g_mq, g_mk, w_out, x, g_ffn, tm=512):
    in_specs = [pl.BlockSpec((tm, MIX_WIDTH), lambda i: (i, 0)),
                pl.BlockSpec((tm, MEM_WIDTH), lambda i: (i, um_blk))] + _tail_common_specs(tm)
    return pl.pallas_call(
        _attn_tail_kernel,
        grid=(N_TOK // tm,), in_specs=in_specs, name="attn_tail", **_tail_outs(tm),
    )(mix, um_src, kvm, g_mq, g_mk, w_out, x, g_ffn)


def _ffn_kernel(exp_ref, row0_ref, nsub_ref, work_ref, tail_ref, xs_hbm, wg_ref, wu_ref, wd_ref, y_hbm,
                x_s, acc, wg_s, wu_s, wd_s, sem):
    s = pl.program_id(0)
    f = pl.program_id(1)
    nsub = nsub_ref[s]
    row0 = row0_ref[s]
    per_sub = FFN_SUB // FFN_TAIL
    nfull = lax.shift_right_logical(work_ref[s], per_sub.bit_length() - 1)
    ntail = work_ref[s] & (per_sub - 1)
    last_f = f == pl.num_programs(1) - 1
    last_step = (s == pl.num_programs(0) - 1) & last_f

    def x_copy(j):
        r = pl.multiple_of(j * FFN_SUB, FFN_SUB)
        g = pl.multiple_of(row0 + r, FFN_SUB)
        return pltpu.make_async_copy(xs_hbm.at[pl.ds(g, FFN_SUB)], x_s.at[pl.ds(r, FFN_SUB)], sem)

    def y_copy(r, rows):
        g = pl.multiple_of(row0 + r, FFN_SUB)
        return pltpu.make_async_copy(acc.at[pl.ds(r, rows)], y_hbm.at[pl.ds(g, rows)], sem)

    def for_subtiles(fn):
        def body(j, c):
            fn(j)
            return c
        lax.fori_loop(0, nsub, body, 0)

    @pl.when(f == 0)
    def _():
        for_subtiles(lambda j: x_copy(j).start())

        def zero(j):
            r = pl.multiple_of(j * FFN_SUB, FFN_SUB)
            acc[pl.ds(r, FFN_SUB), :] = jnp.zeros((FFN_SUB, D_MODEL), f32)
        for_subtiles(zero)
        for_subtiles(lambda j: x_copy(j).wait())

    def whole_subtiles(rows):
        return -(-rows // FFN_SUB) * FFN_SUB

    def swiglu_rows(r, rows, wg, wu, wd):
        lo, hi = _unpack_halves(x_s[pl.ds(r, rows), :])
        g = _dot(lo, wg[:HALF_D, :]) + _dot(hi, wg[HALF_D:, :])
        u = _dot(lo, wu[:HALF_D, :]) + _dot(hi, wu[HALF_D:, :])
        a = (g * jax.nn.sigmoid(g) * u).astype(bf16)
        acc[pl.ds(r, rows), :] += _dot(a, wd[...])

        @pl.when(last_f)
        def _():
            y_copy(r, whole_subtiles(rows)).start()

    n_first = jnp.where(nfull >= 4, 4 + (nfull & 1), 1)
    rem = nfull - n_first
    n_quad = lax.shift_right_logical(rem, 2)
    first_rows = pl.multiple_of(n_first * FFN_SUB, FFN_SUB)
    quad_row = lambda j: pl.multiple_of(first_rows + j * 4 * FFN_SUB, FFN_SUB)
    pair_row = quad_row(n_quad)
    last_row = pl.multiple_of((nfull - 1) * FFN_SUB, FFN_SUB)
    tail_row = pl.multiple_of(nfull * FFN_SUB, FFN_SUB)

    def for_groups(first, fn):
        for n in (1, 4, 5):
            @pl.when(n_first == n)
            def _():
                first(n * FFN_SUB)

        def body(j, c):
            fn(quad_row(j), 4 * FFN_SUB)
            return c
        lax.fori_loop(0, n_quad, body, 0)

        @pl.when((rem & 2) != 0)
        def _():
            fn(pair_row, 2 * FFN_SUB)

        @pl.when((rem & 1) != 0)
        def _():
            fn(last_row, FFN_SUB)

        for n in range(1, per_sub):
            @pl.when(ntail == n)
            def _():
                fn(tail_row, n * FFN_TAIL)

    def first_group(rows):
        wg = wg_ref[0].astype(bf16)
        wu = wu_ref[0].astype(bf16)
        wd = wd_ref[0].astype(bf16)
        wg_s[...] = wg
        wu_s[...] = wu
        wd_s[...] = wd
        swiglu_rows(0, rows, wg, wu, wd)

    for_groups(first_group, lambda r, rows: swiglu_rows(r, rows, wg_s, wu_s, wd_s))

    @pl.when(last_f)
    def _():
        for_groups(lambda rows: y_copy(0, rows).wait(), lambda r, rows: y_copy(r, whole_subtiles(rows)).wait())

    @pl.when(last_step)
    def _():
        acc[pl.ds(0, FFN_SUB), :] = jnp.zeros((FFN_SUB, D_MODEL), f32)

        def tail_copy(j):
            g = pl.multiple_of(j * FFN_SUB, FFN_SUB)
            return pltpu.make_async_copy(acc.at[pl.ds(0, FFN_SUB)], y_hbm.at[pl.ds(g, FFN_SUB)], sem)

        n_tail = y_hbm.shape[0] // FFN_SUB
        lax.fori_loop(tail_ref[0], n_tail, lambda j, c: (tail_copy(j).start(), c)[1], 0)
        lax.fori_loop(tail_ref[0], n_tail, lambda j, c: (tail_copy(j).wait(), c)[1], 0)


def swiglu_ffn(xs, w_gate, w_up, w_down, st_expert, st_row0, st_nsub, st_work, st_tail):
    rows = xs.shape[0]
    nf = D_FF // FFN_TF

    grid_spec = pltpu.PrefetchScalarGridSpec(
        num_scalar_prefetch=5,
        grid=(st_tail[1], nf),
        in_specs=[pl.BlockSpec(memory_space=pl.ANY),
                  pl.BlockSpec((1, D_MODEL, FFN_TF), lambda s, f, e, *_: (e[s], 0, f)),
                  pl.BlockSpec((1, D_MODEL, FFN_TF), lambda s, f, e, *_: (e[s], 0, f)),
                  pl.BlockSpec((1, FFN_TF, D_MODEL), lambda s, f, e, *_: (e[s], f, 0))],
        out_specs=pl.BlockSpec(memory_space=pl.ANY),
        scratch_shapes=[pltpu.VMEM((FFN_SUPER * FFN_SUB, HALF_D), jnp.uint32),
                        pltpu.VMEM((FFN_SUPER * FFN_SUB, D_MODEL), f32),
                        pltpu.VMEM((D_MODEL, FFN_TF), bf16),
                        pltpu.VMEM((D_MODEL, FFN_TF), bf16),
                        pltpu.VMEM((FFN_TF, D_MODEL), bf16),
                        pltpu.SemaphoreType.DMA(())],
    )
    return pl.pallas_call(
        _ffn_kernel,
        grid_spec=grid_spec,
        out_shape=jax.ShapeDtypeStruct((rows, D_MODEL), f32),
        compiler_params=_cparams(("arbitrary", "arbitrary")),
        name="swiglu_ffn",
    )(st_expert, st_row0, st_nsub, st_work, st_tail, xs, w_gate, w_up, w_down)


def _super_tiles(counts, padded, offsets, n_super):
    cap = FFN_SUB * FFN_SUPER
    per_group = (padded + cap - 1) // cap
    ends = jnp.cumsum(per_group)
    t = jnp.arange(n_super, dtype=jnp.int32)
    grp = jnp.minimum(jnp.sum((t[:, None] >= ends[None, :]).astype(jnp.int32), axis=1), padded.shape[0] - 1)
    k = t - (ends[grp] - per_group[grp])
    used = t < ends[-1]
    nsub = jnp.where(used, jnp.minimum(FFN_SUPER, (padded[grp] - k * cap) // FFN_SUB), 0)
    row0 = jnp.where(used, offsets[grp] + k * cap, 0)
    valid = jnp.clip(counts[grp] - k * cap, 0, nsub * FFN_SUB)
    work = jnp.where(used, jnp.maximum((valid + FFN_TAIL - 1) // FFN_TAIL, FFN_SUB // FFN_TAIL), 0)
    last_grp = grp[jnp.maximum(ends[-1] - 1, 0)]
    grp = jnp.where(used, grp, last_grp)
    tail = jnp.stack([jnp.sum(padded) // FFN_SUB, ends[-1]])
    return (grp.astype(jnp.int32), row0.astype(jnp.int32), nsub.astype(jnp.int32), work.astype(jnp.int32),
            tail.astype(jnp.int32))


def _rope(x, cos, sin):
    return x * cos + pltpu.roll(x, LANES // 2, 1) * sin


def _dot_split(a, sel):
    hi = a.astype(bf16)
    lo = (a - hi.astype(f32)).astype(bf16)
    return _dot(hi, sel) + _dot(lo, sel)


def _mla_kvq_kernel(x0_ref, y_ref, pos_ref, freq_ref, sign_ref, gkv_ref, wkva_ref, gkva_ref, wkvb_ref, gkn_ref, gkr_ref,
                    gmix_ref, win_ref, gqa_ref, wqb_ref, gqn_ref, gqr_ref, sel_ref,
                    x1_ref, k_ref, v_ref, q_ref, um_ref):
    x1 = x0_ref[...] + y_ref[...]
    x1_ref[...] = x1
    ang = pos_ref[...].astype(f32) * freq_ref[...]
    cos = jnp.cos(ang)
    sin = jnp.sin(ang) * sign_ref[...]

    tm = x1.shape[0]
    kva = _dot(_rms(x1, gkv_ref[...]).astype(bf16), wkva_ref[...])
    u = _dot(_rms(x1, gmix_ref[...]).astype(bf16), win_ref[...])
    um_ref[...] = u[:, Q_LORA:].astype(bf16)
    c_kv = _rms(kva[:, :KV_LORA], gkva_ref[...]).astype(bf16)
    kv = _dot(c_kv, wkvb_ref[...])
    c_q = _rms(u[:, :Q_LORA], gqa_ref[...]).astype(bf16)
    q = _dot(c_q, wqb_ref[...])
    v_ref[...] = kv[:, MIX_WIDTH:].astype(bf16)

    kpe = kva[:, KV_LORA:]
    ss_pe = jnp.sum(kpe * kpe, axis=-1, keepdims=True)
    sck = lax.rsqrt((_dot_split(kv[:, :MIX_WIDTH] * kv[:, :MIX_WIDTH], sel_ref[...]) + ss_pe) * (1.0 / QK_HEAD) + EPS)
    q2 = jnp.concatenate([q[:, h * HEAD_PAD:h * HEAD_PAD + LANES] * q[:, h * HEAD_PAD:h * HEAD_PAD + LANES]
                          + q[:, h * HEAD_PAD + LANES:(h + 1) * HEAD_PAD] * q[:, h * HEAD_PAD + LANES:(h + 1) * HEAD_PAD]
                          for h in range(N_HEADS)], axis=-1)
    scq = lax.rsqrt(_dot_split(q2, sel_ref[...]) * (1.0 / QK_HEAD) + EPS)
    k_rot = _rope(kpe * gkr_ref[...], cos, sin)
    for h in range(N_HEADS):
        sch = jnp.broadcast_to(sck[:, h:h + 1], (tm, LANES))
        k_ref[:, h * HEAD_PAD:h * HEAD_PAD + LANES] = (
            kv[:, h * HEAD_DIM:(h + 1) * HEAD_DIM] * sch * gkn_ref[...]).astype(bf16)
        k_ref[:, h * HEAD_PAD + LANES:(h + 1) * HEAD_PAD] = (k_rot * sch).astype(bf16)
        sch = jnp.broadcast_to(scq[:, h:h + 1], (tm, LANES))
        qn = q[:, h * HEAD_PAD:h * HEAD_PAD + LANES]
        qr = q[:, h * HEAD_PAD + LANES:(h + 1) * HEAD_PAD]
        q_ref[:, h * HEAD_PAD:h * HEAD_PAD + LANES] = (qn * sch * gqn_ref[...]).astype(bf16)
        q_ref[:, h * HEAD_PAD + LANES:(h + 1) * HEAD_PAD] = (_rope(qr * gqr_ref[...], cos, sin) * sch).astype(bf16)


def mla_kvq(x0, y, pos, rope_freq, rope_sign, g_kv, w_kva, g_kva, w_kvb, gk_n, gk_r, g_mix, w_in, g_qa, w_qb, gq_n, gq_r, tm=256):
    row = lambda w: pl.BlockSpec((tm, w), lambda i: (i, 0))
    full = lambda a: pl.BlockSpec(a.shape, lambda i: (0, 0))
    head = jnp.arange(MIX_WIDTH, dtype=jnp.int32) // HEAD_DIM
    sel = (head[:, None] == jnp.arange(LANES, dtype=jnp.int32)[None, :]).astype(bf16)
    consts = (rope_freq, rope_sign, g_kv, w_kva, g_kva, w_kvb, gk_n, gk_r, g_mix, w_in, g_qa, w_qb, gq_n, gq_r, sel)
    return pl.pallas_call(
        _mla_kvq_kernel,
        grid=(N_TOK // tm,),
        in_specs=[row(D_MODEL), row(D_MODEL), row(1)] + [full(a) for a in consts],
        out_specs=[row(D_MODEL), row(N_HEADS * HEAD_PAD), row(MIX_WIDTH), row(N_HEADS * HEAD_PAD), row(MEM_WIDTH)],
        out_shape=[jax.ShapeDtypeStruct((N_TOK, D_MODEL), f32),
                   jax.ShapeDtypeStruct((N_TOK, N_HEADS * HEAD_PAD), bf16),
                   jax.ShapeDtypeStruct((N_TOK, MIX_WIDTH), bf16),
                   jax.ShapeDtypeStruct((N_TOK, N_HEADS * HEAD_PAD), bf16),
                   jax.ShapeDtypeStruct((N_TOK, MEM_WIDTH), bf16)],
        compiler_params=_cparams(("arbitrary",)),
        name="mla_kvq",
    )(x0, y, pos, *consts)


def _flash_kernel(q_ref, k_ref, v_ref, o_ref, m_s, l_s, acc_s, *, tq, heads):
    qi = pl.program_id(2)
    c_exp = (QK_HEAD ** -0.5) * 1.4426950408889634
    m_s[...] = jnp.full(m_s.shape, -jnp.inf, f32)
    l_s[...] = jnp.zeros(l_s.shape, f32)
    acc_s[...] = jnp.zeros(acc_s.shape, f32)
    n_blk = tq // LANES

    def chunk(kj, masked):
        r = pl.multiple_of(kj * tq, tq)

        def scores(h):
            q = q_ref[:, h * HEAD_PAD:(h + 1) * HEAD_PAD]
            s = _dot_nt(q, k_ref[pl.ds(r, tq), h * HEAD_PAD:(h + 1) * HEAD_PAD])
            if masked:
                qpos = lax.broadcasted_iota(jnp.int32, (tq, tq), 0)
                kpos = lax.broadcasted_iota(jnp.int32, (tq, tq), 1)
                s = jnp.where(kpos <= qpos, s, -jnp.inf)
            return s

        def update(h, s):
            blocks = [s[:, b * LANES:(b + 1) * LANES] for b in range(n_blk)]
            lane_max = functools.reduce(jnp.maximum, blocks)
            m_old = m_s[h]
            m_new = jnp.maximum(m_old, jnp.max(lane_max, axis=-1, keepdims=True))
            alpha = jnp.exp2((m_old - m_new) * c_exp)
            p = [jnp.exp2((blk - m_new) * c_exp) for blk in blocks]
            l_s[h] = alpha * l_s[h] + functools.reduce(jnp.add, p)
            pv = _dot(jnp.concatenate(p, axis=-1).astype(bf16),
                      v_ref[pl.ds(r, tq), h * HEAD_DIM:(h + 1) * HEAD_DIM])
            acc_s[h] = alpha * acc_s[h] + pv
            m_s[h] = m_new

        ahead = 1
        pending = [scores(h) for h in range(min(ahead, heads))]
        for h in range(heads):
            if h + ahead < heads:
                pending.append(scores(h + ahead))
            update(h, pending[h])

    def body(kj, c):
        chunk(kj, False)
        return c

    lax.fori_loop(0, qi, body, 0)
    chunk(qi, True)
    for h in range(heads):
        l = jnp.sum(l_s[h], axis=-1, keepdims=True)
        o_ref[:, h * HEAD_DIM:(h + 1) * HEAD_DIM] = (acc_s[h] / l).astype(o_ref.dtype)


def flash_attention(q, k, v, tq=512, heads=6):
    per_seq = SEQ // tq
    return pl.pallas_call(
        functools.partial(_flash_kernel, tq=tq, heads=heads),
        grid=(BATCH, N_HEADS // heads, per_seq),
        in_specs=[pl.BlockSpec((tq, heads * HEAD_PAD), lambda b, h, i: (b * per_seq + i, h)),
                  pl.BlockSpec((SEQ, heads * HEAD_PAD), lambda b, h, i: (b, h)),
                  pl.BlockSpec((SEQ, heads * HEAD_DIM), lambda b, h, i: (b, h))],
        out_specs=pl.BlockSpec((tq, heads * HEAD_DIM), lambda b, h, i: (b * per_seq + i, h)),
        out_shape=jax.ShapeDtypeStruct((N_TOK, MIX_WIDTH), bf16),
        scratch_shapes=[pltpu.VMEM((heads, tq, LANES), f32),
                        pltpu.VMEM((heads, tq, LANES), f32),
                        pltpu.VMEM((heads, tq, HEAD_DIM), f32)],
        compiler_params=_cparams(("arbitrary", "arbitrary", "arbitrary")),
        name="flash_attention",
    )(q, k, v)


def _router_kernel(x_ref, g_ref, wr_ref, mi_ref, mf_ref, cnt_ref, carry, *, tm):
    i = pl.program_id(0)

    @pl.when(i == 0)
    def _():
        carry[...] = jnp.zeros_like(carry)

    h = _rms(x_ref[...], g_ref[...])
    w = wr_ref[...]
    h_hi, w_hi = h.astype(bf16), w.astype(bf16)
    h_lo, w_lo = (h - h_hi.astype(f32)).astype(bf16), (w - w_hi.astype(f32)).astype(bf16)
    logits = _dot(h_hi, w_hi) + (_dot(h_hi, w_lo) + _dot(h_lo, w_hi))
    lane = lax.broadcasted_iota(jnp.int32, (tm, LANES), 1)
    logits = jnp.where(lane < N_EXPERTS, logits, -jnp.inf)
    lane_f = lane.astype(f32)
    v1 = jnp.max(logits, axis=-1, keepdims=True)
    e1 = jnp.min(jnp.where(logits == v1, lane_f, float(LANES)), axis=-1, keepdims=True).astype(jnp.int32)
    rest = jnp.where(lane == e1, -jnp.inf, logits)
    v2 = jnp.max(rest, axis=-1, keepdims=True)
    e2 = jnp.min(jnp.where(rest == v2, lane_f, float(LANES)), axis=-1, keepdims=True).astype(jnp.int32)
    t = jnp.exp(v2 - v1)
    w1 = 1.0 / (1.0 + t)
    w2 = t / (1.0 + t)

    hot = jnp.where((lane == e1) | (lane == e2), 1.0, 0.0)
    r_io = lax.broadcasted_iota(jnp.int32, (tm, tm), 0)
    c_io = lax.broadcasted_iota(jnp.int32, (tm, tm), 1)
    below = jnp.where(c_io < r_io, 1.0, 0.0).astype(bf16)
    rank = _dot(below, hot.astype(bf16)) + carry[...]
    carry[...] += jnp.sum(hot, axis=0, keepdims=True)
    rank1 = jnp.sum(jnp.where(lane == e1, rank, 0.0), axis=-1, keepdims=True).astype(jnp.int32)
    rank2 = jnp.sum(jnp.where(lane == e2, rank, 0.0), axis=-1, keepdims=True).astype(jnp.int32)

    meta = jnp.where(lane == 0, e1, jnp.where(lane == 1, e2, jnp.where(lane == 2, rank1,
                     jnp.where(lane == 3, rank2, 0))))
    mi_ref[...] = meta.T[:8, :]
    mf_ref[...] = jnp.where(lane == 0, w1, jnp.where(lane == 1, w2, 0.0))
    cnt_ref[...] = carry[...].astype(jnp.int32)


def router(x, g, w_router_pad, tm=512):
    return pl.pallas_call(
        functools.partial(_router_kernel, tm=tm),
        grid=(N_TOK // tm,),
        in_specs=[pl.BlockSpec((tm, D_MODEL), lambda i: (i, 0)),
                  pl.BlockSpec((1, D_MODEL), lambda i: (0, 0)),
                  pl.BlockSpec((D_MODEL, LANES), lambda i: (0, 0))],
        out_specs=[pl.BlockSpec((8, tm), lambda i: (0, i)),
                   pl.BlockSpec((tm, LANES), lambda i: (i, 0)),
                   pl.BlockSpec((1, LANES), lambda i: (0, 0))],
        out_shape=[jax.ShapeDtypeStruct((8, N_TOK), jnp.int32),
                   jax.ShapeDtypeStruct((N_TOK, LANES), f32),
                   jax.ShapeDtypeStruct((1, LANES), jnp.int32)],
        scratch_shapes=[pltpu.VMEM((1, LANES), f32)],
        compiler_params=_cparams(("arbitrary",)),
        name="router",
    )(x, g, w_router_pad)


def _dispatch_kernel(p1_ref, p2_ref, pad0_ref, padn_ref, hp_ref, xs_out, zrow, sem, *, tm):
    base = pl.program_id(0) * tm

    @pl.when(pl.program_id(0) == 0)
    def _():
        zrow[...] = jnp.zeros(zrow.shape, zrow.dtype)

        def pad_copy(e, i):
            return pltpu.make_async_copy(zrow.at[pl.ds(0, 1)], xs_out.at[pl.ds(pad0_ref[e] + i, 1)], sem)

        def tail_copy(j):
            r = pl.multiple_of(j * FFN_SUB, FFN_SUB)
            return pltpu.make_async_copy(zrow, xs_out.at[pl.ds(r, FFN_SUB)], sem)

        tail0 = lax.shift_right_logical(pad0_ref[N_EXPERTS], FFN_SUB.bit_length() - 1)
        n_tail = xs_out.shape[0] // FFN_SUB
        for e in range(N_EXPERTS):
            lax.fori_loop(0, padn_ref[e], lambda i, c: (pad_copy(e, i).start(), c)[1], 0)
        lax.fori_loop(tail0, n_tail, lambda j, c: (tail_copy(j).start(), c)[1], 0)
        for e in range(N_EXPERTS):
            lax.fori_loop(0, padn_ref[e], lambda i, c: (pad_copy(e, i).wait(), c)[1], 0)
        lax.fori_loop(tail0, n_tail, lambda j, c: (tail_copy(j).wait(), c)[1], 0)

    def copy(g, u, pos_ref):
        r = pl.multiple_of(g * SUBLANES, SUBLANES) + u
        return pltpu.make_async_copy(hp_ref.at[pl.ds(r, 1)], xs_out.at[pl.ds(pos_ref[base + r], 1)], sem)

    def start(g, c):
        for u in range(SUBLANES):
            copy(g, u, p1_ref).start(priority=0)
            copy(g, u, p2_ref).start(priority=1)
        return c

    def wait(g, c):
        for u in range(SUBLANES):
            copy(g, u, p1_ref).wait()
            copy(g, u, p2_ref).wait()
        return c

    lax.fori_loop(0, tm // SUBLANES, start, 0)
    lax.fori_loop(0, tm // SUBLANES, wait, 0)


def dispatch(pos1, pos2, pad_start, pad_len, hp, tm=512):
    grid_spec = pltpu.PrefetchScalarGridSpec(
        num_scalar_prefetch=4,
        grid=(N_TOK // tm,),
        in_specs=[pl.BlockSpec((tm, HALF_D), lambda i, *_: (i, 0))],
        out_specs=pl.BlockSpec(memory_space=pl.ANY),
        scratch_shapes=[pltpu.VMEM((FFN_SUB, HALF_D), jnp.uint32), pltpu.SemaphoreType.DMA(())],
    )
    return pl.pallas_call(
        functools.partial(_dispatch_kernel, tm=tm),
        grid_spec=grid_spec,
        out_shape=jax.ShapeDtypeStruct((MOE_ROWS, HALF_D), jnp.uint32),
        compiler_params=_cparams(("arbitrary",)),
        name="dispatch",
    )(pos1, pos2, pad_start, pad_len, hp)


def _combine_kernel(p1_ref, p2_ref, x_ref, mf_ref, y_hbm, o_ref, buf, sem, *, tm):
    i = pl.program_id(0)

    def for_tile(t, fn):
        slot = t & 1
        base = t * tm

        def body(g, c):
            for u in range(SUBLANES):
                r = pl.multiple_of(g * SUBLANES, SUBLANES) + u
                for k, pos_ref in enumerate((p1_ref, p2_ref)):
                    fn(pltpu.make_async_copy(y_hbm.at[pl.ds(pos_ref[base + r], 1)],
                                             buf.at[slot, k, pl.ds(r, 1)], sem.at[slot]), k)
            return c
        lax.fori_loop(0, tm // SUBLANES, body, 0)

    @pl.when(i == 0)
    def _():
        for_tile(0, lambda c, k: c.start(priority=k))

    @pl.when(i + 1 < pl.num_programs(0))
    def _():
        for_tile(i + 1, lambda c, k: c.start(priority=k))

    for_tile(i, lambda c, k: c.wait())
    w = mf_ref[...]
    slot = i & 1
    o_ref[...] = x_ref[...] + w[:, 0:1] * buf[slot, 0] + w[:, 1:2] * buf[slot, 1]


def combine(pos1, pos2, x, mf, y, tm=256):
    grid_spec = pltpu.PrefetchScalarGridSpec(
        num_scalar_prefetch=2,
        grid=(N_TOK // tm,),
        in_specs=[pl.BlockSpec((tm, D_MODEL), lambda i, p1, p2: (i, 0)),
                  pl.BlockSpec((tm, LANES), lambda i, p1, p2: (i, 0)),
                  pl.BlockSpec(memory_space=pl.ANY)],
        out_specs=pl.BlockSpec((tm, D_MODEL), lambda i, p1, p2: (i, 0)),
        scratch_shapes=[pltpu.VMEM((2, TOP_K, tm, D_MODEL), f32), pltpu.SemaphoreType.DMA((2,))],
    )
    return pl.pallas_call(
        functools.partial(_combine_kernel, tm=tm),
        grid_spec=grid_spec,
        out_shape=jax.ShapeDtypeStruct((N_TOK, D_MODEL), f32),
        compiler_params=_cparams(("arbitrary",)),
        name="combine",
    )(pos1, pos2, x, mf, y)


def _rope_lanes(v):
    z = jnp.zeros(v.shape[:-1] + (QK_ROPE // 2,), v.dtype)
    return jnp.concatenate([v[..., :QK_ROPE // 2], z, v[..., QK_ROPE // 2:], z], axis=-1)


def kernel(x, mem, positions, g_mix, g_ffn, g_mem, w_mem_kv, g_mq, g_mk, w_out, a_w_in, a_conv_w, b_w_in,
           b_g_q_a, b_w_q_b, b_g_qn, g_kv, w_kv_a, g_kv_a, w_kv_b, g_kn, ffn_w_gate, ffn_w_up, ffn_w_down,
           moe_w_router, moe_w_gate, moe_w_up, moe_w_down):
    assert x.shape == (BATCH, SEQ, D_MODEL) and mem.shape == (BATCH, N_MEM, D_MODEL), (x.shape, mem.shape)
    assert positions.shape == (BATCH, SEQ) and moe_w_gate.shape == (1, N_EXPERTS, D_MODEL, D_FF)
    assert ffn_w_gate.shape == (1, D_MODEL, D_FF) and a_w_in.shape == (1, D_MODEL, 3 * MIX_WIDTH + MEM_WIDTH)
    row = lambda v: v.reshape(1, -1).astype(f32)
    x0 = x.reshape(N_TOK, D_MODEL)
    mem2 = mem.reshape(BATCH * N_MEM, D_MODEL)

    w_kva = jnp.concatenate([w_kv_a[:, :KV_LORA], _rope_lanes(w_kv_a[:, KV_LORA:])], axis=1).astype(bf16)
    kvb = w_kv_b.reshape(KV_LORA, N_HEADS, QK_NOPE + HEAD_DIM)
    w_kvb = jnp.concatenate([kvb[:, :, :QK_NOPE].reshape(KV_LORA, MIX_WIDTH),
                             kvb[:, :, QK_NOPE:].reshape(KV_LORA, MIX_WIDTH)], axis=1).astype(bf16)
    qb = b_w_q_b[0].reshape(Q_LORA, N_HEADS, QK_HEAD)
    w_qb = jnp.concatenate([qb[:, :, :QK_NOPE], _rope_lanes(qb[:, :, QK_NOPE:])], axis=-1)
    w_qb = w_qb.reshape(Q_LORA, N_HEADS * HEAD_PAD).astype(bf16)
    gk_n, gk_r = row(g_kn[:QK_NOPE]), row(_rope_lanes(g_kn[QK_NOPE:]))
    gq_n, gq_r = row(b_g_qn[0, :QK_NOPE]), row(_rope_lanes(b_g_qn[0, QK_NOPE:]))
    w_router_pad = jnp.pad(moe_w_router[0].astype(f32), ((0, 0), (0, LANES - N_EXPERTS)))

    inv_freq = 1.0 / (ROPE_THETA ** (jnp.arange(0, QK_ROPE, 2, dtype=f32) / QK_ROPE))
    half = jnp.ones((QK_ROPE // 2,), f32)
    rope_freq = row(_rope_lanes(jnp.concatenate([inv_freq, inv_freq])))
    rope_sign = row(_rope_lanes(jnp.concatenate([-half, half])))
    pos = positions.reshape(N_TOK, 1).astype(jnp.int32)

    u0 = norm_matmul(x0, row(g_mix[0]), a_w_in[0], tm=1024, tn=1280)
    kvm0 = norm_matmul(mem2, row(g_mem[0]), w_mem_kv[0], tm=BATCH * N_MEM, tn=2 * MEM_WIDTH)
    xa, hp0 = conv_tail(u0, a_conv_w[0].astype(f32), kvm0, row(g_mq[0]), row(g_mk[0]),
                        w_out[0].astype(bf16), x0, row(g_ffn[0]))
    dense_tbl = _super_tiles(jnp.array([N_TOK], jnp.int32), jnp.array([N_TOK], jnp.int32), jnp.array([0], jnp.int32),
                             -(-N_TOK // (FFN_SUB * FFN_SUPER)))
    y0 = swiglu_ffn(hp0, ffn_w_gate, ffn_w_up, ffn_w_down, *dense_tbl)

    x1, k, v, q, um1 = mla_kvq(xa, y0, pos, rope_freq, rope_sign, row(g_kv), w_kva, row(g_kv_a), w_kvb, gk_n, gk_r,
                               row(g_mix[1]), b_w_in[0].astype(bf16), row(b_g_q_a[0]), w_qb, gq_n, gq_r)
    mix1 = flash_attention(q, k, v)
    kvm1 = norm_matmul(mem2, row(g_mem[1]), w_mem_kv[1], tm=BATCH * N_MEM, tn=2 * MEM_WIDTH)
    x2, hp1 = attn_tail(mix1, um1, 0, kvm1, row(g_mq[1]), row(g_mk[1]), w_out[1].astype(bf16), x1, row(g_ffn[1]))

    mi, mf, cnt = router(x2, row(g_ffn[1]), w_router_pad)
    counts = cnt[0, :N_EXPERTS]
    padded = (counts + FFN_SUB - 1) // FFN_SUB * FFN_SUB
    offsets = jnp.cumsum(padded) - padded
    pos1 = offsets[mi[0]] + mi[2]
    pos2 = offsets[mi[1]] + mi[3]
    pad_start = jnp.concatenate([offsets + counts, jnp.sum(padded, keepdims=True)])
    xs = dispatch(pos1, pos2, pad_start, padded - counts, hp1)
    y1 = swiglu_ffn(xs, moe_w_gate[0], moe_w_up[0], moe_w_down[0], *_super_tiles(counts, padded, offsets, MOE_SUPERS))
    out = combine(pos1, pos2, x2, mf, y1)
    return out.reshape(BATCH, SEQ, D_MODEL)
```
